```python
import jax, jax.numpy as jnp
from jax import lax
import numpy as np

D_MODEL = 1024
BATCH = 4
SEQ = 8192
DEPTH = 2

GRID_W = 64
CTX_LEN = 256
HEAD_DIM = 64
ATT_HEADS = (D_MODEL // 2) // HEAD_DIM
ATT_KV_HEADS = ATT_HEADS // 4
ATT_GROUP = ATT_HEADS // ATT_KV_HEADS
ATT_WIDTH = ATT_HEADS * HEAD_DIM
KV_WIDTH = ATT_KV_HEADS * HEAD_DIM
WINDOW = 128
BLOCK = 128
ROPE_BASE = 10000.0
RWKV_HEADS = (D_MODEL // 2) // HEAD_DIM
RWKV_WIDTH = RWKV_HEADS * HEAD_DIM
DECAY_LORA = 64
ICLR_LORA = 64
GATE_LORA = 128
MLP_HIDDEN = 4 * D_MODEL
N_EVEN = (DEPTH + 1) // 2
N_ODD = DEPTH // 2
NORM_EPS = 1e-6
GN_EPS = 64e-5
ATT_SPLITS = (ATT_WIDTH, KV_WIDTH, KV_WIDTH)
RWKV_SPLITS = (RWKV_WIDTH, RWKV_WIDTH, RWKV_WIDTH, DECAY_LORA, DECAY_LORA, ICLR_LORA, ICLR_LORA, GATE_LORA)
ATT_IN_WIDTH = ATT_WIDTH + 2 * KV_WIDTH
RWKV_IN_WIDTH = 3 * RWKV_WIDTH + 2 * DECAY_LORA + 2 * ICLR_LORA + GATE_LORA
AB_IN_WIDTH = ATT_IN_WIDTH + RWKV_IN_WIDTH
MIX_WIDTH = ATT_WIDTH + RWKV_WIDTH
F32 = jnp.float32

kernel_name = 'hybrid_swa_rwkv7_shortconv_dit'


def _split(u, sizes):
    out, start = [], 0
    for s in sizes:
        out.append(u[..., start:start + s])
        start += s
    return out


def rms_norm(u, g):
    uf = u.astype(F32)
    y = uf * lax.rsqrt(jnp.mean(uf * uf, axis=-1, keepdims=True) + NORM_EPS)
    return (y * g.astype(F32)).astype(u.dtype)


def shift_centred(u):
    prev = jnp.pad(u[:, :-1], ((0, 0), (1, 0), (0, 0)))
    nxt = jnp.pad(u[:, 1:], ((0, 0), (0, 1), (0, 0)))
    return 0.5 * (prev + nxt)


def conv3_centred(u, w):
    prev = jnp.pad(u[:, :-1], ((0, 0), (1, 0), (0, 0)))
    nxt = jnp.pad(u[:, 1:], ((0, 0), (0, 1), (0, 0)))
    return prev * w[0] + u * w[1] + nxt * w[2]


def rope_1d(u, pos):
    m = u.shape[-1] // 2
    inv = ROPE_BASE ** (-jnp.arange(m, dtype=F32) / m)
    ang = pos.astype(F32)[:, None] * inv[None, :]
    cos = jnp.cos(ang)[None, :, None, :]
    sin = jnp.sin(ang)[None, :, None, :]
    uf = u.astype(F32)
    u1, u2 = uf[..., :m], uf[..., m:]
    return jnp.concatenate([u1 * cos - u2 * sin, u2 * cos + u1 * sin], axis=-1).astype(u.dtype)


def rope_2d(u, row, col):
    h = u.shape[-1] // 2
    return jnp.concatenate([rope_1d(u[..., :h], row), rope_1d(u[..., h:], col)], axis=-1)


def window_attention(q, k, v, kc, vc, sink):
    B, S = q.shape[0], q.shape[1]
    L = kc.shape[1]
    nb = S // BLOCK
    n_loc = 3 * BLOCK
    scale = HEAD_DIM ** -0.5
    qb = q.reshape(B, nb, BLOCK, ATT_KV_HEADS, ATT_GROUP, HEAD_DIM).transpose(1, 0, 2, 3, 4, 5)
    pad = ((0, 0), (BLOCK, BLOCK), (0, 0), (0, 0))
    kp, vp = jnp.pad(k, pad), jnp.pad(v, pad)
    sink_l = sink.astype(F32).reshape(1, ATT_KV_HEADS, ATT_GROUP, 1, 1)
    rel = jnp.arange(BLOCK)[:, None] + BLOCK - jnp.arange(n_loc)[None, :]
    band = jnp.abs(rel) <= WINDOW

    def one_block(args):
        bi, qblk = args
        start = bi * BLOCK
        kb = lax.dynamic_slice_in_dim(kp, start, n_loc, axis=1)
        vb = lax.dynamic_slice_in_dim(vp, start, n_loc, axis=1)
        kpos = start - BLOCK + jnp.arange(n_loc)
        valid = band & ((kpos >= 0) & (kpos < S))[None, :]
        s_loc = jnp.einsum('bqhgd,bkhd->bhgqk', qblk, kb).astype(F32) * scale
        s_loc = jnp.where(valid, s_loc, -jnp.inf)
        s_ctx = jnp.einsum('bqhgd,bkhd->bhgqk', qblk, kc).astype(F32) * scale
        s_sink = jnp.broadcast_to(sink_l, s_loc.shape[:-1] + (1,))
        p = jax.nn.softmax(jnp.concatenate([s_loc, s_ctx, s_sink], axis=-1), axis=-1).astype(v.dtype)
        return (jnp.einsum('bhgqk,bkhd->bqhgd', p[..., :n_loc], vb)
                + jnp.einsum('bhgqk,bkhd->bqhgd', p[..., n_loc:n_loc + L], vc))

    out = lax.map(one_block, (jnp.arange(nb), qb))
    return out.transpose(1, 0, 2, 3, 4, 5).reshape(B, S, ATT_WIDTH)


def context_attention(qc, kc, vc, sink):
    B, L = qc.shape[0], qc.shape[1]
    qg = qc.reshape(B, L, ATT_KV_HEADS, ATT_GROUP, HEAD_DIM)
    s = jnp.einsum('bqhgd,bkhd->bhgqk', qg, kc).astype(F32) * HEAD_DIM ** -0.5
    s_sink = jnp.broadcast_to(sink.astype(F32).reshape(1, ATT_KV_HEADS, ATT_GROUP, 1, 1), s.shape[:-1] + (1,))
    p = jax.nn.softmax(jnp.concatenate([s, s_sink], axis=-1), axis=-1).astype(vc.dtype)
    return jnp.einsum('bhgqk,bkhd->bqhgd', p[..., :L], vc).reshape(B, L, ATT_WIDTH)


def _heads(u):
    return u.reshape(u.shape[:-1] + (RWKV_HEADS, HEAD_DIM))


def rwkv_features(pr, lp):
    pr = pr.astype(F32)
    pr = pr + lp['mu'] * (shift_centred(pr) - pr)
    r, k, v, wf, wb, af, ab, g = _split(pr, RWKV_SPLITS)
    kk = _heads(k * lp['kk'])
    kk = kk / jnp.maximum(jnp.sqrt(jnp.sum(kk * kk, axis=-1, keepdims=True)), 1e-12)
    decays, ks, iclrs = [], [], []
    for d, (wd, ad) in enumerate(((wf, af), (wb, ab))):
        w_log = -jax.nn.softplus(-(lp['w0'][d] + jnp.tanh(wd) @ lp['w2'][d])) - 0.5
        a = jax.nn.sigmoid(lp['a0'][d] + ad @ lp['a2'][d])
        decays.append(_heads(jnp.exp(-jnp.exp(w_log))))
        ks.append(_heads(k * (1.0 + (a - 1.0) * lp['ka'])))
        iclrs.append(_heads(a))
    gate = jax.nn.sigmoid(g) @ lp['g2']
    return dict(r=_heads(r), v=_heads(v), kk=kk, gate=gate, decay=decays, k=ks, a=iclrs)


def rwkv_scan(state0, f, d, emit, reverse):
    xs = tuple(jnp.moveaxis(t, 1, 0) for t in (f['decay'][d], f['k'][d], f['v'], -f['kk'], f['kk'] * f['a'][d]))

    def update(S, w_t, k_t, v_t, a_t, b_t):
        sa = jnp.einsum('bhvk,bhk->bhv', S, a_t)
        return S * w_t[:, :, None, :] + sa[..., None] * b_t[:, :, None, :] + v_t[..., None] * k_t[:, :, None, :]

    if not emit:
        s_fin, _ = lax.scan(lambda S, inp: (update(S, *inp), None), state0, xs, reverse=reverse)
        return s_fin, None

    def step(S, inp):
        S = update(S, *inp[1:])
        return S, jnp.einsum('bhvk,bhk->bhv', S, inp[0])

    s_fin, ys = lax.scan(step, state0, (jnp.moveaxis(f['r'], 1, 0),) + xs, reverse=reverse)
    return s_fin, jnp.moveaxis(ys, 0, 1)


def rwkv_output(y, f, lp):
    mean = jnp.mean(y, axis=-1, keepdims=True)
    var = jnp.mean(jnp.square(y - mean), axis=-1, keepdims=True)
    yn = ((y - mean) * lax.rsqrt(var + GN_EPS)).reshape(y.shape[:2] + (RWKV_WIDTH,))
    yn = yn * lp['ln_w'] + lp['ln_b']
    bonus = sum(jnp.sum(f['r'] * kd * lp['rk'], axis=-1, keepdims=True) * f['v'] for kd in f['k'])
    return (yn + bonus.reshape(yn.shape)) * f['gate']


def ab_mixer(hl, hc, row, col, lp, need_ctx_out):
    B, S = hl.shape[0], hl.shape[1]
    L = hc.shape[1]
    pl = hl @ lp['w_in']
    pc = hc @ lp['w_in']
    ql, kl, vl = _split(pl[..., :ATT_IN_WIDTH], ATT_SPLITS)
    qc, kc, vc = _split(pc[..., :ATT_IN_WIDTH], ATT_SPLITS)
    ql = rope_2d(ql.reshape(B, S, ATT_HEADS, HEAD_DIM), row, col)
    kl = rope_2d(kl.reshape(B, S, ATT_KV_HEADS, HEAD_DIM), row, col)
    vl = vl.reshape(B, S, ATT_KV_HEADS, HEAD_DIM)
    kc = kc.reshape(B, L, ATT_KV_HEADS, HEAD_DIM)
    vc = vc.reshape(B, L, ATT_KV_HEADS, HEAD_DIM)
    att_l = window_attention(ql, kl, vl, kc, vc, lp['sink'])

    fl = rwkv_features(pl[..., ATT_IN_WIDTH:], lp)
    fc = rwkv_features(pc[..., ATT_IN_WIDTH:], lp)
    state0 = jnp.zeros((B, RWKV_HEADS, HEAD_DIM, HEAD_DIM), F32)
    ys_l, ys_c = [], []
    for d in range(2):
        rev = d == 1
        s_ctx, yc_d = rwkv_scan(state0, fc, d, need_ctx_out, rev)
        _, yl_d = rwkv_scan(s_ctx, fl, d, True, rev)
        ys_l.append(yl_d)
        ys_c.append(yc_d)
    rw_l = rwkv_output(ys_l[0] + ys_l[1], fl, lp)
    out_l = jnp.concatenate([att_l, rw_l.astype(att_l.dtype)], axis=-1) @ lp['w_out']
    if not need_ctx_out:
        return out_l, None
    att_c = context_attention(qc.reshape(B, L, ATT_HEADS, HEAD_DIM), kc, vc, lp['sink'])
    rw_c = rwkv_output(ys_c[0] + ys_c[1], fc, lp)
    out_c = jnp.concatenate([att_c, rw_c.astype(att_c.dtype)], axis=-1) @ lp['w_out']
    return out_l, out_c


def conv_mixer(h, w_in, conv_w, w_out):
    gate_b, gate_c, u = jnp.split(h @ w_in, 3, axis=-1)
    return (gate_b * conv3_centred(gate_c * u, conv_w)) @ w_out


def sq_relu_mlp(h, w1, w2):
    return jnp.square(jax.nn.relu(h @ w1)) @ w2


def setup_inputs(seed: int = 0) -> dict:
    key = jax.random.key(seed)
    ks = jax.random.split(key, 26)
    D = D_MODEL

    def nrm(k, shape, s):
        return jax.random.normal(k, shape, F32) * s

    return {
        'x': nrm(ks[0], (BATCH, SEQ, D), 1.0),
        'c': nrm(ks[1], (BATCH, D), 1.0),
        'ctx': nrm(ks[2], (BATCH, CTX_LEN, D), 1.0),
        'c_ctx': nrm(ks[3], (D,), 1.0),
        'mod_w': nrm(ks[4], (DEPTH, D, 6 * D), 0.5 * D ** -0.5),
        'mod_b': nrm(ks[5], (DEPTH, 6 * D), 0.02),
        'norm_g': 1.0 + nrm(ks[6], (DEPTH, 4, D), 0.02),
        'mlp_w1': nrm(ks[7], (DEPTH, D, MLP_HIDDEN), D ** -0.5),
        'mlp_w2': nrm(ks[8], (DEPTH, MLP_HIDDEN, D), MLP_HIDDEN ** -0.5),
        'ab_w_in': nrm(ks[9], (N_EVEN, D, AB_IN_WIDTH), D ** -0.5),
        'ab_w_out': nrm(ks[10], (N_EVEN, MIX_WIDTH, D), MIX_WIDTH ** -0.5),
        'att_sink': nrm(ks[11], (N_EVEN, ATT_HEADS), 0.5),
        'rwkv_mu': jax.random.uniform(ks[12], (N_EVEN, RWKV_IN_WIDTH), F32, 0.0, 1.0),
        'rwkv_w0': nrm(ks[13], (N_EVEN, 2, RWKV_WIDTH), 0.5),
        'rwkv_w2': nrm(ks[14], (N_EVEN, 2, DECAY_LORA, RWKV_WIDTH), 0.5 * DECAY_LORA ** -0.5),
        'rwkv_a0': nrm(ks[15], (N_EVEN, 2, RWKV_WIDTH), 0.5),
        'rwkv_a2': nrm(ks[16], (N_EVEN, 2, ICLR_LORA, RWKV_WIDTH), 0.5 * ICLR_LORA ** -0.5),
        'rwkv_g2': nrm(ks[17], (N_EVEN, GATE_LORA, RWKV_WIDTH), GATE_LORA ** -0.5),
        'rwkv_kk': 0.85 + nrm(ks[18], (N_EVEN, RWKV_WIDTH), 0.05),
        'rwkv_ka': 1.0 + nrm(ks[19], (N_EVEN, RWKV_WIDTH), 0.05),
        'rwkv_rk': nrm(ks[20], (N_EVEN, RWKV_HEADS, HEAD_DIM), 0.1),
        'rwkv_ln_w': 1.0 + nrm(ks[21], (N_EVEN, RWKV_WIDTH), 0.02),
        'rwkv_ln_b': nrm(ks[22], (N_EVEN, RWKV_WIDTH), 0.02),
        'conv_w_in': nrm(ks[23], (N_ODD, D, 3 * D), D ** -0.5),
        'conv_w': nrm(ks[24], (N_ODD, 3, D), 3.0 ** -0.5),
        'conv_w_out': nrm(ks[25], (N_ODD, D, D), D ** -0.5),
    }


def reference(x, c, ctx, c_ctx, mod_w, mod_b, norm_g, mlp_w1, mlp_w2, ab_w_in, ab_w_out, att_sink,
              rwkv_mu, rwkv_w0, rwkv_w2, rwkv_a0, rwkv_a2, rwkv_g2, rwkv_kk, rwkv_ka, rwkv_rk,
              rwkv_ln_w, rwkv_ln_b, conv_w_in, conv_w, conv_w_out):
    S = x.shape[1]
    rows = S // GRID_W
    t = jnp.arange(rows * GRID_W)
    row, col = t // GRID_W, t % GRID_W
    last_even = DEPTH - 1 if (DEPTH - 1) % 2 == 0 else DEPTH - 2
    xl, xc = x, ctx
    for i in range(DEPTH):
        need_ctx = i < last_even
        ctx_live = (i % 2 == 0) or need_ctx
        mod_l = (jax.nn.silu(c) @ mod_w[i] + mod_b[i])[:, None, :]
        sh_a, sc_a, gt_a, sh_m, sc_m, gt_m = jnp.split(mod_l, 6, axis=-1)
        hl = rms_norm(xl, norm_g[i, 0]) * (1 + sc_a) + sh_a
        hc = None
        if ctx_live:
            mod_c = (jax.nn.silu(c_ctx) @ mod_w[i] + mod_b[i])[None, None, :]
            csh_a, csc_a, cgt_a, csh_m, csc_m, cgt_m = jnp.split(mod_c, 6, axis=-1)
            hc = rms_norm(xc, norm_g[i, 0]) * (1 + csc_a) + csh_a
        if i % 2 == 0:
            e = i // 2
            lp = dict(w_in=ab_w_in[e], w_out=ab_w_out[e], sink=att_sink[e], mu=rwkv_mu[e],
                      w0=rwkv_w0[e], w2=rwkv_w2[e], a0=rwkv_a0[e], a2=rwkv_a2[e], g2=rwkv_g2[e],
                      kk=rwkv_kk[e], ka=rwkv_ka[e], rk=rwkv_rk[e], ln_w=rwkv_ln_w[e], ln_b=rwkv_ln_b[e])
            yl, yc = ab_mixer(hl, hc, row, col, lp, need_ctx)
        else:
            o = i // 2
            yl = conv_mixer(hl, conv_w_in[o], conv_w[o], conv_w_out[o])
            yc = conv_mixer(hc, conv_w_in[o], conv_w[o], conv_w_out[o]) if need_ctx else None
        xl = xl + gt_a * rms_norm(yl, norm_g[i, 1])
        hm = rms_norm(xl, norm_g[i, 2]) * (1 + sc_m) + sh_m
        xl = xl + gt_m * rms_norm(sq_relu_mlp(hm, mlp_w1[i], mlp_w2[i]), norm_g[i, 3])
        if need_ctx:
            xc = xc + cgt_a * rms_norm(yc, norm_g[i, 1])
            hmc = rms_norm(xc, norm_g[i, 2]) * (1 + csc_m) + csh_m
            xc = xc + cgt_m * rms_norm(sq_relu_mlp(hmc, mlp_w1[i], mlp_w2[i]), norm_g[i, 3])
    return xl
```

```python
import functools
import math

import jax
import jax.numpy as jnp
from jax import lax
from jax.experimental import pallas as pl
from jax.experimental.pallas import tpu as pltpu

F32 = jnp.float32
BF16 = jnp.bfloat16

HEAD_DIM = 64
GRID_W = 64
WINDOW = 128
ATT_BLOCK = 128
ATT_GROUP = 4
ROPE_BASE = 10000.0
NORM_EPS = 1e-6
GN_EPS = 64e-5
LANES = 128
SUBLANES = 8
CHUNK = 64
NEUMANN_STEPS = 6
DECAY_SCALE = math.exp(-0.5)
VMEM_LIMIT = 56 * 1024 * 1024


def _cparams(*sem):
    return pltpu.CompilerParams(dimension_semantics=sem, vmem_limit_bytes=VMEM_LIMIT)


def _const_spec(shape):
    nd = len(shape)
    return pl.BlockSpec(shape, lambda *_: (0,) * nd, pipeline_mode=pl.Buffered(1))


def _dot(a, b):
    return jnp.dot(a.astype(BF16), b.astype(BF16), preferred_element_type=F32)


def _dot_nt(a, b):
    return lax.dot_general(a.astype(BF16), b.astype(BF16), (((1,), (1,)), ((), ())), preferred_element_type=F32)


def _dot_tn(a, b):
    return lax.dot_general(a.astype(BF16), b.astype(BF16), (((0,), (0,)), ((), ())), preferred_element_type=F32)


def _split2(x):
    hi = x.astype(BF16)
    lo = (x - hi.astype(F32)).astype(BF16)
    return hi, lo


def _split3(x):
    hi = x.astype(BF16)
    r1 = x - hi.astype(F32)
    mid = r1.astype(BF16)
    lo = (r1 - mid.astype(F32)).astype(BF16)
    return hi, mid, lo


def _head_sum(x, ones_bd):
    hi, lo = _split2(x)
    return (jnp.dot(hi, ones_bd, preferred_element_type=F32)
            + jnp.dot(lo, ones_bd, preferred_element_type=F32))


def _rms(u, g):
    return u * lax.rsqrt(jnp.mean(u * u, axis=-1, keepdims=True) + NORM_EPS) * g


def _sigmoid(z):
    return 1.0 / (1.0 + jnp.exp(-z))


def _mod_kernel(cv_ref, w_ref, b_ref, o_ref):
    cv = cv_ref[...]
    s = cv * _sigmoid(cv)
    o_ref[...] = _dot(s, w_ref[...]) + b_ref[...]


def _modulation(cv, mod_w, mod_b):
    depth, d, six_d = mod_w.shape
    rows = cv.shape[0]
    nj = six_d // d
    return pl.pallas_call(
        _mod_kernel,
        grid=(depth, nj),
        in_specs=[pl.BlockSpec((rows, d), lambda l, j: (0, 0)),
                  pl.BlockSpec((None, d, d), lambda l, j: (l, 0, j)),
                  pl.BlockSpec((None, 1, d), lambda l, j: (l, 0, j))],
        out_specs=pl.BlockSpec((None, rows, d), lambda l, j: (l, 0, j)),
        out_shape=jax.ShapeDtypeStruct((depth, rows, six_d), F32),
        compiler_params=_cparams("arbitrary", "arbitrary"),
    )(cv, mod_w, mod_b.reshape(depth, 1, six_d))


def _proj_kernel(x_ref, mod_ref, g_ref, w_ref, cos_ref, sin_ref, q_ref, k_ref, v_ref, pr_ref, *, att_w, kv_w):
    m = mod_ref[...]
    h = (_rms(x_ref[...], g_ref[...]) * (1.0 + m[1:2]) + m[0:1]).astype(BF16)
    cos, sin = cos_ref[...], sin_ref[...]
    scale = HEAD_DIM ** -0.5
    o = 0
    for j in range(att_w // LANES):
        u = jnp.dot(h, w_ref[:, o + j * LANES:o + (j + 1) * LANES], preferred_element_type=F32)
        ur = jnp.dot(h, w_ref[:, o + att_w + j * LANES:o + att_w + (j + 1) * LANES], preferred_element_type=F32)
        q_ref[:, j * LANES:(j + 1) * LANES] = ((u * cos + ur * sin) * scale).astype(q_ref.dtype)
    o += 2 * att_w
    for j in range(kv_w // LANES):
        u = jnp.dot(h, w_ref[:, o + j * LANES:o + (j + 1) * LANES], preferred_element_type=F32)
        ur = jnp.dot(h, w_ref[:, o + kv_w + j * LANES:o + kv_w + (j + 1) * LANES], preferred_element_type=F32)
        k_ref[:, j * LANES:(j + 1) * LANES] = (u * cos + ur * sin).astype(k_ref.dtype)
    o += 2 * kv_w
    v_ref[...] = jnp.dot(h, w_ref[:, o:o + kv_w], preferred_element_type=F32).astype(v_ref.dtype)
    o += kv_w
    pr_ref[...] = jnp.dot(h, w_ref[:, o:], preferred_element_type=F32)


def _proj_ctx_kernel(x_ref, mod_ref, g_ref, w_ref, k_ref, v_ref, pr_ref, *, kv_w):
    m = mod_ref[...]
    h = (_rms(x_ref[...], g_ref[...]) * (1.0 + m[1:2]) + m[0:1]).astype(BF16)
    k_ref[...] = jnp.dot(h, w_ref[:, 0:kv_w], preferred_element_type=F32).astype(k_ref.dtype)
    v_ref[...] = jnp.dot(h, w_ref[:, kv_w:2 * kv_w], preferred_element_type=F32).astype(v_ref.dtype)
    pr_ref[...] = jnp.dot(h, w_ref[:, 2 * kv_w:], preferred_element_type=F32)


def _rot_cols(w):
    d, n = w.shape
    m = HEAD_DIM // 4
    w4 = w.reshape(d, n // (2 * m), 2, m)
    return jnp.stack([-w4[:, :, 1], w4[:, :, 0]], axis=2).reshape(d, n)


def _rope_tables(seq):
    m = HEAD_DIM // 4
    t = jnp.arange(seq)
    inv = ROPE_BASE ** (-jnp.arange(m, dtype=F32) / m)
    ang_r = (t // GRID_W).astype(F32)[:, None] * inv[None, :]
    ang_c = (t % GRID_W).astype(F32)[:, None] * inv[None, :]
    ang = jnp.concatenate([ang_r, ang_r, ang_c, ang_c], axis=-1)
    ang = jnp.tile(ang, (1, LANES // HEAD_DIM))
    return jnp.cos(ang), jnp.sin(ang)


def _project_latent(x, mod_l, g, w_all, cos, sin, att_w, kv_w, rw_w, tm):
    b, s, d = x.shape
    n_all = w_all.shape[1]
    kern = functools.partial(_proj_kernel, att_w=att_w, kv_w=kv_w)
    return pl.pallas_call(
        kern,
        grid=(b, s // tm),
        in_specs=[pl.BlockSpec((None, tm, d), lambda bi, i: (bi, i, 0)),
                  pl.BlockSpec((None, 6, d), lambda bi, i: (bi, 0, 0)),
                  _const_spec((1, d)),
                  _const_spec((d, n_all)),
                  pl.BlockSpec((tm, LANES), lambda bi, i: (i, 0)),
                  pl.BlockSpec((tm, LANES), lambda bi, i: (i, 0))],
        out_specs=[pl.BlockSpec((None, tm, att_w), lambda bi, i: (bi, i, 0)),
                   pl.BlockSpec((None, tm, kv_w), lambda bi, i: (bi, i, 0)),
                   pl.BlockSpec((None, tm, kv_w), lambda bi, i: (bi, i, 0)),
                   pl.BlockSpec((None, tm, rw_w), lambda bi, i: (bi, i, 0))],
        out_shape=[jax.ShapeDtypeStruct((b, s, att_w), BF16),
                   jax.ShapeDtypeStruct((b, s, kv_w), BF16),
                   jax.ShapeDtypeStruct((b, s, kv_w), BF16),
                   jax.ShapeDtypeStruct((b, s, rw_w), F32)],
        compiler_params=_cparams("parallel", "parallel"),
    )(x, mod_l, g, w_all, cos, sin)


def _project_ctx(ctx, mod_c, g, w_ctx, kv_w, rw_w):
    b, l, d = ctx.shape
    kern = functools.partial(_proj_ctx_kernel, kv_w=kv_w)
    return pl.pallas_call(
        kern,
        grid=(b,),
        in_specs=[pl.BlockSpec((None, l, d), lambda bi: (bi, 0, 0)),
                  _const_spec((6, d)),
                  _const_spec((1, d)),
                  _const_spec((d, w_ctx.shape[1]))],
        out_specs=[pl.BlockSpec((None, l, kv_w), lambda bi: (bi, 0, 0)),
                   pl.BlockSpec((None, l, kv_w), lambda bi: (bi, 0, 0)),
                   pl.BlockSpec((None, l, rw_w), lambda bi: (bi, 0, 0))],
        out_shape=[jax.ShapeDtypeStruct((b, l, kv_w), BF16),
                   jax.ShapeDtypeStruct((b, l, kv_w), BF16),
                   jax.ShapeDtypeStruct((b, l, rw_w), F32)],
        compiler_params=_cparams("parallel"),
    )(ctx, mod_c, g, w_ctx)


def _attn_kernel(sink_ref, q_ref, kp_ref, kc_ref, kn_ref, vp_ref, vc_ref, vn_ref, kx_ref, vx_ref, o_ref, *, n_heads):
    i = pl.program_id(1)
    nb = pl.num_programs(1)
    blk = ATT_BLOCK
    kloc = jnp.concatenate([kp_ref[...], kc_ref[...], kn_ref[...]], axis=0)
    vloc = jnp.concatenate([vp_ref[...], vc_ref[...], vn_ref[...]], axis=0)
    kctx, vctx = kx_ref[...], vx_ref[...]
    row = lax.broadcasted_iota(jnp.int32, (blk, 3 * blk), 0)
    col = lax.broadcasted_iota(jnp.int32, (blk, 3 * blk), 1)
    rel = row + blk - col
    valid = (jnp.abs(rel) <= WINDOW)
    valid = valid & jnp.logical_not((i == 0) & (col < blk))
    valid = valid & jnp.logical_not((i == nb - 1) & (col >= 2 * blk))
    q = q_ref[...]
    for hq in range(n_heads):
        hk = hq // ATT_GROUP
        qh = q[:, hq * HEAD_DIM:(hq + 1) * HEAD_DIM]
        kh = kloc[:, hk * HEAD_DIM:(hk + 1) * HEAD_DIM]
        vh = vloc[:, hk * HEAD_DIM:(hk + 1) * HEAD_DIM]
        kxh = kctx[:, hk * HEAD_DIM:(hk + 1) * HEAD_DIM]
        vxh = vctx[:, hk * HEAD_DIM:(hk + 1) * HEAD_DIM]
        s_loc = jnp.where(valid, _dot_nt(qh, kh), -1e30)
        s_ctx = _dot_nt(qh, kxh)
        sink = sink_ref[hq]
        mx = jnp.maximum(jnp.maximum(jnp.max(s_loc, axis=-1, keepdims=True),
                                     jnp.max(s_ctx, axis=-1, keepdims=True)), sink)
        p_loc = jnp.exp(s_loc - mx)
        p_ctx = jnp.exp(s_ctx - mx)
        den = (jnp.sum(p_loc, axis=-1, keepdims=True) + jnp.sum(p_ctx, axis=-1, keepdims=True)
               + jnp.exp(sink - mx))
        o = (_dot(p_loc, vh) + _dot(p_ctx, vxh)) / den
        o_ref[:, hq * HEAD_DIM:(hq + 1) * HEAD_DIM] = o.astype(o_ref.dtype)


def _attention(q, k, v, kc, vc, sink):
    b, s, att_w = q.shape
    kv_w = k.shape[2]
    l = kc.shape[1]
    nb = s // ATT_BLOCK
    blk = ATT_BLOCK
    kern = functools.partial(_attn_kernel, n_heads=att_w // HEAD_DIM)
    kv_prev = pl.BlockSpec((None, blk, kv_w), lambda bi, i: (bi, jnp.maximum(i - 1, 0), 0))
    kv_cur = pl.BlockSpec((None, blk, kv_w), lambda bi, i: (bi, i, 0))
    kv_next = pl.BlockSpec((None, blk, kv_w), lambda bi, i: (bi, jnp.minimum(i + 1, nb - 1), 0))
    kv_ctx = pl.BlockSpec((None, l, kv_w), lambda bi, i: (bi, 0, 0))
    return pl.pallas_call(
        kern,
        grid=(b, nb),
        in_specs=[pl.BlockSpec(memory_space=pltpu.SMEM),
                  pl.BlockSpec((None, blk, att_w), lambda bi, i: (bi, i, 0)),
                  kv_prev, kv_cur, kv_next, kv_prev, kv_cur, kv_next, kv_ctx, kv_ctx],
        out_specs=pl.BlockSpec((None, blk, att_w), lambda bi, i: (bi, i, 0)),
        out_shape=jax.ShapeDtypeStruct((b, s, att_w), BF16),
        compiler_params=_cparams("parallel", "parallel"),
    )(sink, q, k, k, k, v, v, v, kc, vc)


def _stack_heads(x, even):
    return jnp.concatenate([jnp.where(even, x, 0.0), jnp.where(even, 0.0, x)], axis=0)


def _chunk_summary(rr, vv, aa, bb, kd, cs, lw, tot, forward):
    c2 = 2 * CHUNK
    even = lax.broadcasted_iota(jnp.int32, (CHUNK, LANES), 1) < HEAD_DIM
    a_t = aa * jnp.exp(cs - lw)
    g_inv = jnp.exp(-cs)
    b_t = bb * g_inv
    k_t = kd * g_inv
    r_t = rr * jnp.exp(cs)
    g_rem = jnp.exp(tot - cs)
    b_b = bb * g_rem
    k_b = kd * g_rem

    a_s = _stack_heads(a_t, even)
    r_s = _stack_heads(r_t, even)
    v_s = _stack_heads(vv, even)
    bb_s = _stack_heads(b_b, even)
    kb_s = _stack_heads(k_b, even)

    lhs = jnp.concatenate([a_s, r_s], axis=0)
    rhs = jnp.concatenate([b_t, b_t, k_t, k_t], axis=0)
    g = _dot_nt(lhs, rhs)

    row = lax.broadcasted_iota(jnp.int32, (c2, c2), 0)
    col = lax.broadcasted_iota(jnp.int32, (c2, c2), 1)
    same = (row // CHUNK) == (col // CHUNK)
    rt, ct = row % CHUNK, col % CHUNK
    if forward:
        strict, incl = same & (ct < rt), same & (ct <= rt)
    else:
        strict, incl = same & (ct > rt), same & (ct >= rt)
    l_ab = jnp.where(strict, g[:c2, :c2], 0.0)
    l_ak = jnp.where(strict, g[:c2, c2:], 0.0)
    m_rb = jnp.where(incl, g[c2:, :c2], 0.0)
    m_rk = jnp.where(incl, g[c2:, c2:], 0.0)

    x = jnp.concatenate([a_s, _dot(l_ak, v_s)], axis=1)
    lp = l_ab
    for step in range(NEUMANN_STEPS):
        x = x + _dot(lp, x)
        if step + 1 < NEUMANN_STEPS:
            lp = _dot(lp, lp)
    x_a, x_u = x[:, :LANES], x[:, LANES:]

    ry = jnp.concatenate([r_s, _dot(m_rk, v_s)], axis=1) + _dot(m_rb, x)
    rh = ry[:CHUNK, :LANES] + ry[CHUNK:, :LANES]
    yh = ry[:CHUNK, LANES:] + ry[CHUNK:, LANES:]

    eye = lax.broadcasted_iota(jnp.int32, (LANES, LANES), 0) == lax.broadcasted_iota(jnp.int32, (LANES, LANES), 1)
    pt = jnp.where(eye, jnp.exp(tot), 0.0) + _dot_tn(bb_s, x_a)
    qt = _dot_tn(jnp.concatenate([bb_s, kb_s], axis=0), jnp.concatenate([x_u, v_s], axis=0))
    return pt, qt, rh, yh


def _feat_kernel(pr_ref, hp_ref, hn_ref, mu_ref, w0_ref, a0_ref, w2_ref, a2_ref, g2_ref, kkp_ref, ka_ref, rk_ref,
                 ones_ref, pt_ref, qt_ref, rh_ref, yh_ref, bonus_ref, gate_ref, *, rw_w):
    i = pl.program_id(1)
    last = pl.num_programs(1) - 1
    tt = pr_ref.shape[0]
    pr = pr_ref[...]
    ridx = lax.broadcasted_iota(jnp.int32, (tt, 1), 0)
    edge_prev = jnp.where(i == 0, 0.0, hp_ref[SUBLANES - 1:SUBLANES, :])
    edge_next = jnp.where(i == last, 0.0, hn_ref[0:1, :])
    prev = jnp.where(ridx == 0, edge_prev, pltpu.roll(pr, 1, 0))
    nxt = jnp.where(ridx == tt - 1, edge_next, pltpu.roll(pr, tt - 1, 0))
    x = pr + mu_ref[...] * (0.5 * (prev + nxt) - pr)

    r = x[:, 0:rw_w]
    k = x[:, rw_w:2 * rw_w]
    v = x[:, 2 * rw_w:3 * rw_w]
    o = 3 * rw_w
    wd = x[:, o:o + LANES]
    ad = x[:, o + LANES:o + 2 * LANES]
    gd = x[:, o + 2 * LANES:o + 3 * LANES]

    ones_bd = ones_ref[...]
    logw = -DECAY_SCALE * _sigmoid(_dot(jnp.tanh(wd), w2_ref[...]) + w0_ref[...])
    iclr = _sigmoid(_dot(ad, a2_ref[...]) + a0_ref[...])
    gate_ref[...] = _dot(_sigmoid(gd), g2_ref[...])

    kkv = k * kkp_ref[...]
    kk = kkv / jnp.maximum(jnp.sqrt(_head_sum(kkv * kkv, ones_bd)), 1e-12)
    ka = ka_ref[...]
    k_dir = [k * (1.0 + (iclr[:, d * rw_w:(d + 1) * rw_w] - 1.0) * ka) for d in range(2)]
    bonus_ref[...] = _head_sum(r * (k_dir[0] + k_dir[1]) * rk_ref[...], ones_bd) * v

    row = lax.broadcasted_iota(jnp.int32, (tt, tt), 0)
    col = lax.broadcasted_iota(jnp.int32, (tt, tt), 1)
    same = (row // CHUNK) == (col // CHUNK)
    tri = [jnp.where(same & (col <= row), 1.0, 0.0).astype(BF16),
           jnp.where(same & (col >= row), 1.0, 0.0).astype(BF16)]
    neg_kk = -kk
    for d in range(2):
        lw_d = logw[:, d * rw_w:(d + 1) * rw_w]
        parts = _split3(lw_d)
        cs_d = sum(jnp.dot(tri[d], p, preferred_element_type=F32) for p in parts)
        b_d = kk * iclr[:, d * rw_w:(d + 1) * rw_w]
        for c in range(tt // CHUNK):
            rows = slice(c * CHUNK, (c + 1) * CHUNK)
            end = (c + 1) * CHUNK - 1 if d == 0 else c * CHUNK
            for p in range(rw_w // LANES):
                ln = slice(p * LANES, (p + 1) * LANES)
                pt, qt, rh, yh = _chunk_summary(
                    r[rows, ln], v[rows, ln], neg_kk[rows, ln], b_d[rows, ln], k_dir[d][rows, ln],
                    cs_d[rows, ln], lw_d[rows, ln], cs_d[end:end + 1, ln], forward=(d == 0))
                pt_ref[c, d, p] = pt
                qt_ref[c, d, p] = qt
                rh_ref[d, rows, ln] = rh
                yh_ref[d, rows, ln] = yh


def _rwkv_features(pr, fp, tt):
    b, t, w_all = pr.shape
    rw_w = fp["kkp"].shape[1]
    npair = rw_w // LANES
    nt = t // tt
    cpt = tt // CHUNK
    hb = tt // SUBLANES
    kern = functools.partial(_feat_kernel, rw_w=rw_w)
    names = ("mu", "w0", "a0", "w2", "a2", "g2", "kkp", "ka", "rk", "ones")
    return pl.pallas_call(
        kern,
        grid=(b, nt),
        in_specs=[pl.BlockSpec((None, tt, w_all), lambda bi, i: (bi, i, 0)),
                  pl.BlockSpec((None, SUBLANES, w_all), lambda bi, i: (bi, jnp.maximum(i * hb - 1, 0), 0)),
                  pl.BlockSpec((None, SUBLANES, w_all), lambda bi, i: (bi, jnp.minimum((i + 1) * hb, nt * hb - 1), 0))]
                 + [_const_spec(fp[n].shape) for n in names],
        out_specs=[pl.BlockSpec((None, cpt, 2, npair, LANES, LANES), lambda bi, i: (bi, i, 0, 0, 0, 0)),
                   pl.BlockSpec((None, cpt, 2, npair, LANES, LANES), lambda bi, i: (bi, i, 0, 0, 0, 0)),
                   pl.BlockSpec((None, 2, tt, rw_w), lambda bi, i: (bi, 0, i, 0)),
                   pl.BlockSpec((None, 2, tt, rw_w), lambda bi, i: (bi, 0, i, 0)),
                   pl.BlockSpec((None, tt, rw_w), lambda bi, i: (bi, i, 0)),
                   pl.BlockSpec((None, tt, rw_w), lambda bi, i: (bi, i, 0))],
        out_shape=[jax.ShapeDtypeStruct((b, t // CHUNK, 2, npair, LANES, LANES), F32),
                   jax.ShapeDtypeStruct((b, t // CHUNK, 2, npair, LANES, LANES), F32),
                   jax.ShapeDtypeStruct((b, 2, t, rw_w), F32),
                   jax.ShapeDtypeStruct((b, 2, t, rw_w), F32),
                   jax.ShapeDtypeStruct((b, t, rw_w), F32),
                   jax.ShapeDtypeStruct((b, t, rw_w), F32)],
        compiler_params=_cparams("parallel", "parallel"),
    )(pr, pr, pr, *[fp[n] for n in names])


def _dot_f32(a, b):
    return jnp.dot(a, b, preferred_element_type=F32, precision=lax.Precision.HIGHEST)


def _scan_kernel(s0_ref, ptf_ref, qtf_ref, ptb_ref, qtb_ref, rhf_ref, yhf_ref, rhb_ref, yhb_ref,
                 yf_ref, yb_ref, sfin_ref, st_ref, *, cps, npair):
    i = pl.program_id(1)

    @pl.when(i == 0)
    def _():
        st_ref[...] = s0_ref[...]

    for d, (pt_ref, qt_ref, rh_ref, yh_ref, y_ref) in enumerate(
            ((ptf_ref, qtf_ref, rhf_ref, yhf_ref, yf_ref), (ptb_ref, qtb_ref, rhb_ref, yhb_ref, yb_ref))):
        order = range(cps) if d == 0 else range(cps - 1, -1, -1)
        for c in order:
            rows = slice(c * CHUNK, (c + 1) * CHUNK)
            for p in range(npair):
                ln = slice(p * LANES, (p + 1) * LANES)
                st = st_ref[d, p]
                y_ref[rows, ln] = _dot_f32(rh_ref[rows, ln], st) + yh_ref[rows, ln]
                st_ref[d, p] = _dot_f32(pt_ref[c, p], st) + qt_ref[c, p]

    @pl.when(i == pl.num_programs(1) - 1)
    def _():
        sfin_ref[...] = st_ref[...]


def _rwkv_scan(s0, pt, qt, rh, yh, cps):
    b, nc, _, npair, _, _ = pt.shape
    t, rw_w = rh.shape[2], rh.shape[3]
    ns = nc // cps
    ts = cps * CHUNK
    kern = functools.partial(_scan_kernel, cps=cps, npair=npair)
    mat_f = pl.BlockSpec((None, cps, None, npair, LANES, LANES), lambda bi, i: (bi, i, 0, 0, 0, 0))
    mat_b = pl.BlockSpec((None, cps, None, npair, LANES, LANES), lambda bi, i: (bi, ns - 1 - i, 1, 0, 0, 0))
    tok_f = pl.BlockSpec((None, None, ts, rw_w), lambda bi, i: (bi, 0, i, 0))
    tok_b = pl.BlockSpec((None, None, ts, rw_w), lambda bi, i: (bi, 1, ns - 1 - i, 0))
    state = pl.BlockSpec((None, 2, npair, LANES, LANES), lambda bi, i: (bi, 0, 0, 0, 0))
    return pl.pallas_call(
        kern,
        grid=(b, ns),
        in_specs=[state, mat_f, mat_f, mat_b, mat_b, tok_f, tok_f, tok_b, tok_b],
        out_specs=[pl.BlockSpec((None, ts, rw_w), lambda bi, i: (bi, i, 0)),
                   pl.BlockSpec((None, ts, rw_w), lambda bi, i: (bi, ns - 1 - i, 0)),
                   state],
        out_shape=[jax.ShapeDtypeStruct((b, t, rw_w), F32),
                   jax.ShapeDtypeStruct((b, t, rw_w), F32),
                   jax.ShapeDtypeStruct((b, 2, npair, LANES, LANES), F32)],
        scratch_shapes=[pltpu.VMEM((2, npair, LANES, LANES), F32)],
        compiler_params=_cparams("parallel", "arbitrary"),
    )(s0, pt, qt, pt, qt, rh, yh, rh, yh)


def _residual_mlp(xl, yl, m, ng, w1_ref, w2_ref):
    x2 = xl + m[2:3] * _rms(yl, ng[1:2])
    hm = _rms(x2, ng[2:3]) * (1.0 + m[4:5]) + m[3:4]
    hid = jnp.maximum(jnp.dot(hm.astype(BF16), w1_ref[...], preferred_element_type=F32), 0.0)
    out = jnp.dot((hid * hid).astype(BF16), w2_ref[...], preferred_element_type=F32)
    return x2 + m[5:6] * _rms(out, ng[3:4])


def _mix_out_kernel(x_ref, att_ref, yf_ref, yb_ref, bonus_ref, gate_ref, mod_ref, ng_ref, lnw_ref, lnb_ref, ones_ref,
                    wo_ref, w1_ref, w2_ref, o_ref, *, att_w):
    ones_bd = ones_ref[...]
    y = yf_ref[...] + yb_ref[...]
    mean = _head_sum(y, ones_bd) * (1.0 / HEAD_DIM)
    yc = y - mean
    var = _head_sum(yc * yc, ones_bd) * (1.0 / HEAD_DIM)
    yn = yc * lax.rsqrt(var + GN_EPS) * lnw_ref[...] + lnb_ref[...]
    rw = (yn + bonus_ref[...]) * gate_ref[...]
    yl = (jnp.dot(att_ref[...], wo_ref[0:att_w, :], preferred_element_type=F32)
          + jnp.dot(rw.astype(BF16), wo_ref[att_w:, :], preferred_element_type=F32))
    o_ref[...] = _residual_mlp(x_ref[...], yl, mod_ref[...], ng_ref[...], w1_ref, w2_ref)


def _mix_out_mlp(x, att, yf, yb, bonus, gate, mod_l, ng, lnw, lnb, ones_bd, wo, w1, w2, tm):
    b, s, d = x.shape
    att_w = att.shape[2]
    rw_w = yf.shape[2]
    kern = functools.partial(_mix_out_kernel, att_w=att_w)
    tok = lambda w: pl.BlockSpec((None, tm, w), lambda bi, i: (bi, i, 0))
    return pl.pallas_call(
        kern,
        grid=(b, s // tm),
        in_specs=[tok(d), tok(att_w), tok(rw_w), tok(rw_w), tok(rw_w), tok(rw_w),
                  pl.BlockSpec((None, 6, d), lambda bi, i: (bi, 0, 0)),
                  _const_spec(ng.shape), _const_spec(lnw.shape), _const_spec(lnb.shape), _const_spec(ones_bd.shape),
                  _const_spec(wo.shape), _const_spec(w1.shape), _const_spec(w2.shape)],
        out_specs=tok(d),
        out_shape=jax.ShapeDtypeStruct((b, s, d), F32),
        compiler_params=_cparams("parallel", "parallel"),
    )(x, att, yf, yb, bonus, gate, mod_l, ng, lnw, lnb, ones_bd, wo, w1, w2)


def _conv_kernel(x_ref, xp_ref, xn_ref, mod_ref, ng_ref, wi_ref, cw_ref, wo_ref, w1_ref, w2_ref, o_ref):
    i = pl.program_id(1)
    last = pl.num_programs(1) - 1
    d = x_ref.shape[1]
    tm = x_ref.shape[0]
    m = mod_ref[...]
    ng = ng_ref[...]

    def modnorm(u):
        return (_rms(u, ng[0:1]) * (1.0 + m[1:2]) + m[0:1]).astype(BF16)

    x = x_ref[...]
    proj = jnp.dot(modnorm(x), wi_ref[...], preferred_element_type=F32)
    z = proj[:, d:2 * d] * proj[:, 2 * d:]
    pp = jnp.dot(modnorm(xp_ref[...]), wi_ref[:, d:], preferred_element_type=F32)
    pn = jnp.dot(modnorm(xn_ref[...]), wi_ref[:, d:], preferred_element_type=F32)
    zp = jnp.where(i == 0, 0.0, (pp[:, :d] * pp[:, d:])[SUBLANES - 1:SUBLANES, :])
    zn = jnp.where(i == last, 0.0, (pn[:, :d] * pn[:, d:])[0:1, :])
    ridx = lax.broadcasted_iota(jnp.int32, (tm, 1), 0)
    prev = jnp.where(ridx == 0, zp, pltpu.roll(z, 1, 0))
    nxt = jnp.where(ridx == tm - 1, zn, pltpu.roll(z, tm - 1, 0))
    cw = cw_ref[...]
    y = proj[:, :d] * (prev * cw[0:1] + z * cw[1:2] + nxt * cw[2:3])
    yl = jnp.dot(y.astype(BF16), wo_ref[...], preferred_element_type=F32)
    o_ref[...] = _residual_mlp(x, yl, m, ng, w1_ref, w2_ref)


def _conv_mlp(x, mod_l, ng, wi, cw, wo, w1, w2, tm):
    b, s, d = x.shape
    hb = tm // SUBLANES
    nt = s // tm
    return pl.pallas_call(
        _conv_kernel,
        grid=(b, nt),
        in_specs=[pl.BlockSpec((None, tm, d), lambda bi, i: (bi, i, 0)),
                  pl.BlockSpec((None, SUBLANES, d), lambda bi, i: (bi, jnp.maximum(i * hb - 1, 0), 0)),
                  pl.BlockSpec((None, SUBLANES, d), lambda bi, i: (bi, jnp.minimum((i + 1) * hb, nt * hb - 1), 0)),
                  pl.BlockSpec((None, 6, d), lambda bi, i: (bi, 0, 0)),
                  _const_spec(ng.shape), _const_spec(wi.shape), _const_spec(cw.shape), _const_spec(wo.shape),
                  _const_spec(w1.shape), _const_spec(w2.shape)],
        out_specs=pl.BlockSpec((None, tm, d), lambda bi, i: (bi, i, 0)),
        out_shape=jax.ShapeDtypeStruct((b, s, d), F32),
        compiler_params=_cparams("parallel", "parallel"),
    )(x, x, x, mod_l, ng, wi, cw, wo, w1, w2)


def _block_diag2(m):
    z = jnp.zeros_like(m[0])
    return jnp.concatenate([jnp.concatenate([m[0], z], axis=1), jnp.concatenate([z, m[1]], axis=1)], axis=0)


def _pick_tile(n, want):
    t = min(n, want)
    while n % t:
        t //= 2
    return t


def kernel(x, c, ctx, c_ctx, mod_w, mod_b, norm_g, mlp_w1, mlp_w2, ab_w_in, ab_w_out, att_sink, rwkv_mu, rwkv_w0,
           rwkv_w2, rwkv_a0, rwkv_a2, rwkv_g2, rwkv_kk, rwkv_ka, rwkv_rk, rwkv_ln_w, rwkv_ln_b, conv_w_in, conv_w,
           conv_w_out):
    b, s, d = x.shape
    l = ctx.shape[1]
    depth = mod_w.shape[0]
    assert depth == 2, "layer schedule below is written for one attention/RWKV layer followed by one conv layer"
    att_w = att_sink.shape[1] * HEAD_DIM
    kv_w = att_w // ATT_GROUP
    rw_w = rwkv_kk.shape[1]
    rw_in = rwkv_mu.shape[1]
    assert s % ATT_BLOCK == 0 and s % CHUNK == 0 and l % CHUNK == 0 and kv_w % LANES == 0 and rw_w % LANES == 0

    rows = -(-(b + 1) // SUBLANES) * SUBLANES
    cv = jnp.concatenate([c, c_ctx[None, :], jnp.zeros((rows - b - 1, d), F32)], axis=0)
    mod = _modulation(cv, mod_w, mod_b)
    mod_l = [mod[i, :b].reshape(b, 6, d) for i in range(depth)]
    mod_c0 = mod[0, b].reshape(6, d)

    w_in = ab_w_in[0]
    wq, wk, wv, wr = (w_in[:, :att_w], w_in[:, att_w:att_w + kv_w], w_in[:, att_w + kv_w:att_w + 2 * kv_w],
                      w_in[:, att_w + 2 * kv_w:])
    w_lat = jnp.concatenate([wq, _rot_cols(wq), wk, _rot_cols(wk), wv, wr], axis=1).astype(BF16)
    w_ctx = jnp.concatenate([wk, wv, wr], axis=1).astype(BF16)
    cos, sin = _rope_tables(s)
    g0 = norm_g[0, 0].reshape(1, d)
    q, k, v, pr = _project_latent(x, mod_l[0], g0, w_lat, cos, sin, att_w, kv_w, rw_in, _pick_tile(s, 512))
    kc, vc, prc = _project_ctx(ctx, mod_c0, g0, w_ctx, kv_w, rw_in)
    att = _attention(q, k, v, kc, vc, att_sink[0])

    head_id = jnp.arange(rw_w) // HEAD_DIM
    ones_bd = (head_id[:, None] == head_id[None, :]).astype(BF16)
    fp = dict(mu=rwkv_mu[0].reshape(1, rw_in),
              w0=rwkv_w0[0].reshape(1, 2 * rw_w), a0=rwkv_a0[0].reshape(1, 2 * rw_w),
              w2=_block_diag2(rwkv_w2[0]).astype(BF16), a2=_block_diag2(rwkv_a2[0]).astype(BF16),
              g2=rwkv_g2[0].astype(BF16),
              kkp=rwkv_kk[0].reshape(1, rw_w), ka=rwkv_ka[0].reshape(1, rw_w), rk=rwkv_rk[0].reshape(1, rw_w),
              ones=ones_bd)
    npair = rw_w // LANES
    ptc, qtc, rhc, yhc, _, _ = _rwkv_features(prc, fp, _pick_tile(l, 256))
    zero_state = jnp.zeros((b, 2, npair, LANES, LANES), F32)
    _, _, s_ctx = _rwkv_scan(zero_state, ptc, qtc, rhc, yhc, _pick_tile(l // CHUNK, 4))
    pt, qt, rh, yh, bonus, gate = _rwkv_features(pr, fp, _pick_tile(s, 256))
    yf, yb, _ = _rwkv_scan(s_ctx, pt, qt, rh, yh, _pick_tile(s // CHUNK, 4))

    xl = _mix_out_mlp(x, att, yf, yb, bonus, gate, mod_l[0], norm_g[0],
                      rwkv_ln_w[0].reshape(1, rw_w), rwkv_ln_b[0].reshape(1, rw_w), ones_bd,
                      ab_w_out[0].astype(BF16), mlp_w1[0].astype(BF16), mlp_w2[0].astype(BF16), _pick_tile(s, 256))

    return _conv_mlp(xl, mod_l[1], norm_g[1], conv_w_in[0].astype(BF16), conv_w[0], conv_w_out[0].astype(BF16),
                     mlp_w1[1].astype(BF16), mlp_w2[1].astype(BF16), _pick_tile(s, 256))
```

```python
import functools
import math

import jax
import jax.numpy as jnp
from jax import lax
from jax.experimental import pallas as pl
from jax.experimental.pallas import tpu as pltpu

F32 = jnp.float32
BF16 = jnp.bfloat16

HEAD_DIM = 64
GRID_W = 64
WINDOW = 128
ATT_BLOCK = 128
ATT_GROUP = 4
ROPE_BASE = 10000.0
NORM_EPS = 1e-6
GN_EPS = 64e-5
LANES = 128
SUBLANES = 8
CHUNK = 64
NEUMANN_STEPS = 6
DECAY_SCALE = math.exp(-0.5)
VMEM_LIMIT = 56 * 1024 * 1024


def _cparams(*sem):
    return pltpu.CompilerParams(dimension_semantics=sem, vmem_limit_bytes=VMEM_LIMIT)


def _const_spec(shape):
    nd = len(shape)
    return pl.BlockSpec(shape, lambda *_: (0,) * nd, pipeline_mode=pl.Buffered(1))


def _dot(a, b):
    return jnp.dot(a.astype(BF16), b.astype(BF16), preferred_element_type=F32)


def _dot_nt(a, b):
    return lax.dot_general(a.astype(BF16), b.astype(BF16), (((1,), (1,)), ((), ())), preferred_element_type=F32)


def _dot_tn(a, b):
    return lax.dot_general(a.astype(BF16), b.astype(BF16), (((0,), (0,)), ((), ())), preferred_element_type=F32)


def _split2(x):
    hi = x.astype(BF16)
    lo = (x - hi.astype(F32)).astype(BF16)
    return hi, lo


def _split3(x):
    hi = x.astype(BF16)
    r1 = x - hi.astype(F32)
    mid = r1.astype(BF16)
    lo = (r1 - mid.astype(F32)).astype(BF16)
    return hi, mid, lo


def _head_sum(x, ones_bd):
    hi, lo = _split2(x)
    return (jnp.dot(hi, ones_bd, preferred_element_type=F32)
            + jnp.dot(lo, ones_bd, preferred_element_type=F32))


def _rms(u, g):
    return u * lax.rsqrt(jnp.mean(u * u, axis=-1, keepdims=True) + NORM_EPS) * g


def _sigmoid(z):
    return 1.0 / (1.0 + jnp.exp(-z))


def _mod_kernel(cv_ref, w_ref, b_ref, o_ref):
    cv = cv_ref[...]
    s = cv * _sigmoid(cv)
    o_ref[...] = _dot(s, w_ref[...]) + b_ref[...]


def _modulation(cv, mod_w, mod_b):
    depth, d, six_d = mod_w.shape
    rows = cv.shape[0]
    nj = six_d // d
    return pl.pallas_call(
        _mod_kernel,
        grid=(depth, nj),
        in_specs=[pl.BlockSpec((rows, d), lambda l, j: (0, 0)),
                  pl.BlockSpec((None, d, d), lambda l, j: (l, 0, j)),
                  pl.BlockSpec((None, 1, d), lambda l, j: (l, 0, j))],
        out_specs=pl.BlockSpec((None, rows, d), lambda l, j: (l, 0, j)),
        out_shape=jax.ShapeDtypeStruct((depth, rows, six_d), F32),
        compiler_params=_cparams("arbitrary", "arbitrary"),
    )(cv, mod_w, mod_b.reshape(depth, 1, six_d))


def _proj_kernel(x_ref, mod_ref, g_ref, w_ref, cos_ref, sin_ref, q_ref, k_ref, v_ref, pr_ref, *, att_w, kv_w):
    m = mod_ref[...]
    h = (_rms(x_ref[...], g_ref[...]) * (1.0 + m[1:2]) + m[0:1]).astype(BF16)
    cos, sin = cos_ref[...], sin_ref[...]
    scale = HEAD_DIM ** -0.5
    o = 0
    for j in range(att_w // LANES):
        u = jnp.dot(h, w_ref[:, o + j * LANES:o + (j + 1) * LANES], preferred_element_type=F32)
        ur = jnp.dot(h, w_ref[:, o + att_w + j * LANES:o + att_w + (j + 1) * LANES], preferred_element_type=F32)
        q_ref[:, j * LANES:(j + 1) * LANES] = ((u * cos + ur * sin) * scale).astype(q_ref.dtype)
    o += 2 * att_w
    for j in range(kv_w // LANES):
        u = jnp.dot(h, w_ref[:, o + j * LANES:o + (j + 1) * LANES], preferred_element_type=F32)
        ur = jnp.dot(h, w_ref[:, o + kv_w + j * LANES:o + kv_w + (j + 1) * LANES], preferred_element_type=F32)
        k_ref[:, j * LANES:(j + 1) * LANES] = (u * cos + ur * sin).astype(k_ref.dtype)
    o += 2 * kv_w
    v_ref[...] = jnp.dot(h, w_ref[:, o:o + kv_w], preferred_element_type=F32).astype(v_ref.dtype)
    o += kv_w
    pr_ref[...] = jnp.dot(h, w_ref[:, o:], preferred_element_type=F32)


def _proj_ctx_kernel(x_ref, mod_ref, g_ref, w_ref, k_ref, v_ref, pr_ref, *, kv_w):
    m = mod_ref[...]
    h = (_rms(x_ref[...], g_ref[...]) * (1.0 + m[1:2]) + m[0:1]).astype(BF16)
    k_ref[...] = jnp.dot(h, w_ref[:, 0:kv_w], preferred_element_type=F32).astype(k_ref.dtype)
    v_ref[...] = jnp.dot(h, w_ref[:, kv_w:2 * kv_w], preferred_element_type=F32).astype(v_ref.dtype)
    pr_ref[...] = jnp.dot(h, w_ref[:, 2 * kv_w:], preferred_element_type=F32)


def _rot_cols(w):
    d, n = w.shape
    m = HEAD_DIM // 4
    w4 = w.reshape(d, n // (2 * m), 2, m)
    return jnp.stack([-w4[:, :, 1], w4[:, :, 0]], axis=2).reshape(d, n)


def _rope_tables(seq):
    m = HEAD_DIM // 4
    t = jnp.arange(seq)
    inv = ROPE_BASE ** (-jnp.arange(m, dtype=F32) / m)
    ang_r = (t // GRID_W).astype(F32)[:, None] * inv[None, :]
    ang_c = (t % GRID_W).astype(F32)[:, None] * inv[None, :]
    ang = jnp.concatenate([ang_r, ang_r, ang_c, ang_c], axis=-1)
    ang = jnp.tile(ang, (1, LANES // HEAD_DIM))
    return jnp.cos(ang), jnp.sin(ang)


def _project_latent(x, mod_l, g, w_all, cos, sin, att_w, kv_w, rw_w, tm):
    b, s, d = x.shape
    n_all = w_all.shape[1]
    kern = functools.partial(_proj_kernel, att_w=att_w, kv_w=kv_w)
    return pl.pallas_call(
        kern,
        grid=(b, s // tm),
        in_specs=[pl.BlockSpec((None, tm, d), lambda bi, i: (bi, i, 0)),
                  pl.BlockSpec((None, 6, d), lambda bi, i: (bi, 0, 0)),
                  _const_spec((1, d)),
                  _const_spec((d, n_all)),
                  pl.BlockSpec((tm, LANES), lambda bi, i: (i, 0)),
                  pl.BlockSpec((tm, LANES), lambda bi, i: (i, 0))],
        out_specs=[pl.BlockSpec((None, tm, att_w), lambda bi, i: (bi, i, 0)),
                   pl.BlockSpec((None, tm, kv_w), lambda bi, i: (bi, i, 0)),
                   pl.BlockSpec((None, tm, kv_w), lambda bi, i: (bi, i, 0)),
                   pl.BlockSpec((None, tm, rw_w), lambda bi, i: (bi, i, 0))],
        out_shape=[jax.ShapeDtypeStruct((b, s, att_w), BF16),
                   jax.ShapeDtypeStruct((b, s, kv_w), BF16),
                   jax.ShapeDtypeStruct((b, s, kv_w), BF16),
                   jax.ShapeDtypeStruct((b, s, rw_w), F32)],
        compiler_params=_cparams("parallel", "parallel"),
    )(x, mod_l, g, w_all, cos, sin)


def _project_ctx(ctx, mod_c, g, w_ctx, kv_w, rw_w):
    b, l, d = ctx.shape
    kern = functools.partial(_proj_ctx_kernel, kv_w=kv_w)
    return pl.pallas_call(
        kern,
        grid=(b,),
        in_specs=[pl.BlockSpec((None, l, d), lambda bi: (bi, 0, 0)),
                  _const_spec((6, d)),
                  _const_spec((1, d)),
                  _const_spec((d, w_ctx.shape[1]))],
        out_specs=[pl.BlockSpec((None, l, kv_w), lambda bi: (bi, 0, 0)),
                   pl.BlockSpec((None, l, kv_w), lambda bi: (bi, 0, 0)),
                   pl.BlockSpec((None, l, rw_w), lambda bi: (bi, 0, 0))],
        out_shape=[jax.ShapeDtypeStruct((b, l, kv_w), BF16),
                   jax.ShapeDtypeStruct((b, l, kv_w), BF16),
                   jax.ShapeDtypeStruct((b, l, rw_w), F32)],
        compiler_params=_cparams("parallel"),
    )(ctx, mod_c, g, w_ctx)


def _attn_kernel(sink_ref, q_ref, kp_ref, kc_ref, kn_ref, vp_ref, vc_ref, vn_ref, kx_ref, vx_ref, o_ref):
    i = pl.program_id(1)
    nb = pl.num_programs(1)
    blk = ATT_BLOCK
    kloc = jnp.concatenate([kp_ref[...], kc_ref[...], kn_ref[...]], axis=0)
    vloc = jnp.concatenate([vp_ref[...], vc_ref[...], vn_ref[...]], axis=0)
    kctx, vctx = kx_ref[...], vx_ref[...]
    row = lax.broadcasted_iota(jnp.int32, (2 * blk, 3 * blk), 0) % blk
    col = lax.broadcasted_iota(jnp.int32, (2 * blk, 3 * blk), 1)
    rel = row + blk - col
    valid = (jnp.abs(rel) <= WINDOW)
    valid = valid & jnp.logical_not((i == 0) & (col < blk))
    valid = valid & jnp.logical_not((i == nb - 1) & (col >= 2 * blk))
    even = lax.broadcasted_iota(jnp.int32, (blk, LANES), 1) < HEAD_DIM
    zero = jnp.zeros((), q_ref.dtype)
    tiles = range(q_ref.shape[1] // LANES)
    qs = []
    for j in tiles:
        qj = q_ref[:, j * LANES:(j + 1) * LANES]
        qs.append(jnp.concatenate([jnp.where(even, qj, zero), jnp.where(even, zero, qj)], axis=0))
    s_loc = [jnp.where(valid, _dot_nt(qj, kloc), -1e30) for qj in qs]
    s_ctx = [_dot_nt(qj, kctx) for qj in qs]
    sinks = [sink_ref[j] for j in tiles]
    mx = [jnp.maximum(jnp.maximum(jnp.max(a, axis=-1, keepdims=True), jnp.max(c, axis=-1, keepdims=True)), sk)
          for a, c, sk in zip(s_loc, s_ctx, sinks)]
    p_loc = [jnp.exp(a - m) for a, m in zip(s_loc, mx)]
    p_ctx = [jnp.exp(c - m) for c, m in zip(s_ctx, mx)]
    den = [jnp.sum(a, axis=-1, keepdims=True) + jnp.sum(c, axis=-1, keepdims=True) + jnp.exp(sk - m)
           for a, c, sk, m in zip(p_loc, p_ctx, sinks, mx)]
    outs = [(_dot(a, vloc) + _dot(c, vctx)) / dn for a, c, dn in zip(p_loc, p_ctx, den)]
    for j, o in zip(tiles, outs):
        o_ref[:, j * LANES:(j + 1) * LANES] = jnp.where(even, o[:blk], o[blk:]).astype(o_ref.dtype)


def _attention(q, k, v, kc, vc, sink_rows):
    b, s, att_w = q.shape
    kv_w = k.shape[2]
    l = kc.shape[1]
    nb = s // ATT_BLOCK
    blk = ATT_BLOCK
    kv_prev = pl.BlockSpec((None, blk, kv_w), lambda bi, i: (bi, jnp.maximum(i - 1, 0), 0))
    kv_cur = pl.BlockSpec((None, blk, kv_w), lambda bi, i: (bi, i, 0))
    kv_next = pl.BlockSpec((None, blk, kv_w), lambda bi, i: (bi, jnp.minimum(i + 1, nb - 1), 0))
    kv_ctx = pl.BlockSpec((None, l, kv_w), lambda bi, i: (bi, 0, 0))
    return pl.pallas_call(
        _attn_kernel,
        grid=(b, nb),
        in_specs=[_const_spec(sink_rows.shape),
                  pl.BlockSpec((None, blk, att_w), lambda bi, i: (bi, i, 0)),
                  kv_prev, kv_cur, kv_next, kv_prev, kv_cur, kv_next, kv_ctx, kv_ctx],
        out_specs=pl.BlockSpec((None, blk, att_w), lambda bi, i: (bi, i, 0)),
        out_shape=jax.ShapeDtypeStruct((b, s, att_w), BF16),
        compiler_params=_cparams("parallel", "parallel"),
    )(sink_rows, q, k, k, k, v, v, v, kc, vc)


def _stack_heads(x, even):
    return jnp.concatenate([jnp.where(even, x, 0.0), jnp.where(even, 0.0, x)], axis=0)


def _chunk_summaries(insts):
    c2 = 2 * CHUNK
    even = lax.broadcasted_iota(jnp.int32, (CHUNK, LANES), 1) < HEAD_DIM
    row = lax.broadcasted_iota(jnp.int32, (c2, c2), 0)
    col = lax.broadcasted_iota(jnp.int32, (c2, c2), 1)
    same = (row // CHUNK) == (col // CHUNK)
    rt, ct = row % CHUNK, col % CHUNK
    masks = {True: (same & (ct < rt), same & (ct <= rt)), False: (same & (ct > rt), same & (ct >= rt))}
    eye = lax.broadcasted_iota(jnp.int32, (LANES, LANES), 0) == lax.broadcasted_iota(jnp.int32, (LANES, LANES), 1)

    prep = []
    for rr, vv, aa, bb, kd, cs, lw, tot, forward in insts:
        g_inv = jnp.exp(-cs)
        g_rem = jnp.exp(tot - cs)
        a_s = _stack_heads(aa * jnp.exp(cs - lw), even)
        r_s = _stack_heads(rr * jnp.exp(cs), even)
        v_s = _stack_heads(vv, even).astype(BF16)
        b_t = (bb * g_inv).astype(BF16)
        k_t = (kd * g_inv).astype(BF16)
        bk_s = jnp.concatenate([_stack_heads(bb * g_rem, even), _stack_heads(kd * g_rem, even)], axis=0).astype(BF16)
        lhs = jnp.concatenate([a_s, r_s], axis=0).astype(BF16)
        rhs = jnp.concatenate([b_t, b_t, k_t, k_t], axis=0)
        prep.append((a_s, r_s, v_s, bk_s, lhs, rhs, jnp.exp(tot), masks[forward]))

    gs = [_dot_nt(p[4], p[5]) for p in prep]
    tri = []
    for g, p in zip(gs, prep):
        strict, incl = p[7]
        tri.append((jnp.where(strict, g[:c2, :c2], 0.0).astype(BF16), jnp.where(strict, g[:c2, c2:], 0.0).astype(BF16),
                    jnp.where(incl, g[c2:, :c2], 0.0).astype(BF16), jnp.where(incl, g[c2:, c2:], 0.0).astype(BF16)))
    xs = [jnp.concatenate([p[0], _dot(t[1], p[2])], axis=1) for p, t in zip(prep, tri)]
    lps = [t[0] for t in tri]
    for step in range(NEUMANN_STEPS):
        xs = [x + _dot(lp, x) for x, lp in zip(xs, lps)]
        if step + 1 < NEUMANN_STEPS:
            lps = [_dot(lp, lp).astype(BF16) for lp in lps]
    xbs = [x.astype(BF16) for x in xs]
    rys = [jnp.concatenate([p[1], _dot(t[3], p[2])], axis=1) + _dot(t[2], xb)
           for p, t, xb in zip(prep, tri, xbs)]
    out = []
    for p, xb, ry in zip(prep, xbs, rys):
        v_s, bk_s, g_tot = p[2], p[3], p[6]
        pt = jnp.where(eye, g_tot, 0.0) + _dot_tn(bk_s[:c2], xb[:, :LANES])
        qt = _dot_tn(bk_s, jnp.concatenate([xb[:, LANES:], v_s], axis=0))
        out.append((pt[:CHUNK] + pt[CHUNK:], qt[:CHUNK] + qt[CHUNK:],
                    ry[:CHUNK, :LANES] + ry[CHUNK:, :LANES], ry[:CHUNK, LANES:] + ry[CHUNK:, LANES:]))
    return out


def _feat_kernel(pr_ref, hp_ref, hn_ref, mu_ref, w0_ref, a0_ref, w2_ref, a2_ref, g2_ref, kkp_ref, ka_ref, rk_ref,
                 ones_ref, pt_ref, qt_ref, rh_ref, yh_ref, bonus_ref, gate_ref, *, rw_w):
    i = pl.program_id(1)
    last = pl.num_programs(1) - 1
    tt = pr_ref.shape[0]
    pr = pr_ref[...]
    ridx = lax.broadcasted_iota(jnp.int32, (tt, 1), 0)
    edge_prev = jnp.where(i == 0, 0.0, hp_ref[SUBLANES - 1:SUBLANES, :])
    edge_next = jnp.where(i == last, 0.0, hn_ref[0:1, :])
    prev = jnp.where(ridx == 0, edge_prev, pltpu.roll(pr, 1, 0))
    nxt = jnp.where(ridx == tt - 1, edge_next, pltpu.roll(pr, tt - 1, 0))
    x = pr + mu_ref[...] * (0.5 * (prev + nxt) - pr)

    r = x[:, 0:rw_w]
    k = x[:, rw_w:2 * rw_w]
    v = x[:, 2 * rw_w:3 * rw_w]
    o = 3 * rw_w
    wd = x[:, o:o + LANES]
    ad = x[:, o + LANES:o + 2 * LANES]
    gd = x[:, o + 2 * LANES:o + 3 * LANES]

    ones_bd = ones_ref[...]
    logw = -DECAY_SCALE * _sigmoid(_dot(jnp.tanh(wd), w2_ref[...]) + w0_ref[...])
    iclr = _sigmoid(_dot(ad, a2_ref[...]) + a0_ref[...])
    gate_ref[...] = _dot(_sigmoid(gd), g2_ref[...])

    kkv = k * kkp_ref[...]
    kk = kkv / jnp.maximum(jnp.sqrt(_head_sum(kkv * kkv, ones_bd)), 1e-12)
    ka = ka_ref[...]
    k_dir = [k * (1.0 + (iclr[:, d * rw_w:(d + 1) * rw_w] - 1.0) * ka) for d in range(2)]
    bonus_ref[...] = _head_sum(r * (k_dir[0] + k_dir[1]) * rk_ref[...], ones_bd) * v

    row = lax.broadcasted_iota(jnp.int32, (tt, tt), 0)
    col = lax.broadcasted_iota(jnp.int32, (tt, tt), 1)
    same = (row // CHUNK) == (col // CHUNK)
    tri = [jnp.where(same & (col <= row), 1.0, 0.0).astype(BF16),
           jnp.where(same & (col >= row), 1.0, 0.0).astype(BF16)]
    neg_kk = -kk
    lw_dir, cs_dir, b_dir = [], [], []
    for d in range(2):
        lw_d = logw[:, d * rw_w:(d + 1) * rw_w]
        lw_dir.append(lw_d)
        cs_dir.append(sum(jnp.dot(tri[d], p, preferred_element_type=F32) for p in _split3(lw_d)))
        b_dir.append(kk * iclr[:, d * rw_w:(d + 1) * rw_w])
    for c in range(tt // CHUNK):
        rows = slice(c * CHUNK, (c + 1) * CHUNK)
        keys, insts = [], []
        for d in range(2):
            end = (c + 1) * CHUNK - 1 if d == 0 else c * CHUNK
            for p in range(rw_w // LANES):
                ln = slice(p * LANES, (p + 1) * LANES)
                keys.append((d, p, ln))
                insts.append((r[rows, ln], v[rows, ln], neg_kk[rows, ln], b_dir[d][rows, ln], k_dir[d][rows, ln],
                              cs_dir[d][rows, ln], lw_dir[d][rows, ln], cs_dir[d][end:end + 1, ln], d == 0))
        for (d, p, ln), (pt, qt, rh, yh) in zip(keys, _chunk_summaries(insts)):
            pt_ref[c, d, p] = pt
            qt_ref[c, d, p] = qt
            rh_ref[d, rows, ln] = rh.astype(rh_ref.dtype)
            yh_ref[d, rows, ln] = yh


def _rwkv_features(pr, fp, tt):
    b, t, w_all = pr.shape
    rw_w = fp["kkp"].shape[1]
    npair = rw_w // LANES
    nt = t // tt
    cpt = tt // CHUNK
    hb = tt // SUBLANES
    kern = functools.partial(_feat_kernel, rw_w=rw_w)
    names = ("mu", "w0", "a0", "w2", "a2", "g2", "kkp", "ka", "rk", "ones")
    return pl.pallas_call(
        kern,
        grid=(b, nt),
        in_specs=[pl.BlockSpec((None, tt, w_all), lambda bi, i: (bi, i, 0)),
                  pl.BlockSpec((None, SUBLANES, w_all), lambda bi, i: (bi, jnp.maximum(i * hb - 1, 0), 0)),
                  pl.BlockSpec((None, SUBLANES, w_all), lambda bi, i: (bi, jnp.minimum((i + 1) * hb, nt * hb - 1), 0))]
                 + [_const_spec(fp[n].shape) for n in names],
        out_specs=[pl.BlockSpec((None, cpt, 2, npair, CHUNK, LANES), lambda bi, i: (bi, i, 0, 0, 0, 0)),
                   pl.BlockSpec((None, cpt, 2, npair, CHUNK, LANES), lambda bi, i: (bi, i, 0, 0, 0, 0)),
                   pl.BlockSpec((None, 2, tt, rw_w), lambda bi, i: (bi, 0, i, 0)),
                   pl.BlockSpec((None, 2, tt, rw_w), lambda bi, i: (bi, 0, i, 0)),
                   pl.BlockSpec((None, tt, rw_w), lambda bi, i: (bi, i, 0)),
                   pl.BlockSpec((None, tt, rw_w), lambda bi, i: (bi, i, 0))],
        out_shape=[jax.ShapeDtypeStruct((b, t // CHUNK, 2, npair, CHUNK, LANES), F32),
                   jax.ShapeDtypeStruct((b, t // CHUNK, 2, npair, CHUNK, LANES), F32),
                   jax.ShapeDtypeStruct((b, 2, t, rw_w), BF16),
                   jax.ShapeDtypeStruct((b, 2, t, rw_w), F32),
                   jax.ShapeDtypeStruct((b, t, rw_w), F32),
                   jax.ShapeDtypeStruct((b, t, rw_w), F32)],
        compiler_params=_cparams("parallel", "parallel"),
    )(pr, pr, pr, *[fp[n] for n in names])


def _scan_kernel(s0_ref, ptf_ref, qtf_ref, ptb_ref, qtb_ref, rhf_ref, yhf_ref, rhb_ref, yhb_ref,
                 yf_ref, yb_ref, sfin_ref, st_ref, *, cps, npair):
    i = pl.program_id(1)

    @pl.when(i == 0)
    def _():
        st_ref[...] = s0_ref[...]

    even = lax.broadcasted_iota(jnp.int32, (CHUNK, LANES), 1) < HEAD_DIM
    dirs = ((ptf_ref, qtf_ref, rhf_ref, yhf_ref, yf_ref), (ptb_ref, qtb_ref, rhb_ref, yhb_ref, yb_ref))
    keys = [(d, p) for d in range(2) for p in range(npair)]
    st = [st_ref[d, p] for d, p in keys]
    for step in range(cps):
        hl = []
        for s in st:
            hi = s.astype(BF16)
            hl.append(jnp.concatenate([hi, (s - hi.astype(F32)).astype(BF16)], axis=1))
        new = []
        for (d, p), s2 in zip(keys, hl):
            pt_ref, qt_ref, rh_ref, yh_ref, y_ref = dirs[d]
            c = step if d == 0 else cps - 1 - step
            rows = slice(c * CHUNK, (c + 1) * CHUNK)
            ln = slice(p * LANES, (p + 1) * LANES)
            y2 = jnp.dot(rh_ref[rows, ln], s2, preferred_element_type=F32)
            y_ref[rows, ln] = y2[:, :LANES] + y2[:, LANES:] + yh_ref[rows, ln]
            pt_hi, pt_lo = _split2(_stack_heads(pt_ref[c, p], even))
            n2 = jnp.dot(pt_hi, s2, preferred_element_type=F32)
            new.append(n2[:, :LANES] + n2[:, LANES:] + jnp.dot(pt_lo, s2[:, :LANES], preferred_element_type=F32)
                       + _stack_heads(qt_ref[c, p], even))
        st = new
    for (d, p), s in zip(keys, st):
        st_ref[d, p] = s

    @pl.when(i == pl.num_programs(1) - 1)
    def _():
        sfin_ref[...] = st_ref[...]


def _rwkv_scan(s0, pt, qt, rh, yh, cps):
    b, nc, _, npair, _, _ = pt.shape
    t, rw_w = rh.shape[2], rh.shape[3]
    ns = nc // cps
    ts = cps * CHUNK
    kern = functools.partial(_scan_kernel, cps=cps, npair=npair)
    mat_f = pl.BlockSpec((None, cps, None, npair, CHUNK, LANES), lambda bi, i: (bi, i, 0, 0, 0, 0))
    mat_b = pl.BlockSpec((None, cps, None, npair, CHUNK, LANES), lambda bi, i: (bi, ns - 1 - i, 1, 0, 0, 0))
    tok_f = pl.BlockSpec((None, None, ts, rw_w), lambda bi, i: (bi, 0, i, 0))
    tok_b = pl.BlockSpec((None, None, ts, rw_w), lambda bi, i: (bi, 1, ns - 1 - i, 0))
    state = pl.BlockSpec((None, 2, npair, LANES, LANES), lambda bi, i: (bi, 0, 0, 0, 0))
    return pl.pallas_call(
        kern,
        grid=(b, ns),
        in_specs=[state, mat_f, mat_f, mat_b, mat_b, tok_f, tok_f, tok_b, tok_b],
        out_specs=[pl.BlockSpec((None, ts, rw_w), lambda bi, i: (bi, i, 0)),
                   pl.BlockSpec((None, ts, rw_w), lambda bi, i: (bi, ns - 1 - i, 0)),
                   state],
        out_shape=[jax.ShapeDtypeStruct((b, t, rw_w), F32),
                   jax.ShapeDtypeStruct((b, t, rw_w), F32),
                   jax.ShapeDtypeStruct((b, 2, npair, LANES, LANES), F32)],
        scratch_shapes=[pltpu.VMEM((2, npair, LANES, LANES), F32)],
        compiler_params=_cparams("parallel", "arbitrary"),
    )(s0, pt, qt, pt, qt, rh, yh, rh, yh)


def _residual_mlp(xl, yl, m, ng, w1_ref, w2_ref):
    x2 = xl + m[2:3] * _rms(yl, ng[1:2])
    hm = _rms(x2, ng[2:3]) * (1.0 + m[4:5]) + m[3:4]
    hid = jnp.maximum(jnp.dot(hm.astype(BF16), w1_ref[...], preferred_element_type=F32), 0.0)
    out = jnp.dot((hid * hid).astype(BF16), w2_ref[...], preferred_element_type=F32)
    return x2 + m[5:6] * _rms(out, ng[3:4])


def _mix_out_kernel(x_ref, att_ref, yf_ref, yb_ref, bonus_ref, gate_ref, mod_ref, ng_ref, lnw_ref, lnb_ref, ones_ref,
                    wo_ref, w1_ref, w2_ref, o_ref, *, att_w):
    ones_bd = ones_ref[...]
    y = yf_ref[...] + yb_ref[...]
    mean = _head_sum(y, ones_bd) * (1.0 / HEAD_DIM)
    yc = y - mean
    var = _head_sum(yc * yc, ones_bd) * (1.0 / HEAD_DIM)
    yn = yc * lax.rsqrt(var + GN_EPS) * lnw_ref[...] + lnb_ref[...]
    rw = (yn + bonus_ref[...]) * gate_ref[...]
    yl = (jnp.dot(att_ref[...], wo_ref[0:att_w, :], preferred_element_type=F32)
          + jnp.dot(rw.astype(BF16), wo_ref[att_w:, :], preferred_element_type=F32))
    o_ref[...] = _residual_mlp(x_ref[...], yl, mod_ref[...], ng_ref[...], w1_ref, w2_ref)


def _mix_out_mlp(x, att, yf, yb, bonus, gate, mod_l, ng, lnw, lnb, ones_bd, wo, w1, w2, tm):
    b, s, d = x.shape
    att_w = att.shape[2]
    rw_w = yf.shape[2]
    kern = functools.partial(_mix_out_kernel, att_w=att_w)
    tok = lambda w: pl.BlockSpec((None, tm, w), lambda bi, i: (bi, i, 0))
    return pl.pallas_call(
        kern,
        grid=(b, s // tm),
        in_specs=[tok(d), tok(att_w), tok(rw_w), tok(rw_w), tok(rw_w), tok(rw_w),
                  pl.BlockSpec((None, 6, d), lambda bi, i: (bi, 0, 0)),
                  _const_spec(ng.shape), _const_spec(lnw.shape), _const_spec(lnb.shape), _const_spec(ones_bd.shape),
                  _const_spec(wo.shape), _const_spec(w1.shape), _const_spec(w2.shape)],
        out_specs=tok(d),
        out_shape=jax.ShapeDtypeStruct((b, s, d), F32),
        compiler_params=_cparams("parallel", "parallel"),
    )(x, att, yf, yb, bonus, gate, mod_l, ng, lnw, lnb, ones_bd, wo, w1, w2)


def _conv_kernel(x_ref, xp_ref, xn_ref, mod_ref, ng_ref, wi_ref, cw_ref, wo_ref, w1_ref, w2_ref, o_ref):
    i = pl.program_id(1)
    last = pl.num_programs(1) - 1
    d = x_ref.shape[1]
    tm = x_ref.shape[0]
    m = mod_ref[...]
    ng = ng_ref[...]

    def modnorm(u):
        return (_rms(u, ng[0:1]) * (1.0 + m[1:2]) + m[0:1]).astype(BF16)

    x = x_ref[...]
    proj = jnp.dot(modnorm(x), wi_ref[...], preferred_element_type=F32)
    z = proj[:, d:2 * d] * proj[:, 2 * d:]
    pp = jnp.dot(modnorm(xp_ref[...]), wi_ref[:, d:], preferred_element_type=F32)
    pn = jnp.dot(modnorm(xn_ref[...]), wi_ref[:, d:], preferred_element_type=F32)
    zp = jnp.where(i == 0, 0.0, (pp[:, :d] * pp[:, d:])[SUBLANES - 1:SUBLANES, :])
    zn = jnp.where(i == last, 0.0, (pn[:, :d] * pn[:, d:])[0:1, :])
    ridx = lax.broadcasted_iota(jnp.int32, (tm, 1), 0)
    prev = jnp.where(ridx == 0, zp, pltpu.roll(z, 1, 0))
    nxt = jnp.where(ridx == tm - 1, zn, pltpu.roll(z, tm - 1, 0))
    cw = cw_ref[...]
    y = proj[:, :d] * (prev * cw[0:1] + z * cw[1:2] + nxt * cw[2:3])
    yl = jnp.dot(y.astype(BF16), wo_ref[...], preferred_element_type=F32)
    o_ref[...] = _residual_mlp(x, yl, m, ng, w1_ref, w2_ref)


def _conv_mlp(x, mod_l, ng, wi, cw, wo, w1, w2, tm):
    b, s, d = x.shape
    hb = tm // SUBLANES
    nt = s // tm
    return pl.pallas_call(
        _conv_kernel,
        grid=(b, nt),
        in_specs=[pl.BlockSpec((None, tm, d), lambda bi, i: (bi, i, 0)),
                  pl.BlockSpec((None, SUBLANES, d), lambda bi, i: (bi, jnp.maximum(i * hb - 1, 0), 0)),
                  pl.BlockSpec((None, SUBLANES, d), lambda bi, i: (bi, jnp.minimum((i + 1) * hb, nt * hb - 1), 0)),
                  pl.BlockSpec((None, 6, d), lambda bi, i: (bi, 0, 0)),
                  _const_spec(ng.shape), _const_spec(wi.shape), _const_spec(cw.shape), _const_spec(wo.shape),
                  _const_spec(w1.shape), _const_spec(w2.shape)],
        out_specs=pl.BlockSpec((None, tm, d), lambda bi, i: (bi, i, 0)),
        out_shape=jax.ShapeDtypeStruct((b, s, d), F32),
        compiler_params=_cparams("parallel", "parallel"),
    )(x, x, x, mod_l, ng, wi, cw, wo, w1, w2)


def _block_diag2(m):
    z = jnp.zeros_like(m[0])
    return jnp.concatenate([jnp.concatenate([m[0], z], axis=1), jnp.concatenate([z, m[1]], axis=1)], axis=0)


def _pick_tile(n, want):
    t = min(n, want)
    while n % t:
        t //= 2
    return t


def kernel(x, c, ctx, c_ctx, mod_w, mod_b, norm_g, mlp_w1, mlp_w2, ab_w_in, ab_w_out, att_sink, rwkv_mu, rwkv_w0,
           rwkv_w2, rwkv_a0, rwkv_a2, rwkv_g2, rwkv_kk, rwkv_ka, rwkv_rk, rwkv_ln_w, rwkv_ln_b, conv_w_in, conv_w,
           conv_w_out):
    b, s, d = x.shape
    l = ctx.shape[1]
    depth = mod_w.shape[0]
    assert depth == 2, "layer schedule below is written for one attention/RWKV layer followed by one conv layer"
    att_w = att_sink.shape[1] * HEAD_DIM
    kv_w = att_w // ATT_GROUP
    rw_w = rwkv_kk.shape[1]
    rw_in = rwkv_mu.shape[1]
    assert s % ATT_BLOCK == 0 and s % CHUNK == 0 and l % CHUNK == 0 and kv_w % LANES == 0 and rw_w % LANES == 0

    rows = -(-(b + 1) // SUBLANES) * SUBLANES
    cv = jnp.concatenate([c, c_ctx[None, :], jnp.zeros((rows - b - 1, d), F32)], axis=0)
    mod = _modulation(cv, mod_w, mod_b)
    mod_l = [mod[i, :b].reshape(b, 6, d) for i in range(depth)]
    mod_c0 = mod[0, b].reshape(6, d)

    w_in = ab_w_in[0]
    wq, wk, wv, wr = (w_in[:, :att_w], w_in[:, att_w:att_w + kv_w], w_in[:, att_w + kv_w:att_w + 2 * kv_w],
                      w_in[:, att_w + 2 * kv_w:])
    n_heads = att_w // HEAD_DIM
    perm = jnp.arange(n_heads).reshape(n_heads // ATT_GROUP, ATT_GROUP).T.reshape(-1)
    wq = wq.reshape(d, n_heads, HEAD_DIM)[:, perm].reshape(d, att_w)
    w_lat = jnp.concatenate([wq, _rot_cols(wq), wk, _rot_cols(wk), wv, wr], axis=1).astype(BF16)
    w_ctx = jnp.concatenate([wk, wv, wr], axis=1).astype(BF16)
    cos, sin = _rope_tables(s)
    g0 = norm_g[0, 0].reshape(1, d)
    q, k, v, pr = _project_latent(x, mod_l[0], g0, w_lat, cos, sin, att_w, kv_w, rw_in, _pick_tile(s, 512))
    kc, vc, prc = _project_ctx(ctx, mod_c0, g0, w_ctx, kv_w, rw_in)
    sink_rows = jnp.repeat(att_sink[0][perm].reshape(att_w // LANES, LANES // HEAD_DIM), ATT_BLOCK, axis=1)[..., None]
    att = _attention(q, k, v, kc, vc, sink_rows)
    w_out = ab_w_out[0]
    w_out = jnp.concatenate([w_out[:att_w].reshape(n_heads, HEAD_DIM, d)[perm].reshape(att_w, d), w_out[att_w:]], axis=0)

    head_id = jnp.arange(rw_w) // HEAD_DIM
    ones_bd = (head_id[:, None] == head_id[None, :]).astype(BF16)
    fp = dict(mu=rwkv_mu[0].reshape(1, rw_in),
              w0=rwkv_w0[0].reshape(1, 2 * rw_w), a0=rwkv_a0[0].reshape(1, 2 * rw_w),
              w2=_block_diag2(rwkv_w2[0]).astype(BF16), a2=_block_diag2(rwkv_a2[0]).astype(BF16),
              g2=rwkv_g2[0].astype(BF16),
              kkp=rwkv_kk[0].reshape(1, rw_w), ka=rwkv_ka[0].reshape(1, rw_w), rk=rwkv_rk[0].reshape(1, rw_w),
              ones=ones_bd)
    npair = rw_w // LANES
    ptc, qtc, rhc, yhc, _, _ = _rwkv_features(prc, fp, _pick_tile(l, 256))
    zero_state = jnp.zeros((b, 2, npair, LANES, LANES), F32)
    _, _, s_ctx = _rwkv_scan(zero_state, ptc, qtc, rhc, yhc, _pick_tile(l // CHUNK, 4))
    pt, qt, rh, yh, bonus, gate = _rwkv_features(pr, fp, _pick_tile(s, 256))
    yf, yb, _ = _rwkv_scan(s_ctx, pt, qt, rh, yh, _pick_tile(s // CHUNK, 4))

    xl = _mix_out_mlp(x, att, yf, yb, bonus, gate, mod_l[0], norm_g[0],
                      rwkv_ln_w[0].reshape(1, rw_w), rwkv_ln_b[0].reshape(1, rw_w), ones_bd,
                      w_out.astype(BF16), mlp_w1[0].astype(BF16), mlp_w2[0].astype(BF16), _pick_tile(s, 256))

    return _conv_mlp(xl, mod_l[1], norm_g[1], conv_w_in[0].astype(BF16), conv_w[0], conv_w_out[0].astype(BF16),
                     mlp_w1[1].astype(BF16), mlp_w2[1].astype(BF16), _pick_tile(s, 256))
```

```python
import functools
import math

import jax
import jax.numpy as jnp
from jax import lax
from jax.experimental import pallas as pl
from jax.experimental.pallas import tpu as pltpu

F32 = jnp.float32
BF16 = jnp.bfloat16

HEAD_DIM = 64
GRID_W = 64
WINDOW = 128
ATT_BLOCK = 128
ATT_GROUP = 4
ROPE_BASE = 10000.0
NORM_EPS = 1e-6
GN_EPS = 64e-5
LANES = 128
SUBLANES = 8
CHUNK = 64
NEUMANN_STEPS = 6
DECAY_SCALE = math.exp(-0.5)
PROJ_ROWS = 512
FEAT_ROWS = 256
SCAN_CHUNKS = 4
ATT_SUBBLOCKS = 2
MLP_ROWS = 512
MLP_HIDDEN_BLOCK = 2048
VMEM_LIMIT = 56 * 1024 * 1024


def _cparams(*sem):
    return pltpu.CompilerParams(dimension_semantics=sem, vmem_limit_bytes=VMEM_LIMIT)


def _const_spec(shape):
    nd = len(shape)
    return pl.BlockSpec(shape, lambda *_: (0,) * nd, pipeline_mode=pl.Buffered(1))


def _dot(a, b):
    return jnp.dot(a.astype(BF16), b.astype(BF16), preferred_element_type=F32)


def _dot_nt(a, b):
    return lax.dot_general(a.astype(BF16), b.astype(BF16), (((1,), (1,)), ((), ())), preferred_element_type=F32)


def _dot_tn(a, b):
    return lax.dot_general(a.astype(BF16), b.astype(BF16), (((0,), (0,)), ((), ())), preferred_element_type=F32)


def _split2(x):
    hi = x.astype(BF16)
    lo = (x - hi.astype(F32)).astype(BF16)
    return hi, lo


def _split3(x):
    hi = x.astype(BF16)
    r1 = x - hi.astype(F32)
    mid = r1.astype(BF16)
    lo = (r1 - mid.astype(F32)).astype(BF16)
    return hi, mid, lo


def _head_sum(x, ones_bd):
    hi, lo = _split2(x)
    return (jnp.dot(hi, ones_bd, preferred_element_type=F32)
            + jnp.dot(lo, ones_bd, preferred_element_type=F32))


def _rms(u, g):
    return u * lax.rsqrt(jnp.mean(u * u, axis=-1, keepdims=True) + NORM_EPS) * g


def _sigmoid(z):
    return 1.0 / (1.0 + jnp.exp(-z))


def _mod_kernel(cv_ref, w_ref, b_ref, o_ref):
    cv = cv_ref[...]
    s = cv * _sigmoid(cv)
    o_ref[...] = _dot(s, w_ref[...]) + b_ref[...]


def _modulation(cv, mod_w, mod_b):
    depth, d, six_d = mod_w.shape
    rows = cv.shape[0]
    nj = six_d // d
    return pl.pallas_call(
        _mod_kernel,
        grid=(depth, nj),
        in_specs=[pl.BlockSpec((rows, d), lambda l, j: (0, 0)),
                  pl.BlockSpec((None, d, d), lambda l, j: (l, 0, j)),
                  pl.BlockSpec((None, 1, d), lambda l, j: (l, 0, j))],
        out_specs=pl.BlockSpec((None, rows, d), lambda l, j: (l, 0, j)),
        out_shape=jax.ShapeDtypeStruct((depth, rows, six_d), F32),
        compiler_params=_cparams("arbitrary", "arbitrary"),
    )(cv, mod_w, mod_b.reshape(depth, 1, six_d))


def _proj_kernel(x_ref, mod_ref, g_ref, w_ref, cos_ref, sin_ref, q_ref, k_ref, v_ref, pr_ref, *, att_w, kv_w):
    m = mod_ref[...]
    h = (_rms(x_ref[...], g_ref[...]) * (1.0 + m[1:2]) + m[0:1]).astype(BF16)
    cos, sin = cos_ref[...], sin_ref[...]
    scale = HEAD_DIM ** -0.5
    p = jnp.dot(h, w_ref[...], preferred_element_type=F32)
    o = 0
    for j in range(att_w // LANES):
        u = p[:, o + j * LANES:o + (j + 1) * LANES]
        ur = p[:, o + att_w + j * LANES:o + att_w + (j + 1) * LANES]
        q_ref[:, j * LANES:(j + 1) * LANES] = ((u * cos + ur * sin) * scale).astype(q_ref.dtype)
    o += 2 * att_w
    for j in range(kv_w // LANES):
        u = p[:, o + j * LANES:o + (j + 1) * LANES]
        ur = p[:, o + kv_w + j * LANES:o + kv_w + (j + 1) * LANES]
        k_ref[:, j * LANES:(j + 1) * LANES] = (u * cos + ur * sin).astype(k_ref.dtype)
    o += 2 * kv_w
    v_ref[...] = p[:, o:o + kv_w].astype(v_ref.dtype)
    o += kv_w
    pr_ref[...] = p[:, o:]


def _proj_ctx_kernel(x_ref, mod_ref, g_ref, w_ref, k_ref, v_ref, pr_ref, *, kv_w):
    m = mod_ref[...]
    h = (_rms(x_ref[...], g_ref[...]) * (1.0 + m[1:2]) + m[0:1]).astype(BF16)
    k_ref[...] = jnp.dot(h, w_ref[:, 0:kv_w], preferred_element_type=F32).astype(k_ref.dtype)
    v_ref[...] = jnp.dot(h, w_ref[:, kv_w:2 * kv_w], preferred_element_type=F32).astype(v_ref.dtype)
    pr_ref[...] = jnp.dot(h, w_ref[:, 2 * kv_w:], preferred_element_type=F32)


def _rot_cols(w):
    d, n = w.shape
    m = HEAD_DIM // 4
    w4 = w.reshape(d, n // (2 * m), 2, m)
    return jnp.stack([-w4[:, :, 1], w4[:, :, 0]], axis=2).reshape(d, n)


def _rope_tables(seq):
    m = HEAD_DIM // 4
    t = jnp.arange(seq)
    inv = ROPE_BASE ** (-jnp.arange(m, dtype=F32) / m)
    ang_r = (t // GRID_W).astype(F32)[:, None] * inv[None, :]
    ang_c = (t % GRID_W).astype(F32)[:, None] * inv[None, :]
    ang = jnp.concatenate([ang_r, ang_r, ang_c, ang_c], axis=-1)
    ang = jnp.tile(ang, (1, LANES // HEAD_DIM))
    return jnp.cos(ang), jnp.sin(ang)


def _project_latent(x, mod_l, g, w_all, cos, sin, att_w, kv_w, rw_w, tm):
    b, s, d = x.shape
    n_all = w_all.shape[1]
    kern = functools.partial(_proj_kernel, att_w=att_w, kv_w=kv_w)
    return pl.pallas_call(
        kern,
        grid=(b, s // tm),
        in_specs=[pl.BlockSpec((None, tm, d), lambda bi, i: (bi, i, 0)),
                  pl.BlockSpec((None, 6, d), lambda bi, i: (bi, 0, 0)),
                  _const_spec((1, d)),
                  _const_spec((d, n_all)),
                  pl.BlockSpec((tm, LANES), lambda bi, i: (i, 0)),
                  pl.BlockSpec((tm, LANES), lambda bi, i: (i, 0))],
        out_specs=[pl.BlockSpec((None, tm, att_w), lambda bi, i: (bi, i, 0)),
                   pl.BlockSpec((None, tm, kv_w), lambda bi, i: (bi, i, 0)),
                   pl.BlockSpec((None, tm, kv_w), lambda bi, i: (bi, i, 0)),
                   pl.BlockSpec((None, tm, rw_w), lambda bi, i: (bi, i, 0))],
        out_shape=[jax.ShapeDtypeStruct((b, s, att_w), BF16),
                   jax.ShapeDtypeStruct((b, s, kv_w), BF16),
                   jax.ShapeDtypeStruct((b, s, kv_w), BF16),
                   jax.ShapeDtypeStruct((b, s, rw_w), F32)],
        compiler_params=_cparams("parallel", "parallel"),
    )(x, mod_l, g, w_all, cos, sin)


def _project_ctx(ctx, mod_c, g, w_ctx, kv_w, rw_w):
    b, l, d = ctx.shape
    kern = functools.partial(_proj_ctx_kernel, kv_w=kv_w)
    return pl.pallas_call(
        kern,
        grid=(b,),
        in_specs=[pl.BlockSpec((None, l, d), lambda bi: (bi, 0, 0)),
                  _const_spec((6, d)),
                  _const_spec((1, d)),
                  _const_spec((d, w_ctx.shape[1]))],
        out_specs=[pl.BlockSpec((None, l, kv_w), lambda bi: (bi, 0, 0)),
                   pl.BlockSpec((None, l, kv_w), lambda bi: (bi, 0, 0)),
                   pl.BlockSpec((None, l, rw_w), lambda bi: (bi, 0, 0))],
        out_shape=[jax.ShapeDtypeStruct((b, l, kv_w), BF16),
                   jax.ShapeDtypeStruct((b, l, kv_w), BF16),
                   jax.ShapeDtypeStruct((b, l, rw_w), F32)],
        compiler_params=_cparams("parallel"),
    )(ctx, mod_c, g, w_ctx)


def _attn_kernel(sink_ref, q_ref, kp_ref, kc_ref, kn_ref, vp_ref, vc_ref, vn_ref, kx_ref, vx_ref, o_ref, *, nsub):
    i = pl.program_id(1)
    nb = pl.num_programs(1)
    blk = ATT_BLOCK
    kall = jnp.concatenate([kp_ref[...], kc_ref[...], kn_ref[...]], axis=0)
    vall = jnp.concatenate([vp_ref[...], vc_ref[...], vn_ref[...]], axis=0)
    kctx, vctx = kx_ref[...], vx_ref[...]
    row = lax.broadcasted_iota(jnp.int32, (2 * blk, 3 * blk), 0) % blk
    col = lax.broadcasted_iota(jnp.int32, (2 * blk, 3 * blk), 1)
    band = jnp.abs(row + blk - col) <= WINDOW
    even = lax.broadcasted_iota(jnp.int32, (blk, LANES), 1) < HEAD_DIM
    zero = jnp.zeros((), q_ref.dtype)
    keys = [(a, j) for a in range(nsub) for j in range(q_ref.shape[1] // LANES)]
    qs, valid, kloc, vloc = [], [], [], []
    for a in range(nsub):
        va = band
        if a == 0:
            va = va & jnp.logical_not((i == 0) & (col < blk))
        if a == nsub - 1:
            va = va & jnp.logical_not((i == nb - 1) & (col >= 2 * blk))
        valid.append(va)
        kloc.append(kall[a * blk:(a + 3) * blk])
        vloc.append(vall[a * blk:(a + 3) * blk])
    for a, j in keys:
        qj = q_ref[a * blk:(a + 1) * blk, j * LANES:(j + 1) * LANES]
        qs.append(jnp.concatenate([jnp.where(even, qj, zero), jnp.where(even, zero, qj)], axis=0))
    s_loc = [jnp.where(valid[a], _dot_nt(qj, kloc[a]), -1e30) for (a, j), qj in zip(keys, qs)]
    s_ctx = [_dot_nt(qj, kctx) for qj in qs]
    sinks = [sink_ref[j] for a, j in keys]
    mx = [jnp.maximum(jnp.maximum(jnp.max(sl, axis=-1, keepdims=True), jnp.max(sc, axis=-1, keepdims=True)), sk)
          for sl, sc, sk in zip(s_loc, s_ctx, sinks)]
    p_loc = [jnp.exp(sl - m) for sl, m in zip(s_loc, mx)]
    p_ctx = [jnp.exp(sc - m) for sc, m in zip(s_ctx, mx)]
    den = [jnp.sum(pa, axis=-1, keepdims=True) + jnp.sum(pc, axis=-1, keepdims=True) + jnp.exp(sk - m)
           for pa, pc, sk, m in zip(p_loc, p_ctx, sinks, mx)]
    outs = [(_dot(pa, vloc[a]) + _dot(pc, vctx)) / dn for (a, j), pa, pc, dn in zip(keys, p_loc, p_ctx, den)]
    for (a, j), o in zip(keys, outs):
        o_ref[a * blk:(a + 1) * blk, j * LANES:(j + 1) * LANES] = jnp.where(even, o[:blk], o[blk:]).astype(o_ref.dtype)


def _attention(q, k, v, kc, vc, sink_rows):
    b, s, att_w = q.shape
    kv_w = k.shape[2]
    l = kc.shape[1]
    blk = ATT_BLOCK
    nsub = _pick_tile(s // blk, ATT_SUBBLOCKS)
    nb = s // (nsub * blk)
    kern = functools.partial(_attn_kernel, nsub=nsub)
    kv_prev = pl.BlockSpec((None, blk, kv_w), lambda bi, i: (bi, jnp.maximum(i * nsub - 1, 0), 0))
    kv_cur = pl.BlockSpec((None, nsub * blk, kv_w), lambda bi, i: (bi, i, 0))
    kv_next = pl.BlockSpec((None, blk, kv_w), lambda bi, i: (bi, jnp.minimum((i + 1) * nsub, nb * nsub - 1), 0))
    kv_ctx = pl.BlockSpec((None, l, kv_w), lambda bi, i: (bi, 0, 0))
    return pl.pallas_call(
        kern,
        grid=(b, nb),
        in_specs=[_const_spec(sink_rows.shape),
                  pl.BlockSpec((None, nsub * blk, att_w), lambda bi, i: (bi, i, 0)),
                  kv_prev, kv_cur, kv_next, kv_prev, kv_cur, kv_next, kv_ctx, kv_ctx],
        out_specs=pl.BlockSpec((None, nsub * blk, att_w), lambda bi, i: (bi, i, 0)),
        out_shape=jax.ShapeDtypeStruct((b, s, att_w), BF16),
        compiler_params=_cparams("parallel", "parallel"),
    )(sink_rows, q, k, k, k, v, v, v, kc, vc)


def _stack_heads(x, even):
    return jnp.concatenate([jnp.where(even, x, 0.0), jnp.where(even, 0.0, x)], axis=0)


def _chunk_summaries(insts):
    c2 = 2 * CHUNK
    even = lax.broadcasted_iota(jnp.int32, (CHUNK, LANES), 1) < HEAD_DIM
    rt = lax.broadcasted_iota(jnp.int32, (CHUNK, LANES), 0)
    ct = lax.broadcasted_iota(jnp.int32, (CHUNK, LANES), 1) % HEAD_DIM
    masks = {True: (ct < rt, ct <= rt), False: (ct > rt, ct >= rt)}
    eye = lax.broadcasted_iota(jnp.int32, (LANES, LANES), 0) == lax.broadcasted_iota(jnp.int32, (LANES, LANES), 1)

    prep = []
    for rr, vv, aa, bb, kd, cs, lw, tot, forward in insts:
        g_inv = jnp.exp(-cs)
        g_rem = jnp.exp(tot - cs)
        a_t = aa * jnp.exp(cs - lw)
        r_t = rr * jnp.exp(cs)
        v_s = _stack_heads(vv, even).astype(BF16)
        bk_s = jnp.concatenate([_stack_heads(bb * g_rem, even), _stack_heads(kd * g_rem, even)], axis=0).astype(BF16)
        lhs = jnp.concatenate([a_t, r_t], axis=0).astype(BF16)
        rhs = jnp.concatenate([_stack_heads(bb * g_inv, even), _stack_heads(kd * g_inv, even)], axis=0).astype(BF16)
        prep.append((_stack_heads(a_t, even), _stack_heads(r_t, even), v_s, bk_s, lhs, rhs, jnp.exp(tot), masks[forward]))

    gs = [_dot_nt(p[4], p[5]) for p in prep]
    zero = jnp.zeros((), BF16)
    tri, lfs = [], []
    for g, p in zip(gs, prep):
        strict, incl = p[7]
        lab = jnp.where(strict, g[:CHUNK, :c2], 0.0).astype(BF16)
        lfs.append(lab)
        tri.append(tuple(jnp.concatenate([jnp.where(even, f, zero), jnp.where(even, zero, f)], axis=0) for f in (
            lab, jnp.where(strict, g[:CHUNK, c2:], 0.0).astype(BF16),
            jnp.where(incl, g[CHUNK:, :c2], 0.0).astype(BF16), jnp.where(incl, g[CHUNK:, c2:], 0.0).astype(BF16))))
    own = (lax.broadcasted_iota(jnp.int32, (c2, LANES), 0) < CHUNK) == (
        lax.broadcasted_iota(jnp.int32, (c2, LANES), 1) < HEAD_DIM)
    xs = [p[0] + pltpu.roll(_dot(t[1], p[2]), HEAD_DIM, 1) for p, t in zip(prep, tri)]
    lps = [t[0] for t in tri]
    for step in range(NEUMANN_STEPS):
        xs = [x + _dot(lp, x) for x, lp in zip(xs, lps)]
        if step + 1 < NEUMANN_STEPS:
            lfs = [_dot(lf, lp).astype(BF16) for lf, lp in zip(lfs, lps)]
            lps = [jnp.concatenate([jnp.where(even, lf, zero), jnp.where(even, zero, lf)], axis=0) for lf in lfs]
    rys = [p[1] + pltpu.roll(_dot(t[3], p[2]), HEAD_DIM, 1) + _dot(t[2], x)
           for p, t, x in zip(prep, tri, xs)]
    out = []
    for p, x, ry in zip(prep, xs, rys):
        v_s, bk_s, g_tot = p[2], p[3], p[6]
        x_a = jnp.where(own, x, 0.0).astype(BF16)
        x_u = pltpu.roll(jnp.where(own, 0.0, x), HEAD_DIM, 1).astype(BF16)
        pt = jnp.where(eye, g_tot, 0.0) + _dot_tn(bk_s[:c2], x_a)
        qt = _dot_tn(bk_s, jnp.concatenate([x_u, v_s], axis=0))
        rh = jnp.where(even, ry[:CHUNK], ry[CHUNK:])
        yh = pltpu.roll(jnp.where(even, ry[CHUNK:], ry[:CHUNK]), HEAD_DIM, 1)
        out.append((pt[:CHUNK] + pt[CHUNK:], qt[:CHUNK] + qt[CHUNK:], rh, yh))
    return out


def _feat_kernel(pr_ref, hp_ref, hn_ref, mu_ref, w0_ref, a0_ref, w2_ref, a2_ref, g2_ref, kkp_ref, ka_ref, rk_ref,
                 ones_ref, pt_ref, qt_ref, rh_ref, yh_ref, bonus_ref, gate_ref, *, rw_w):
    i = pl.program_id(1)
    last = pl.num_programs(1) - 1
    tt = pr_ref.shape[0]
    pr = pr_ref[...]
    ridx = lax.broadcasted_iota(jnp.int32, (tt, 1), 0)
    edge_prev = jnp.where(i == 0, 0.0, hp_ref[SUBLANES - 1:SUBLANES, :])
    edge_next = jnp.where(i == last, 0.0, hn_ref[0:1, :])
    prev = jnp.where(ridx == 0, edge_prev, pltpu.roll(pr, 1, 0))
    nxt = jnp.where(ridx == tt - 1, edge_next, pltpu.roll(pr, tt - 1, 0))
    x = pr + mu_ref[...] * (0.5 * (prev + nxt) - pr)

    r = x[:, 0:rw_w]
    k = x[:, rw_w:2 * rw_w]
    v = x[:, 2 * rw_w:3 * rw_w]
    o = 3 * rw_w
    wd = x[:, o:o + LANES]
    ad = x[:, o + LANES:o + 2 * LANES]
    gd = x[:, o + 2 * LANES:o + 3 * LANES]

    ones_bd = ones_ref[...]
    logw = -DECAY_SCALE * _sigmoid(_dot(jnp.tanh(wd), w2_ref[...]) + w0_ref[...])
    iclr = _sigmoid(_dot(ad, a2_ref[...]) + a0_ref[...])
    gate_ref[...] = _dot(_sigmoid(gd), g2_ref[...])

    kkv = k * kkp_ref[...]
    kk = kkv / jnp.maximum(jnp.sqrt(_head_sum(kkv * kkv, ones_bd)), 1e-12)
    ka = ka_ref[...]
    k_dir = [k * (1.0 + (iclr[:, d * rw_w:(d + 1) * rw_w] - 1.0) * ka) for d in range(2)]
    bonus_ref[...] = _head_sum(r * (k_dir[0] + k_dir[1]) * rk_ref[...], ones_bd) * v

    row = lax.broadcasted_iota(jnp.int32, (tt, tt), 0)
    col = lax.broadcasted_iota(jnp.int32, (tt, tt), 1)
    same = (row // CHUNK) == (col // CHUNK)
    tri = [jnp.where(same & (col <= row), 1.0, 0.0).astype(BF16),
           jnp.where(same & (col >= row), 1.0, 0.0).astype(BF16)]
    neg_kk = -kk
    lw_dir, cs_dir, b_dir = [], [], []
    for d in range(2):
        lw_d = logw[:, d * rw_w:(d + 1) * rw_w]
        lw_dir.append(lw_d)
        cs_dir.append(sum(jnp.dot(tri[d], p, preferred_element_type=F32) for p in _split3(lw_d)))
        b_dir.append(kk * iclr[:, d * rw_w:(d + 1) * rw_w])
    for c in range(tt // CHUNK):
        rows = slice(c * CHUNK, (c + 1) * CHUNK)
        keys, insts = [], []
        for d in range(2):
            end = (c + 1) * CHUNK - 1 if d == 0 else c * CHUNK
            for p in range(rw_w // LANES):
                ln = slice(p * LANES, (p + 1) * LANES)
                keys.append((d, p, ln))
                insts.append((r[rows, ln], v[rows, ln], neg_kk[rows, ln], b_dir[d][rows, ln], k_dir[d][rows, ln],
                              cs_dir[d][rows, ln], lw_dir[d][rows, ln], cs_dir[d][end:end + 1, ln], d == 0))
        for (d, p, ln), (pt, qt, rh, yh) in zip(keys, _chunk_summaries(insts)):
            pt_ref[c, d, p] = pt
            qt_ref[c, d, p] = qt
            rh_ref[d, rows, ln] = rh.astype(rh_ref.dtype)
            yh_ref[d, rows, ln] = yh


def _rwkv_features(pr, fp, tt):
    b, t, w_all = pr.shape
    rw_w = fp["kkp"].shape[1]
    npair = rw_w // LANES
    nt = t // tt
    cpt = tt // CHUNK
    hb = tt // SUBLANES
    kern = functools.partial(_feat_kernel, rw_w=rw_w)
    names = ("mu", "w0", "a0", "w2", "a2", "g2", "kkp", "ka", "rk", "ones")
    return pl.pallas_call(
        kern,
        grid=(b, nt),
        in_specs=[pl.BlockSpec((None, tt, w_all), lambda bi, i: (bi, i, 0)),
                  pl.BlockSpec((None, SUBLANES, w_all), lambda bi, i: (bi, jnp.maximum(i * hb - 1, 0), 0)),
                  pl.BlockSpec((None, SUBLANES, w_all), lambda bi, i: (bi, jnp.minimum((i + 1) * hb, nt * hb - 1), 0))]
                 + [_const_spec(fp[n].shape) for n in names],
        out_specs=[pl.BlockSpec((None, cpt, 2, npair, CHUNK, LANES), lambda bi, i: (bi, i, 0, 0, 0, 0)),
                   pl.BlockSpec((None, cpt, 2, npair, CHUNK, LANES), lambda bi, i: (bi, i, 0, 0, 0, 0)),
                   pl.BlockSpec((None, 2, tt, rw_w), lambda bi, i: (bi, 0, i, 0)),
                   pl.BlockSpec((None, 2, tt, rw_w), lambda bi, i: (bi, 0, i, 0)),
                   pl.BlockSpec((None, tt, rw_w), lambda bi, i: (bi, i, 0)),
                   pl.BlockSpec((None, tt, rw_w), lambda bi, i: (bi, i, 0))],
        out_shape=[jax.ShapeDtypeStruct((b, t // CHUNK, 2, npair, CHUNK, LANES), F32),
                   jax.ShapeDtypeStruct((b, t // CHUNK, 2, npair, CHUNK, LANES), F32),
                   jax.ShapeDtypeStruct((b, 2, t, rw_w), BF16),
                   jax.ShapeDtypeStruct((b, 2, t, rw_w), F32),
                   jax.ShapeDtypeStruct((b, t, rw_w), F32),
                   jax.ShapeDtypeStruct((b, t, rw_w), F32)],
        compiler_params=_cparams("parallel", "parallel"),
    )(pr, pr, pr, *[fp[n] for n in names])


def _scan_kernel(s0_ref, ptf_ref, qtf_ref, ptb_ref, qtb_ref, rhf_ref, yhf_ref, rhb_ref, yhb_ref,
                 yf_ref, yb_ref, sfin_ref, st_ref, *, cps, npair):
    i = pl.program_id(1)

    @pl.when(i == 0)
    def _():
        st_ref[...] = s0_ref[...]

    even = lax.broadcasted_iota(jnp.int32, (CHUNK, LANES), 1) < HEAD_DIM
    dirs = ((ptf_ref, qtf_ref, rhf_ref, yhf_ref, yf_ref), (ptb_ref, qtb_ref, rhb_ref, yhb_ref, yb_ref))
    keys = [(d, p) for d in range(2) for p in range(npair)]
    st = [st_ref[d, p] for d, p in keys]
    for step in range(cps):
        hl = []
        for s in st:
            hi = s.astype(BF16)
            hl.append(jnp.concatenate([hi, (s - hi.astype(F32)).astype(BF16)], axis=1))
        new = []
        for (d, p), s2 in zip(keys, hl):
            pt_ref, qt_ref, rh_ref, yh_ref, y_ref = dirs[d]
            c = step if d == 0 else cps - 1 - step
            rows = slice(c * CHUNK, (c + 1) * CHUNK)
            ln = slice(p * LANES, (p + 1) * LANES)
            y2 = jnp.dot(rh_ref[rows, ln], s2, preferred_element_type=F32)
            y_ref[rows, ln] = y2[:, :LANES] + y2[:, LANES:] + yh_ref[rows, ln]
            pt_hi, pt_lo = _split2(_stack_heads(pt_ref[c, p], even))
            n2 = jnp.dot(pt_hi, s2, preferred_element_type=F32)
            new.append(n2[:, :LANES] + n2[:, LANES:] + jnp.dot(pt_lo, s2[:, :LANES], preferred_element_type=F32)
                       + _stack_heads(qt_ref[c, p], even))
        st = new
    for (d, p), s in zip(keys, st):
        st_ref[d, p] = s

    @pl.when(i == pl.num_programs(1) - 1)
    def _():
        sfin_ref[...] = st_ref[...]


def _rwkv_scan(s0, pt, qt, rh, yh, cps):
    b, nc, _, npair, _, _ = pt.shape
    t, rw_w = rh.shape[2], rh.shape[3]
    ns = nc // cps
    ts = cps * CHUNK
    kern = functools.partial(_scan_kernel, cps=cps, npair=npair)
    mat_f = pl.BlockSpec((None, cps, None, npair, CHUNK, LANES), lambda bi, i: (bi, i, 0, 0, 0, 0))
    mat_b = pl.BlockSpec((None, cps, None, npair, CHUNK, LANES), lambda bi, i: (bi, ns - 1 - i, 1, 0, 0, 0))
    tok_f = pl.BlockSpec((None, None, ts, rw_w), lambda bi, i: (bi, 0, i, 0))
    tok_b = pl.BlockSpec((None, None, ts, rw_w), lambda bi, i: (bi, 1, ns - 1 - i, 0))
    state = pl.BlockSpec((None, 2, npair, LANES, LANES), lambda bi, i: (bi, 0, 0, 0, 0))
    return pl.pallas_call(
        kern,
        grid=(b, ns),
        in_specs=[state, mat_f, mat_f, mat_b, mat_b, tok_f, tok_f, tok_b, tok_b],
        out_specs=[pl.BlockSpec((None, ts, rw_w), lambda bi, i: (bi, i, 0)),
                   pl.BlockSpec((None, ts, rw_w), lambda bi, i: (bi, ns - 1 - i, 0)),
                   state],
        out_shape=[jax.ShapeDtypeStruct((b, t, rw_w), F32),
                   jax.ShapeDtypeStruct((b, t, rw_w), F32),
                   jax.ShapeDtypeStruct((b, 2, npair, LANES, LANES), F32)],
        scratch_shapes=[pltpu.VMEM((2, npair, LANES, LANES), F32)],
        compiler_params=_cparams("parallel", "arbitrary"),
    )(s0, pt, qt, pt, qt, rh, yh, rh, yh)


def _residual_mlp(xl, yl, m, ng, w1_ref, w2_ref):
    x2 = xl + m[2:3] * _rms(yl, ng[1:2])
    hm = (_rms(x2, ng[2:3]) * (1.0 + m[4:5]) + m[3:4]).astype(BF16)
    out = None
    for j in range(w1_ref.shape[1] // MLP_HIDDEN_BLOCK):
        cols = slice(j * MLP_HIDDEN_BLOCK, (j + 1) * MLP_HIDDEN_BLOCK)
        hid = jnp.maximum(jnp.dot(hm, w1_ref[:, cols], preferred_element_type=F32), 0.0)
        part = jnp.dot((hid * hid).astype(BF16), w2_ref[cols, :], preferred_element_type=F32)
        out = part if out is None else out + part
    return x2 + m[5:6] * _rms(out, ng[3:4])


def _mix_out_kernel(x_ref, att_ref, yf_ref, yb_ref, bonus_ref, gate_ref, mod_ref, ng_ref, lnw_ref, lnb_ref, ones_ref,
                    wo_ref, w1_ref, w2_ref, o_ref, *, att_w):
    ones_bd = ones_ref[...]
    y = yf_ref[...] + yb_ref[...]
    mean = _head_sum(y, ones_bd) * (1.0 / HEAD_DIM)
    yc = y - mean
    var = _head_sum(yc * yc, ones_bd) * (1.0 / HEAD_DIM)
    yn = yc * lax.rsqrt(var + GN_EPS) * lnw_ref[...] + lnb_ref[...]
    rw = (yn + bonus_ref[...]) * gate_ref[...]
    yl = (jnp.dot(att_ref[...], wo_ref[0:att_w, :], preferred_element_type=F32)
          + jnp.dot(rw.astype(BF16), wo_ref[att_w:, :], preferred_element_type=F32))
    o_ref[...] = _residual_mlp(x_ref[...], yl, mod_ref[...], ng_ref[...], w1_ref, w2_ref)


def _mix_out_mlp(x, att, yf, yb, bonus, gate, mod_l, ng, lnw, lnb, ones_bd, wo, w1, w2, tm):
    b, s, d = x.shape
    att_w = att.shape[2]
    rw_w = yf.shape[2]
    kern = functools.partial(_mix_out_kernel, att_w=att_w)
    tok = lambda w: pl.BlockSpec((None, tm, w), lambda bi, i: (bi, i, 0))
    return pl.pallas_call(
        kern,
        grid=(b, s // tm),
        in_specs=[tok(d), tok(att_w), tok(rw_w), tok(rw_w), tok(rw_w), tok(rw_w),
                  pl.BlockSpec((None, 6, d), lambda bi, i: (bi, 0, 0)),
                  _const_spec(ng.shape), _const_spec(lnw.shape), _const_spec(lnb.shape), _const_spec(ones_bd.shape),
                  _const_spec(wo.shape), _const_spec(w1.shape), _const_spec(w2.shape)],
        out_specs=tok(d),
        out_shape=jax.ShapeDtypeStruct((b, s, d), F32),
        compiler_params=_cparams("parallel", "parallel"),
    )(x, att, yf, yb, bonus, gate, mod_l, ng, lnw, lnb, ones_bd, wo, w1, w2)


def _conv_kernel(x_ref, xp_ref, xn_ref, mod_ref, ng_ref, wi_ref, cw_ref, wo_ref, w1_ref, w2_ref, o_ref):
    i = pl.program_id(1)
    last = pl.num_programs(1) - 1
    d = x_ref.shape[1]
    tm = x_ref.shape[0]
    m = mod_ref[...]
    ng = ng_ref[...]

    def modnorm(u):
        return (_rms(u, ng[0:1]) * (1.0 + m[1:2]) + m[0:1]).astype(BF16)

    x = x_ref[...]
    proj = jnp.dot(modnorm(x), wi_ref[...], preferred_element_type=F32)
    z = proj[:, d:2 * d] * proj[:, 2 * d:]
    pp = jnp.dot(modnorm(xp_ref[...]), wi_ref[:, d:], preferred_element_type=F32)
    pn = jnp.dot(modnorm(xn_ref[...]), wi_ref[:, d:], preferred_element_type=F32)
    zp = jnp.where(i == 0, 0.0, (pp[:, :d] * pp[:, d:])[SUBLANES - 1:SUBLANES, :])
    zn = jnp.where(i == last, 0.0, (pn[:, :d] * pn[:, d:])[0:1, :])
    ridx = lax.broadcasted_iota(jnp.int32, (tm, 1), 0)
    prev = jnp.where(ridx == 0, zp, pltpu.roll(z, 1, 0))
    nxt = jnp.where(ridx == tm - 1, zn, pltpu.roll(z, tm - 1, 0))
    cw = cw_ref[...]
    y = proj[:, :d] * (prev * cw[0:1] + z * cw[1:2] + nxt * cw[2:3])
    yl = jnp.dot(y.astype(BF16), wo_ref[...], preferred_element_type=F32)
    o_ref[...] = _residual_mlp(x, yl, m, ng, w1_ref, w2_ref)


def _conv_mlp(x, mod_l, ng, wi, cw, wo, w1, w2, tm):
    b, s, d = x.shape
    hb = tm // SUBLANES
    nt = s // tm
    return pl.pallas_call(
        _conv_kernel,
        grid=(b, nt),
        in_specs=[pl.BlockSpec((None, tm, d), lambda bi, i: (bi, i, 0)),
                  pl.BlockSpec((None, SUBLANES, d), lambda bi, i: (bi, jnp.maximum(i * hb - 1, 0), 0)),
                  pl.BlockSpec((None, SUBLANES, d), lambda bi, i: (bi, jnp.minimum((i + 1) * hb, nt * hb - 1), 0)),
                  pl.BlockSpec((None, 6, d), lambda bi, i: (bi, 0, 0)),
                  _const_spec(ng.shape), _const_spec(wi.shape), _const_spec(cw.shape), _const_spec(wo.shape),
                  _const_spec(w1.shape), _const_spec(w2.shape)],
        out_specs=pl.BlockSpec((None, tm, d), lambda bi, i: (bi, i, 0)),
        out_shape=jax.ShapeDtypeStruct((b, s, d), F32),
        compiler_params=_cparams("parallel", "parallel"),
    )(x, x, x, mod_l, ng, wi, cw, wo, w1, w2)


def _block_diag2(m):
    z = jnp.zeros_like(m[0])
    return jnp.concatenate([jnp.concatenate([m[0], z], axis=1), jnp.concatenate([z, m[1]], axis=1)], axis=0)


def _pick_tile(n, want):
    t = min(n, want)
    while n % t:
        t //= 2
    return t


def kernel(x, c, ctx, c_ctx, mod_w, mod_b, norm_g, mlp_w1, mlp_w2, ab_w_in, ab_w_out, att_sink, rwkv_mu, rwkv_w0,
           rwkv_w2, rwkv_a0, rwkv_a2, rwkv_g2, rwkv_kk, rwkv_ka, rwkv_rk, rwkv_ln_w, rwkv_ln_b, conv_w_in, conv_w,
           conv_w_out):
    b, s, d = x.shape
    l = ctx.shape[1]
    depth = mod_w.shape[0]
    assert depth == 2, "layer schedule below is written for one attention/RWKV layer followed by one conv layer"
    att_w = att_sink.shape[1] * HEAD_DIM
    kv_w = att_w // ATT_GROUP
    rw_w = rwkv_kk.shape[1]
    rw_in = rwkv_mu.shape[1]
    assert s % ATT_BLOCK == 0 and s % CHUNK == 0 and l % CHUNK == 0 and kv_w % LANES == 0 and rw_w % LANES == 0

    rows = -(-(b + 1) // SUBLANES) * SUBLANES
    cv = jnp.concatenate([c, c_ctx[None, :], jnp.zeros((rows - b - 1, d), F32)], axis=0)
    mod = _modulation(cv, mod_w, mod_b)
    mod_l = [mod[i, :b].reshape(b, 6, d) for i in range(depth)]
    mod_c0 = mod[0, b].reshape(6, d)

    w_in = ab_w_in[0]
    wq, wk, wv, wr = (w_in[:, :att_w], w_in[:, att_w:att_w + kv_w], w_in[:, att_w + kv_w:att_w + 2 * kv_w],
                      w_in[:, att_w + 2 * kv_w:])
    n_heads = att_w // HEAD_DIM
    perm = jnp.arange(n_heads).reshape(n_heads // ATT_GROUP, ATT_GROUP).T.reshape(-1)
    wq = wq.reshape(d, n_heads, HEAD_DIM)[:, perm].reshape(d, att_w)
    w_lat = jnp.concatenate([wq, _rot_cols(wq), wk, _rot_cols(wk), wv, wr], axis=1).astype(BF16)
    w_ctx = jnp.concatenate([wk, wv, wr], axis=1).astype(BF16)
    cos, sin = _rope_tables(s)
    g0 = norm_g[0, 0].reshape(1, d)
    q, k, v, pr = _project_latent(x, mod_l[0], g0, w_lat, cos, sin, att_w, kv_w, rw_in, _pick_tile(s, PROJ_ROWS))
    kc, vc, prc = _project_ctx(ctx, mod_c0, g0, w_ctx, kv_w, rw_in)
    sink_rows = jnp.repeat(att_sink[0][perm].reshape(att_w // LANES, LANES // HEAD_DIM), ATT_BLOCK, axis=1)[..., None]
    att = _attention(q, k, v, kc, vc, sink_rows)
    w_out = ab_w_out[0]
    w_out = jnp.concatenate([w_out[:att_w].reshape(n_heads, HEAD_DIM, d)[perm].reshape(att_w, d), w_out[att_w:]], axis=0)

    head_id = jnp.arange(rw_w) // HEAD_DIM
    ones_bd = (head_id[:, None] == head_id[None, :]).astype(BF16)
    fp = dict(mu=rwkv_mu[0].reshape(1, rw_in),
              w0=rwkv_w0[0].reshape(1, 2 * rw_w), a0=rwkv_a0[0].reshape(1, 2 * rw_w),
              w2=_block_diag2(rwkv_w2[0]).astype(BF16), a2=_block_diag2(rwkv_a2[0]).astype(BF16),
              g2=rwkv_g2[0].astype(BF16),
              kkp=rwkv_kk[0].reshape(1, rw_w), ka=rwkv_ka[0].reshape(1, rw_w), rk=rwkv_rk[0].reshape(1, rw_w),
              ones=ones_bd)
    npair = rw_w // LANES
    ptc, qtc, rhc, yhc, _, _ = _rwkv_features(prc, fp, _pick_tile(l, FEAT_ROWS))
    zero_state = jnp.zeros((b, 2, npair, LANES, LANES), F32)
    _, _, s_ctx = _rwkv_scan(zero_state, ptc, qtc, rhc, yhc, _pick_tile(l // CHUNK, SCAN_CHUNKS))
    pt, qt, rh, yh, bonus, gate = _rwkv_features(pr, fp, _pick_tile(s, FEAT_ROWS))
    yf, yb, _ = _rwkv_scan(s_ctx, pt, qt, rh, yh, _pick_tile(s // CHUNK, SCAN_CHUNKS))

    xl = _mix_out_mlp(x, att, yf, yb, bonus, gate, mod_l[0], norm_g[0],
                      rwkv_ln_w[0].reshape(1, rw_w), rwkv_ln_b[0].reshape(1, rw_w), ones_bd,
                      w_out.astype(BF16), mlp_w1[0].astype(BF16), mlp_w2[0].astype(BF16), _pick_tile(s, MLP_ROWS))

    return _conv_mlp(xl, mod_l[1], norm_g[1], conv_w_in[0].astype(BF16), conv_w[0], conv_w_out[0].astype(BF16),
                     mlp_w1[1].astype(BF16), mlp_w2[1].astype(BF16), _pick_tile(s, MLP_ROWS))
```

```python
import functools
import math

import jax
import jax.numpy as jnp
from jax import lax
from jax.experimental import pallas as pl
from jax.experimental.pallas import tpu as pltpu

F32 = jnp.float32
BF16 = jnp.bfloat16

HEAD_DIM = 64
GRID_W = 64
WINDOW = 128
ATT_BLOCK = 128
ATT_GROUP = 4
ROPE_BASE = 10000.0
NORM_EPS = 1e-6
GN_EPS = 64e-5
LANES = 128
SUBLANES = 8
CHUNK = 64
DECAY_SCALE = math.exp(-0.5)
PROJ_ROWS = 512
FEAT_ROWS = 256
FEAT_CHUNK_GROUP = 2
SCAN_CHUNKS = 8
ATT_SUBBLOCKS = 4
MLP_ROWS = 512
MLP_HIDDEN_BLOCK = 2048
VMEM_LIMIT = 56 * 1024 * 1024


def _cparams(*sem):
    return pltpu.CompilerParams(dimension_semantics=sem, vmem_limit_bytes=VMEM_LIMIT)


def _const_spec(shape):
    nd = len(shape)
    return pl.BlockSpec(shape, lambda *_: (0,) * nd, pipeline_mode=pl.Buffered(1))


def _dot(a, b):
    return jnp.dot(a.astype(BF16), b.astype(BF16), preferred_element_type=F32)


def _dot_nt(a, b):
    return lax.dot_general(a.astype(BF16), b.astype(BF16), (((1,), (1,)), ((), ())), preferred_element_type=F32)


def _dot_tn(a, b):
    return lax.dot_general(a.astype(BF16), b.astype(BF16), (((0,), (0,)), ((), ())), preferred_element_type=F32)


def _split2(x):
    hi = x.astype(BF16)
    lo = (x - hi.astype(F32)).astype(BF16)
    return hi, lo


def _split3(x):
    hi = x.astype(BF16)
    r1 = x - hi.astype(F32)
    mid = r1.astype(BF16)
    lo = (r1 - mid.astype(F32)).astype(BF16)
    return hi, mid, lo


def _head_sum(x, ones_bd):
    hi, lo = _split2(x)
    return (jnp.dot(hi, ones_bd, preferred_element_type=F32)
            + jnp.dot(lo, ones_bd, preferred_element_type=F32))


def _rms(u, g):
    return u * lax.rsqrt(jnp.mean(u * u, axis=-1, keepdims=True) + NORM_EPS) * g


def _sigmoid(z):
    return 1.0 / (1.0 + jnp.exp(-z))


def _mod_kernel(cv_ref, w_ref, b_ref, o_ref):
    cv = cv_ref[...]
    s = cv * _sigmoid(cv)
    o_ref[...] = _dot(s, w_ref[...]) + b_ref[...]


def _modulation(cv, mod_w, mod_b):
    depth, d, six_d = mod_w.shape
    rows = cv.shape[0]
    nj = six_d // d
    return pl.pallas_call(
        _mod_kernel,
        grid=(depth, nj),
        in_specs=[pl.BlockSpec((rows, d), lambda l, j: (0, 0)),
                  pl.BlockSpec((None, d, d), lambda l, j: (l, 0, j)),
                  pl.BlockSpec((None, 1, d), lambda l, j: (l, 0, j))],
        out_specs=pl.BlockSpec((None, rows, d), lambda l, j: (l, 0, j)),
        out_shape=jax.ShapeDtypeStruct((depth, rows, six_d), F32),
        compiler_params=_cparams("arbitrary", "arbitrary"),
    )(cv, mod_w, mod_b.reshape(depth, 1, six_d))


def _proj_kernel(x_ref, mod_ref, g_ref, w_ref, cos_ref, sin_ref, q_ref, k_ref, v_ref, pr_ref, *, att_w, kv_w):
    m = mod_ref[...]
    h = (_rms(x_ref[...], g_ref[...]) * (1.0 + m[1:2]) + m[0:1]).astype(BF16)
    cos, sin = cos_ref[...], sin_ref[...]
    scale = HEAD_DIM ** -0.5
    p = jnp.dot(h, w_ref[...], preferred_element_type=F32)
    o = 0
    for j in range(att_w // LANES):
        u = p[:, o + j * LANES:o + (j + 1) * LANES]
        ur = p[:, o + att_w + j * LANES:o + att_w + (j + 1) * LANES]
        q_ref[:, j * LANES:(j + 1) * LANES] = ((u * cos + ur * sin) * scale).astype(q_ref.dtype)
    o += 2 * att_w
    for j in range(kv_w // LANES):
        u = p[:, o + j * LANES:o + (j + 1) * LANES]
        ur = p[:, o + kv_w + j * LANES:o + kv_w + (j + 1) * LANES]
        k_ref[:, j * LANES:(j + 1) * LANES] = (u * cos + ur * sin).astype(k_ref.dtype)
    o += 2 * kv_w
    v_ref[...] = p[:, o:o + kv_w].astype(v_ref.dtype)
    o += kv_w
    pr_ref[...] = p[:, o:]


def _proj_ctx_kernel(x_ref, mod_ref, g_ref, w_ref, k_ref, v_ref, pr_ref, *, kv_w):
    m = mod_ref[...]
    h = (_rms(x_ref[...], g_ref[...]) * (1.0 + m[1:2]) + m[0:1]).astype(BF16)
    k_ref[...] = jnp.dot(h, w_ref[:, 0:kv_w], preferred_element_type=F32).astype(k_ref.dtype)
    v_ref[...] = jnp.dot(h, w_ref[:, kv_w:2 * kv_w], preferred_element_type=F32).astype(v_ref.dtype)
    pr_ref[...] = jnp.dot(h, w_ref[:, 2 * kv_w:], preferred_element_type=F32)


def _rot_cols(w):
    d, n = w.shape
    m = HEAD_DIM // 4
    w4 = w.reshape(d, n // (2 * m), 2, m)
    return jnp.stack([-w4[:, :, 1], w4[:, :, 0]], axis=2).reshape(d, n)


def _rope_tables(seq):
    m = HEAD_DIM // 4
    t = jnp.arange(seq)
    inv = ROPE_BASE ** (-jnp.arange(m, dtype=F32) / m)
    ang_r = (t // GRID_W).astype(F32)[:, None] * inv[None, :]
    ang_c = (t % GRID_W).astype(F32)[:, None] * inv[None, :]
    ang = jnp.concatenate([ang_r, ang_r, ang_c, ang_c], axis=-1)
    ang = jnp.tile(ang, (1, LANES // HEAD_DIM))
    return jnp.cos(ang), jnp.sin(ang)


def _project_latent(x, mod_l, g, w_all, cos, sin, att_w, kv_w, rw_w, tm):
    b, s, d = x.shape
    n_all = w_all.shape[1]
    kern = functools.partial(_proj_kernel, att_w=att_w, kv_w=kv_w)
    return pl.pallas_call(
        kern,
        grid=(b, s // tm),
        in_specs=[pl.BlockSpec((None, tm, d), lambda bi, i: (bi, i, 0)),
                  pl.BlockSpec((None, 6, d), lambda bi, i: (bi, 0, 0)),
                  _const_spec((1, d)),
                  _const_spec((d, n_all)),
                  pl.BlockSpec((tm, LANES), lambda bi, i: (i, 0)),
                  pl.BlockSpec((tm, LANES), lambda bi, i: (i, 0))],
        out_specs=[pl.BlockSpec((None, tm, att_w), lambda bi, i: (bi, i, 0)),
                   pl.BlockSpec((None, tm, kv_w), lambda bi, i: (bi, i, 0)),
                   pl.BlockSpec((None, tm, kv_w), lambda bi, i: (bi, i, 0)),
                   pl.BlockSpec((None, tm, rw_w), lambda bi, i: (bi, i, 0))],
        out_shape=[jax.ShapeDtypeStruct((b, s, att_w), BF16),
                   jax.ShapeDtypeStruct((b, s, kv_w), BF16),
                   jax.ShapeDtypeStruct((b, s, kv_w), BF16),
                   jax.ShapeDtypeStruct((b, s, rw_w), F32)],
        compiler_params=_cparams("parallel", "parallel"),
    )(x, mod_l, g, w_all, cos, sin)


def _project_ctx(ctx, mod_c, g, w_ctx, kv_w, rw_w):
    b, l, d = ctx.shape
    kern = functools.partial(_proj_ctx_kernel, kv_w=kv_w)
    return pl.pallas_call(
        kern,
        grid=(b,),
        in_specs=[pl.BlockSpec((None, l, d), lambda bi: (bi, 0, 0)),
                  _const_spec((6, d)),
                  _const_spec((1, d)),
                  _const_spec((d, w_ctx.shape[1]))],
        out_specs=[pl.BlockSpec((None, l, kv_w), lambda bi: (bi, 0, 0)),
                   pl.BlockSpec((None, l, kv_w), lambda bi: (bi, 0, 0)),
                   pl.BlockSpec((None, l, rw_w), lambda bi: (bi, 0, 0))],
        out_shape=[jax.ShapeDtypeStruct((b, l, kv_w), BF16),
                   jax.ShapeDtypeStruct((b, l, kv_w), BF16),
                   jax.ShapeDtypeStruct((b, l, rw_w), F32)],
        compiler_params=_cparams("parallel"),
    )(ctx, mod_c, g, w_ctx)


def _attn_kernel(sink_ref, q_ref, kp_ref, kc_ref, kn_ref, vp_ref, vc_ref, vn_ref, kx_ref, vx_ref, o_ref, *, nsub):
    i = pl.program_id(1)
    nb = pl.num_programs(1)
    blk = ATT_BLOCK
    kall = jnp.concatenate([kp_ref[...], kc_ref[...], kn_ref[...]], axis=0)
    vall = jnp.concatenate([vp_ref[...], vc_ref[...], vn_ref[...]], axis=0)
    vall = jnp.concatenate([vall, jnp.ones_like(vall)], axis=1)
    kctx = kx_ref[...]
    vctx = jnp.concatenate([vx_ref[...], jnp.ones_like(vx_ref[...])], axis=1)
    row = lax.broadcasted_iota(jnp.int32, (2 * blk, 3 * blk), 0) % blk
    col = lax.broadcasted_iota(jnp.int32, (2 * blk, 3 * blk), 1)
    band = jnp.abs(row + blk - col) <= WINDOW
    even = lax.broadcasted_iota(jnp.int32, (blk, LANES), 1) < HEAD_DIM
    zero = jnp.zeros((), q_ref.dtype)
    keys = [(a, j) for a in range(nsub) for j in range(q_ref.shape[1] // LANES)]
    qs, valid, kloc, vloc = [], [], [], []
    for a in range(nsub):
        va = band
        if a == 0:
            va = va & jnp.logical_not((i == 0) & (col < blk))
        if a == nsub - 1:
            va = va & jnp.logical_not((i == nb - 1) & (col >= 2 * blk))
        valid.append(va)
        kloc.append(kall[a * blk:(a + 3) * blk])
        vloc.append(vall[a * blk:(a + 3) * blk])
    for a, j in keys:
        qj = q_ref[a * blk:(a + 1) * blk, j * LANES:(j + 1) * LANES]
        qs.append(jnp.concatenate([jnp.where(even, qj, zero), jnp.where(even, zero, qj)], axis=0))
    s_loc = [jnp.where(valid[a], _dot_nt(qj, kloc[a]), -1e30) for (a, j), qj in zip(keys, qs)]
    s_ctx = [_dot_nt(qj, kctx) for qj in qs]
    sinks = [sink_ref[j] for a, j in keys]
    mx = [jnp.maximum(jnp.maximum(jnp.max(sl, axis=-1, keepdims=True), jnp.max(sc, axis=-1, keepdims=True)), sk)
          for sl, sc, sk in zip(s_loc, s_ctx, sinks)]
    p_loc = [jnp.exp((sl - m).astype(BF16)) for sl, m in zip(s_loc, mx)]
    p_ctx = [jnp.exp((sc - m).astype(BF16)) for sc, m in zip(s_ctx, mx)]
    num = [jnp.dot(pa, vloc[a], preferred_element_type=F32) + jnp.dot(pc, vctx, preferred_element_type=F32)
           for (a, j), pa, pc in zip(keys, p_loc, p_ctx)]
    outs = [n[:, :LANES] / (n[:, LANES:] + jnp.exp(sk - m)) for n, sk, m in zip(num, sinks, mx)]
    for (a, j), o in zip(keys, outs):
        o_ref[a * blk:(a + 1) * blk, j * LANES:(j + 1) * LANES] = jnp.where(even, o[:blk], o[blk:]).astype(o_ref.dtype)


def _attention(q, k, v, kc, vc, sink_rows):
    b, s, att_w = q.shape
    kv_w = k.shape[2]
    l = kc.shape[1]
    blk = ATT_BLOCK
    nsub = _pick_tile(s // blk, ATT_SUBBLOCKS)
    nb = s // (nsub * blk)
    kern = functools.partial(_attn_kernel, nsub=nsub)
    kv_prev = pl.BlockSpec((None, blk, kv_w), lambda bi, i: (bi, jnp.maximum(i * nsub - 1, 0), 0))
    kv_cur = pl.BlockSpec((None, nsub * blk, kv_w), lambda bi, i: (bi, i, 0))
    kv_next = pl.BlockSpec((None, blk, kv_w), lambda bi, i: (bi, jnp.minimum((i + 1) * nsub, nb * nsub - 1), 0))
    kv_ctx = pl.BlockSpec((None, l, kv_w), lambda bi, i: (bi, 0, 0))
    return pl.pallas_call(
        kern,
        grid=(b, nb),
        in_specs=[_const_spec(sink_rows.shape),
                  pl.BlockSpec((None, nsub * blk, att_w), lambda bi, i: (bi, i, 0)),
                  kv_prev, kv_cur, kv_next, kv_prev, kv_cur, kv_next, kv_ctx, kv_ctx],
        out_specs=pl.BlockSpec((None, nsub * blk, att_w), lambda bi, i: (bi, i, 0)),
        out_shape=jax.ShapeDtypeStruct((b, s, att_w), BF16),
        compiler_params=_cparams("parallel", "parallel"),
    )(sink_rows, q, k, k, k, v, v, v, kc, vc)


def _stack_heads(x, even):
    return jnp.concatenate([jnp.where(even, x, 0.0), jnp.where(even, 0.0, x)], axis=0)


def _chunk_summaries(insts):
    c2 = 2 * CHUNK
    even = lax.broadcasted_iota(jnp.int32, (CHUNK, LANES), 1) < HEAD_DIM
    rt = lax.broadcasted_iota(jnp.int32, (CHUNK, LANES), 0)
    ct = lax.broadcasted_iota(jnp.int32, (CHUNK, LANES), 1) % HEAD_DIM
    masks = {True: (ct < rt, ct <= rt), False: (ct > rt, ct >= rt)}
    eye = lax.broadcasted_iota(jnp.int32, (LANES, LANES), 0) == lax.broadcasted_iota(jnp.int32, (LANES, LANES), 1)

    prep = []
    for rr, vv, aa, bb, kd, cs, lw, tot, forward in insts:
        g_inv = jnp.exp(-cs)
        g_rem = jnp.exp(tot - cs)
        a_t = aa * jnp.exp(cs - lw)
        r_t = rr * jnp.exp(cs)
        v_s = _stack_heads(vv, even).astype(BF16)
        bk_s = jnp.concatenate([_stack_heads(bb * g_rem, even), _stack_heads(kd * g_rem, even)], axis=0).astype(BF16)
        lhs = jnp.concatenate([a_t, r_t], axis=0).astype(BF16)
        rhs = jnp.concatenate([_stack_heads(bb * g_inv, even), _stack_heads(kd * g_inv, even)], axis=0).astype(BF16)
        prep.append((_stack_heads(a_t, even), _stack_heads(r_t, even), v_s, bk_s, lhs, rhs, jnp.exp(tot), masks[forward]))

    gs = [_dot_nt(p[4], p[5]) for p in prep]
    zero = jnp.zeros((), BF16)
    tri, lfs = [], []
    for g, p in zip(gs, prep):
        strict, incl = p[7]
        lfs.append(jnp.where(strict, g[:CHUNK, :c2], 0.0).astype(BF16))
        tri.append(tuple(jnp.concatenate([jnp.where(even, f, zero), jnp.where(even, zero, f)], axis=0) for f in (
            jnp.where(strict, g[:CHUNK, c2:], 0.0).astype(BF16),
            jnp.where(incl, g[CHUNK:, :c2], 0.0).astype(BF16), jnp.where(incl, g[CHUNK:, c2:], 0.0).astype(BF16))))
    own = (lax.broadcasted_iota(jnp.int32, (c2, LANES), 0) < CHUNK) == (
        lax.broadcasted_iota(jnp.int32, (c2, LANES), 1) < HEAD_DIM)
    xs = [p[0] + pltpu.roll(_dot(t[0], p[2]), HEAD_DIM, 1) for p, t in zip(prep, tri)]

    def stacked(f):
        f = f.astype(BF16)
        return jnp.concatenate([jnp.where(even, f, zero), jnp.where(even, zero, f)], axis=0)

    ts = [jnp.where((rt // 2 == ct // 2), lf.astype(F32), 0.0) + jnp.where(rt == ct, 1.0, 0.0) for lf in lfs]
    size = 2
    while size < CHUNK:
        couple = (rt // (2 * size) == ct // (2 * size)) & (rt // size != ct // size)
        os_ = [jnp.where(couple, lf, zero) for lf in lfs]
        ots = [_dot(o, stacked(t)) for o, t in zip(os_, ts)]
        ts = [t + _dot(t, stacked(ot)) for t, ot in zip(ts, ots)]
        size *= 2
    xs = [_dot(stacked(t), x) for t, x in zip(ts, xs)]
    rys = [p[1] + pltpu.roll(_dot(t[2], p[2]), HEAD_DIM, 1) + _dot(t[1], x)
           for p, t, x in zip(prep, tri, xs)]
    out = []
    for p, x, ry in zip(prep, xs, rys):
        v_s, bk_s, g_tot = p[2], p[3], p[6]
        x_a = jnp.where(own, x, 0.0).astype(BF16)
        x_u = pltpu.roll(jnp.where(own, 0.0, x), HEAD_DIM, 1).astype(BF16)
        pt = jnp.where(eye, g_tot, 0.0) + _dot_tn(bk_s[:c2], x_a)
        qt = _dot_tn(bk_s, jnp.concatenate([x_u, v_s], axis=0))
        rh = jnp.where(even, ry[:CHUNK], ry[CHUNK:])
        yh = pltpu.roll(jnp.where(even, ry[CHUNK:], ry[:CHUNK]), HEAD_DIM, 1)
        out.append((pt[:CHUNK] + pt[CHUNK:], qt[:CHUNK] + qt[CHUNK:], rh, yh))
    return out


def _feat_kernel(pr_ref, hp_ref, hn_ref, mu_ref, w0_ref, a0_ref, w2_ref, a2_ref, g2_ref, kkp_ref, ka_ref, rk_ref,
                 ones_ref, pt_ref, qt_ref, rh_ref, yh_ref, bonus_ref, gate_ref, *, rw_w):
    i = pl.program_id(1)
    last = pl.num_programs(1) - 1
    tt = pr_ref.shape[0]
    pr = pr_ref[...]
    ridx = lax.broadcasted_iota(jnp.int32, (tt, 1), 0)
    edge_prev = jnp.where(i == 0, 0.0, hp_ref[SUBLANES - 1:SUBLANES, :])
    edge_next = jnp.where(i == last, 0.0, hn_ref[0:1, :])
    prev = jnp.where(ridx == 0, edge_prev, pltpu.roll(pr, 1, 0))
    nxt = jnp.where(ridx == tt - 1, edge_next, pltpu.roll(pr, tt - 1, 0))
    x = pr + mu_ref[...] * (0.5 * (prev + nxt) - pr)

    r = x[:, 0:rw_w]
    k = x[:, rw_w:2 * rw_w]
    v = x[:, 2 * rw_w:3 * rw_w]
    o = 3 * rw_w
    wd = x[:, o:o + LANES]
    ad = x[:, o + LANES:o + 2 * LANES]
    gd = x[:, o + 2 * LANES:o + 3 * LANES]

    ones_bd = ones_ref[...]
    logw = -DECAY_SCALE * _sigmoid(_dot(jnp.tanh(wd), w2_ref[...]) + w0_ref[...])
    iclr = _sigmoid(_dot(ad, a2_ref[...]) + a0_ref[...])
    gate_ref[...] = _dot(_sigmoid(gd), g2_ref[...])

    kkv = k * kkp_ref[...]
    kk = kkv / jnp.maximum(jnp.sqrt(_head_sum(kkv * kkv, ones_bd)), 1e-12)
    ka = ka_ref[...]
    k_dir = [k * (1.0 + (iclr[:, d * rw_w:(d + 1) * rw_w] - 1.0) * ka) for d in range(2)]
    bonus_ref[...] = _head_sum(r * (k_dir[0] + k_dir[1]) * rk_ref[...], ones_bd) * v

    row = lax.broadcasted_iota(jnp.int32, (tt, tt), 0)
    col = lax.broadcasted_iota(jnp.int32, (tt, tt), 1)
    same = (row // CHUNK) == (col // CHUNK)
    tri = [jnp.where(same & (col <= row), 1.0, 0.0).astype(BF16),
           jnp.where(same & (col >= row), 1.0, 0.0).astype(BF16)]
    neg_kk = -kk
    lw_dir, cs_dir, b_dir = [], [], []
    for d in range(2):
        lw_d = logw[:, d * rw_w:(d + 1) * rw_w]
        lw_dir.append(lw_d)
        cs_dir.append(sum(jnp.dot(tri[d], p, preferred_element_type=F32) for p in _split3(lw_d)))
        b_dir.append(kk * iclr[:, d * rw_w:(d + 1) * rw_w])
    nchunks = tt // CHUNK
    group = _pick_tile(nchunks, FEAT_CHUNK_GROUP)
    for c0 in range(0, nchunks, group):
        keys, insts = [], []
        for c in range(c0, c0 + group):
            rows = slice(c * CHUNK, (c + 1) * CHUNK)
            for d in range(2):
                end = (c + 1) * CHUNK - 1 if d == 0 else c * CHUNK
                for p in range(rw_w // LANES):
                    ln = slice(p * LANES, (p + 1) * LANES)
                    keys.append((c, d, p, rows, ln))
                    insts.append((r[rows, ln], v[rows, ln], neg_kk[rows, ln], b_dir[d][rows, ln], k_dir[d][rows, ln],
                                  cs_dir[d][rows, ln], lw_dir[d][rows, ln], cs_dir[d][end:end + 1, ln], d == 0))
        for (c, d, p, rows, ln), (pt, qt, rh, yh) in zip(keys, _chunk_summaries(insts)):
            pt_ref[c, d, p] = pt
            qt_ref[c, d, p] = qt
            rh_ref[d, rows, ln] = rh.astype(rh_ref.dtype)
            yh_ref[d, rows, ln] = yh


def _rwkv_features(pr, fp, tt):
    b, t, w_all = pr.shape
    rw_w = fp["kkp"].shape[1]
    npair = rw_w // LANES
    nt = t // tt
    cpt = tt // CHUNK
    hb = tt // SUBLANES
    kern = functools.partial(_feat_kernel, rw_w=rw_w)
    names = ("mu", "w0", "a0", "w2", "a2", "g2", "kkp", "ka", "rk", "ones")
    return pl.pallas_call(
        kern,
        grid=(b, nt),
        in_specs=[pl.BlockSpec((None, tt, w_all), lambda bi, i: (bi, i, 0)),
                  pl.BlockSpec((None, SUBLANES, w_all), lambda bi, i: (bi, jnp.maximum(i * hb - 1, 0), 0)),
                  pl.BlockSpec((None, SUBLANES, w_all), lambda bi, i: (bi, jnp.minimum((i + 1) * hb, nt * hb - 1), 0))]
                 + [_const_spec(fp[n].shape) for n in names],
        out_specs=[pl.BlockSpec((None, cpt, 2, npair, CHUNK, LANES), lambda bi, i: (bi, i, 0, 0, 0, 0)),
                   pl.BlockSpec((None, cpt, 2, npair, CHUNK, LANES), lambda bi, i: (bi, i, 0, 0, 0, 0)),
                   pl.BlockSpec((None, 2, tt, rw_w), lambda bi, i: (bi, 0, i, 0)),
                   pl.BlockSpec((None, 2, tt, rw_w), lambda bi, i: (bi, 0, i, 0)),
                   pl.BlockSpec((None, tt, rw_w), lambda bi, i: (bi, i, 0)),
                   pl.BlockSpec((None, tt, rw_w), lambda bi, i: (bi, i, 0))],
        out_shape=[jax.ShapeDtypeStruct((b, t // CHUNK, 2, npair, CHUNK, LANES), F32),
                   jax.ShapeDtypeStruct((b, t // CHUNK, 2, npair, CHUNK, LANES), F32),
                   jax.ShapeDtypeStruct((b, 2, t, rw_w), BF16),
                   jax.ShapeDtypeStruct((b, 2, t, rw_w), F32),
                   jax.ShapeDtypeStruct((b, t, rw_w), F32),
                   jax.ShapeDtypeStruct((b, t, rw_w), F32)],
        compiler_params=_cparams("parallel", "parallel"),
    )(pr, pr, pr, *[fp[n] for n in names])


def _scan_kernel(s0_ref, ptf_ref, qtf_ref, ptb_ref, qtb_ref, rhf_ref, yhf_ref, rhb_ref, yhb_ref,
                 yf_ref, yb_ref, sfin_ref, st_ref, *, cps, npair):
    i = pl.program_id(1)

    @pl.when(i == 0)
    def _():
        st_ref[...] = s0_ref[...]

    even = lax.broadcasted_iota(jnp.int32, (CHUNK, LANES), 1) < HEAD_DIM
    dirs = ((ptf_ref, qtf_ref, rhf_ref, yhf_ref, yf_ref), (ptb_ref, qtb_ref, rhb_ref, yhb_ref, yb_ref))
    keys = [(d, p) for d in range(2) for p in range(npair)]
    st = [st_ref[d, p] for d, p in keys]
    for step in range(cps):
        hl = []
        for s in st:
            hi = s.astype(BF16)
            hl.append(jnp.concatenate([hi, (s - hi.astype(F32)).astype(BF16)], axis=1))
        new = []
        for (d, p), s2 in zip(keys, hl):
            pt_ref, qt_ref, rh_ref, yh_ref, y_ref = dirs[d]
            c = step if d == 0 else cps - 1 - step
            rows = slice(c * CHUNK, (c + 1) * CHUNK)
            ln = slice(p * LANES, (p + 1) * LANES)
            y2 = jnp.dot(rh_ref[rows, ln], s2, preferred_element_type=F32)
            y_ref[rows, ln] = y2[:, :LANES] + y2[:, LANES:] + yh_ref[rows, ln]
            pt_hi, pt_lo = _split2(_stack_heads(pt_ref[c, p], even))
            n2 = jnp.dot(pt_hi, s2, preferred_element_type=F32)
            new.append(n2[:, :LANES] + n2[:, LANES:] + jnp.dot(pt_lo, s2[:, :LANES], preferred_element_type=F32)
                       + _stack_heads(qt_ref[c, p], even))
        st = new
    for (d, p), s in zip(keys, st):
        st_ref[d, p] = s

    @pl.when(i == pl.num_programs(1) - 1)
    def _():
        sfin_ref[...] = st_ref[...]


def _rwkv_scan(s0, pt, qt, rh, yh, cps):
    b, nc, _, npair, _, _ = pt.shape
    t, rw_w = rh.shape[2], rh.shape[3]
    ns = nc // cps
    ts = cps * CHUNK
    kern = functools.partial(_scan_kernel, cps=cps, npair=npair)
    mat_f = pl.BlockSpec((None, cps, None, npair, CHUNK, LANES), lambda bi, i: (bi, i, 0, 0, 0, 0))
    mat_b = pl.BlockSpec((None, cps, None, npair, CHUNK, LANES), lambda bi, i: (bi, ns - 1 - i, 1, 0, 0, 0))
    tok_f = pl.BlockSpec((None, None, ts, rw_w), lambda bi, i: (bi, 0, i, 0))
    tok_b = pl.BlockSpec((None, None, ts, rw_w), lambda bi, i: (bi, 1, ns - 1 - i, 0))
    state = pl.BlockSpec((None, 2, npair, LANES, LANES), lambda bi, i: (bi, 0, 0, 0, 0))
    return pl.pallas_call(
        kern,
        grid=(b, ns),
        in_specs=[state, mat_f, mat_f, mat_b, mat_b, tok_f, tok_f, tok_b, tok_b],
        out_specs=[pl.BlockSpec((None, ts, rw_w), lambda bi, i: (bi, i, 0)),
                   pl.BlockSpec((None, ts, rw_w), lambda bi, i: (bi, ns - 1 - i, 0)),
                   state],
        out_shape=[jax.ShapeDtypeStruct((b, t, rw_w), F32),
                   jax.ShapeDtypeStruct((b, t, rw_w), F32),
                   jax.ShapeDtypeStruct((b, 2, npair, LANES, LANES), F32)],
        scratch_shapes=[pltpu.VMEM((2, npair, LANES, LANES), F32)],
        compiler_params=_cparams("parallel", "arbitrary"),
    )(s0, pt, qt, pt, qt, rh, yh, rh, yh)


def _residual_mlp(xl, yl, m, ng, w1_ref, w2_ref):
    x2 = xl + m[2:3] * _rms(yl, ng[1:2])
    hm = (_rms(x2, ng[2:3]) * (1.0 + m[4:5]) + m[3:4]).astype(BF16)
    out = None
    for j in range(w1_ref.shape[1] // MLP_HIDDEN_BLOCK):
        cols = slice(j * MLP_HIDDEN_BLOCK, (j + 1) * MLP_HIDDEN_BLOCK)
        hid = jnp.maximum(jnp.dot(hm, w1_ref[:, cols], preferred_element_type=F32), 0.0)
        part = jnp.dot((hid * hid).astype(BF16), w2_ref[cols, :], preferred_element_type=F32)
        out = part if out is None else out + part
    return x2 + m[5:6] * _rms(out, ng[3:4])


def _mix_out_kernel(x_ref, att_ref, yf_ref, yb_ref, bonus_ref, gate_ref, mod_ref, ng_ref, lnw_ref, lnb_ref, ones_ref,
                    wo_ref, w1_ref, w2_ref, o_ref, *, att_w):
    ones_bd = ones_ref[...]
    y = yf_ref[...] + yb_ref[...]
    mean = _head_sum(y, ones_bd) * (1.0 / HEAD_DIM)
    yc = y - mean
    var = _head_sum(yc * yc, ones_bd) * (1.0 / HEAD_DIM)
    yn = yc * lax.rsqrt(var + GN_EPS) * lnw_ref[...] + lnb_ref[...]
    rw = (yn + bonus_ref[...]) * gate_ref[...]
    yl = (jnp.dot(att_ref[...], wo_ref[0:att_w, :], preferred_element_type=F32)
          + jnp.dot(rw.astype(BF16), wo_ref[att_w:, :], preferred_element_type=F32))
    o_ref[...] = _residual_mlp(x_ref[...], yl, mod_ref[...], ng_ref[...], w1_ref, w2_ref)


def _mix_out_mlp(x, att, yf, yb, bonus, gate, mod_l, ng, lnw, lnb, ones_bd, wo, w1, w2, tm):
    b, s, d = x.shape
    att_w = att.shape[2]
    rw_w = yf.shape[2]
    kern = functools.partial(_mix_out_kernel, att_w=att_w)
    tok = lambda w: pl.BlockSpec((None, tm, w), lambda bi, i: (bi, i, 0))
    return pl.pallas_call(
        kern,
        grid=(b, s // tm),
        in_specs=[tok(d), tok(att_w), tok(rw_w), tok(rw_w), tok(rw_w), tok(rw_w),
                  pl.BlockSpec((None, 6, d), lambda bi, i: (bi, 0, 0)),
                  _const_spec(ng.shape), _const_spec(lnw.shape), _const_spec(lnb.shape), _const_spec(ones_bd.shape),
                  _const_spec(wo.shape), _const_spec(w1.shape), _const_spec(w2.shape)],
        out_specs=tok(d),
        out_shape=jax.ShapeDtypeStruct((b, s, d), F32),
        compiler_params=_cparams("parallel", "parallel"),
    )(x, att, yf, yb, bonus, gate, mod_l, ng, lnw, lnb, ones_bd, wo, w1, w2)


def _conv_kernel(x_ref, xp_ref, xn_ref, mod_ref, ng_ref, wi_ref, cw_ref, wo_ref, w1_ref, w2_ref, o_ref):
    i = pl.program_id(1)
    last = pl.num_programs(1) - 1
    d = x_ref.shape[1]
    tm = x_ref.shape[0]
    m = mod_ref[...]
    ng = ng_ref[...]

    def modnorm(u):
        return (_rms(u, ng[0:1]) * (1.0 + m[1:2]) + m[0:1]).astype(BF16)

    x = x_ref[...]
    proj = jnp.dot(modnorm(x), wi_ref[...], preferred_element_type=F32)
    z = proj[:, d:2 * d] * proj[:, 2 * d:]
    pp = jnp.dot(modnorm(xp_ref[...]), wi_ref[:, d:], preferred_element_type=F32)
    pn = jnp.dot(modnorm(xn_ref[...]), wi_ref[:, d:], preferred_element_type=F32)
    zp = jnp.where(i == 0, 0.0, (pp[:, :d] * pp[:, d:])[SUBLANES - 1:SUBLANES, :])
    zn = jnp.where(i == last, 0.0, (pn[:, :d] * pn[:, d:])[0:1, :])
    ridx = lax.broadcasted_iota(jnp.int32, (tm, 1), 0)
    prev = jnp.where(ridx == 0, zp, pltpu.roll(z, 1, 0))
    nxt = jnp.where(ridx == tm - 1, zn, pltpu.roll(z, tm - 1, 0))
    cw = cw_ref[...]
    y = proj[:, :d] * (prev * cw[0:1] + z * cw[1:2] + nxt * cw[2:3])
    yl = jnp.dot(y.astype(BF16), wo_ref[...], preferred_element_type=F32)
    o_ref[...] = _residual_mlp(x, yl, m, ng, w1_ref, w2_ref)


def _conv_mlp(x, mod_l, ng, wi, cw, wo, w1, w2, tm):
    b, s, d = x.shape
    hb = tm // SUBLANES
    nt = s // tm
    return pl.pallas_call(
        _conv_kernel,
        grid=(b, nt),
        in_specs=[pl.BlockSpec((None, tm, d), lambda bi, i: (bi, i, 0)),
                  pl.BlockSpec((None, SUBLANES, d), lambda bi, i: (bi, jnp.maximum(i * hb - 1, 0), 0)),
                  pl.BlockSpec((None, SUBLANES, d), lambda bi, i: (bi, jnp.minimum((i + 1) * hb, nt * hb - 1), 0)),
                  pl.BlockSpec((None, 6, d), lambda bi, i: (bi, 0, 0)),
                  _const_spec(ng.shape), _const_spec(wi.shape), _const_spec(cw.shape), _const_spec(wo.shape),
                  _const_spec(w1.shape), _const_spec(w2.shape)],
        out_specs=pl.BlockSpec((None, tm, d), lambda bi, i: (bi, i, 0)),
        out_shape=jax.ShapeDtypeStruct((b, s, d), F32),
        compiler_params=_cparams("parallel", "parallel"),
    )(x, x, x, mod_l, ng, wi, cw, wo, w1, w2)


def _block_diag2(m):
    z = jnp.zeros_like(m[0])
    return jnp.concatenate([jnp.concatenate([m[0], z], axis=1), jnp.concatenate([z, m[1]], axis=1)], axis=0)


def _pick_tile(n, want):
    t = min(n, want)
    while n % t:
        t //= 2
    return t


def kernel(x, c, ctx, c_ctx, mod_w, mod_b, norm_g, mlp_w1, mlp_w2, ab_w_in, ab_w_out, att_sink, rwkv_mu, rwkv_w0,
           rwkv_w2, rwkv_a0, rwkv_a2, rwkv_g2, rwkv_kk, rwkv_ka, rwkv_rk, rwkv_ln_w, rwkv_ln_b, conv_w_in, conv_w,
           conv_w_out):
    b, s, d = x.shape
    l = ctx.shape[1]
    depth = mod_w.shape[0]
    assert depth == 2, "layer schedule below is written for one attention/RWKV layer followed by one conv layer"
    att_w = att_sink.shape[1] * HEAD_DIM
    kv_w = att_w // ATT_GROUP
    rw_w = rwkv_kk.shape[1]
    rw_in = rwkv_mu.shape[1]
    assert s % ATT_BLOCK == 0 and s % CHUNK == 0 and l % CHUNK == 0 and kv_w % LANES == 0 and rw_w % LANES == 0

    rows = -(-(b + 1) // SUBLANES) * SUBLANES
    cv = jnp.concatenate([c, c_ctx[None, :], jnp.zeros((rows - b - 1, d), F32)], axis=0)
    mod = _modulation(cv, mod_w, mod_b)
    mod_l = [mod[i, :b].reshape(b, 6, d) for i in range(depth)]
    mod_c0 = mod[0, b].reshape(6, d)

    w_in = ab_w_in[0]
    wq, wk, wv, wr = (w_in[:, :att_w], w_in[:, att_w:att_w + kv_w], w_in[:, att_w + kv_w:att_w + 2 * kv_w],
                      w_in[:, att_w + 2 * kv_w:])
    n_heads = att_w // HEAD_DIM
    perm = jnp.arange(n_heads).reshape(n_heads // ATT_GROUP, ATT_GROUP).T.reshape(-1)
    wq = wq.reshape(d, n_heads, HEAD_DIM)[:, perm].reshape(d, att_w)
    w_lat = jnp.concatenate([wq, _rot_cols(wq), wk, _rot_cols(wk), wv, wr], axis=1).astype(BF16)
    w_ctx = jnp.concatenate([wk, wv, wr], axis=1).astype(BF16)
    cos, sin = _rope_tables(s)
    g0 = norm_g[0, 0].reshape(1, d)
    q, k, v, pr = _project_latent(x, mod_l[0], g0, w_lat, cos, sin, att_w, kv_w, rw_in, _pick_tile(s, PROJ_ROWS))
    kc, vc, prc = _project_ctx(ctx, mod_c0, g0, w_ctx, kv_w, rw_in)
    sink_rows = jnp.repeat(att_sink[0][perm].reshape(att_w // LANES, LANES // HEAD_DIM), ATT_BLOCK, axis=1)[..., None]
    att = _attention(q, k, v, kc, vc, sink_rows)
    w_out = ab_w_out[0]
    w_out = jnp.concatenate([w_out[:att_w].reshape(n_heads, HEAD_DIM, d)[perm].reshape(att_w, d), w_out[att_w:]], axis=0)

    head_id = jnp.arange(rw_w) // HEAD_DIM
    ones_bd = (head_id[:, None] == head_id[None, :]).astype(BF16)
    fp = dict(mu=rwkv_mu[0].reshape(1, rw_in),
              w0=rwkv_w0[0].reshape(1, 2 * rw_w), a0=rwkv_a0[0].reshape(1, 2 * rw_w),
              w2=_block_diag2(rwkv_w2[0]).astype(BF16), a2=_block_diag2(rwkv_a2[0]).astype(BF16),
              g2=rwkv_g2[0].astype(BF16),
              kkp=rwkv_kk[0].reshape(1, rw_w), ka=rwkv_ka[0].reshape(1, rw_w), rk=rwkv_rk[0].reshape(1, rw_w),
              ones=ones_bd)
    npair = rw_w // LANES
    ptc, qtc, rhc, yhc, _, _ = _rwkv_features(prc, fp, _pick_tile(l, FEAT_ROWS))
    zero_state = jnp.zeros((b, 2, npair, LANES, LANES), F32)
    _, _, s_ctx = _rwkv_scan(zero_state, ptc, qtc, rhc, yhc, _pick_tile(l // CHUNK, SCAN_CHUNKS))
    pt, qt, rh, yh, bonus, gate = _rwkv_features(pr, fp, _pick_tile(s, FEAT_ROWS))
    yf, yb, _ = _rwkv_scan(s_ctx, pt, qt, rh, yh, _pick_tile(s // CHUNK, SCAN_CHUNKS))

    xl = _mix_out_mlp(x, att, yf, yb, bonus, gate, mod_l[0], norm_g[0],
                      rwkv_ln_w[0].reshape(1, rw_w), rwkv_ln_b[0].reshape(1, rw_w), ones_bd,
                      w_out.astype(BF16), mlp_w1[0].astype(BF16), mlp_w2[0].astype(BF16), _pick_tile(s, MLP_ROWS))

    return _conv_mlp(xl, mod_l[1], norm_g[1], conv_w_in[0].astype(BF16), conv_w[0], conv_w_out[0].astype(BF16),
                     mlp_w1[1].astype(BF16), mlp_w2[1].astype(BF16), _pick_tile(s, MLP_ROWS))
```

```python
import functools
import math

import jax
import jax.numpy as jnp
from jax import lax
from jax.experimental import pallas as pl
from jax.experimental.pallas import tpu as pltpu

F32 = jnp.float32
BF16 = jnp.bfloat16

HEAD_DIM = 64
GRID_W = 64
WINDOW = 128
ATT_BLOCK = 128
ATT_GROUP = 4
ROPE_BASE = 10000.0
NORM_EPS = 1e-6
GN_EPS = 64e-5
LANES = 128
SUBLANES = 8
CHUNK = 64
DECAY_SCALE = math.exp(-0.5)
PROJ_ROWS = 512
FEAT_ROWS = 256
FEAT_CHUNK_GROUP = 2
SCAN_CHUNKS = 8
ATT_SUBBLOCKS = 8
MLP_ROWS = 512
MLP_HIDDEN_BLOCK = 2048
VMEM_LIMIT = 56 * 1024 * 1024


def _cparams(*sem):
    return pltpu.CompilerParams(dimension_semantics=sem, vmem_limit_bytes=VMEM_LIMIT)


def _const_spec(shape):
    nd = len(shape)
    return pl.BlockSpec(shape, lambda *_: (0,) * nd, pipeline_mode=pl.Buffered(1))


def _dot(a, b):
    return jnp.dot(a.astype(BF16), b.astype(BF16), preferred_element_type=F32)


def _dot_nt(a, b):
    return lax.dot_general(a.astype(BF16), b.astype(BF16), (((1,), (1,)), ((), ())), preferred_element_type=F32)


def _dot_tn(a, b):
    return lax.dot_general(a.astype(BF16), b.astype(BF16), (((0,), (0,)), ((), ())), preferred_element_type=F32)


def _split2(x):
    hi = x.astype(BF16)
    lo = (x - hi.astype(F32)).astype(BF16)
    return hi, lo


def _split3(x):
    hi = x.astype(BF16)
    r1 = x - hi.astype(F32)
    mid = r1.astype(BF16)
    lo = (r1 - mid.astype(F32)).astype(BF16)
    return hi, mid, lo


def _head_sum(x, ones_bd):
    hi, lo = _split2(x)
    return (jnp.dot(hi, ones_bd, preferred_element_type=F32)
            + jnp.dot(lo, ones_bd, preferred_element_type=F32))


def _rms(u, g):
    return u * lax.rsqrt(jnp.mean(u * u, axis=-1, keepdims=True) + NORM_EPS) * g


def _sigmoid(z):
    return 1.0 / (1.0 + jnp.exp(-z))


def _mod_kernel(cv_ref, w_ref, b_ref, o_ref):
    cv = cv_ref[...]
    s = cv * _sigmoid(cv)
    o_ref[...] = _dot(s, w_ref[...]) + b_ref[...]


def _modulation(cv, mod_w, mod_b):
    depth, d, six_d = mod_w.shape
    rows = cv.shape[0]
    nj = six_d // d
    return pl.pallas_call(
        _mod_kernel,
        grid=(depth, nj),
        in_specs=[pl.BlockSpec((rows, d), lambda l, j: (0, 0)),
                  pl.BlockSpec((None, d, d), lambda l, j: (l, 0, j)),
                  pl.BlockSpec((None, 1, d), lambda l, j: (l, 0, j))],
        out_specs=pl.BlockSpec((None, rows, d), lambda l, j: (l, 0, j)),
        out_shape=jax.ShapeDtypeStruct((depth, rows, six_d), F32),
        compiler_params=_cparams("arbitrary", "arbitrary"),
    )(cv, mod_w, mod_b.reshape(depth, 1, six_d))


def _proj_kernel(x_ref, mod_ref, g_ref, w_ref, cos_ref, sin_ref, q_ref, k_ref, v_ref, pr_ref, *, att_w, kv_w):
    m = mod_ref[...]
    h = (_rms(x_ref[...], g_ref[...] * (1.0 + m[1:2])) + m[0:1]).astype(BF16)
    cos, sin = cos_ref[...], sin_ref[...]
    scale = HEAD_DIM ** -0.5
    p = jnp.dot(h, w_ref[...], preferred_element_type=F32)
    o = 0
    for j in range(att_w // LANES):
        u = p[:, o + j * LANES:o + (j + 1) * LANES]
        ur = p[:, o + att_w + j * LANES:o + att_w + (j + 1) * LANES]
        q_ref[:, j * LANES:(j + 1) * LANES] = ((u * cos + ur * sin) * scale).astype(q_ref.dtype)
    o += 2 * att_w
    for j in range(kv_w // LANES):
        u = p[:, o + j * LANES:o + (j + 1) * LANES]
        ur = p[:, o + kv_w + j * LANES:o + kv_w + (j + 1) * LANES]
        k_ref[:, j * LANES:(j + 1) * LANES] = (u * cos + ur * sin).astype(k_ref.dtype)
    o += 2 * kv_w
    v_ref[...] = p[:, o:o + kv_w].astype(v_ref.dtype)
    o += kv_w
    pr_ref[...] = p[:, o:]


def _proj_ctx_kernel(x_ref, mod_ref, g_ref, w_ref, k_ref, v_ref, pr_ref, *, kv_w):
    m = mod_ref[...]
    h = (_rms(x_ref[...], g_ref[...] * (1.0 + m[1:2])) + m[0:1]).astype(BF16)
    k_ref[...] = jnp.dot(h, w_ref[:, 0:kv_w], preferred_element_type=F32).astype(k_ref.dtype)
    v_ref[...] = jnp.dot(h, w_ref[:, kv_w:2 * kv_w], preferred_element_type=F32).astype(v_ref.dtype)
    pr_ref[...] = jnp.dot(h, w_ref[:, 2 * kv_w:], preferred_element_type=F32)


def _rot_cols(w):
    d, n = w.shape
    m = HEAD_DIM // 4
    w4 = w.reshape(d, n // (2 * m), 2, m)
    return jnp.stack([-w4[:, :, 1], w4[:, :, 0]], axis=2).reshape(d, n)


def _rope_tables(seq):
    m = HEAD_DIM // 4
    t = jnp.arange(seq)
    inv = ROPE_BASE ** (-jnp.arange(m, dtype=F32) / m)
    ang_r = (t // GRID_W).astype(F32)[:, None] * inv[None, :]
    ang_c = (t % GRID_W).astype(F32)[:, None] * inv[None, :]
    ang = jnp.concatenate([ang_r, ang_r, ang_c, ang_c], axis=-1)
    ang = jnp.tile(ang, (1, LANES // HEAD_DIM))
    return jnp.cos(ang), jnp.sin(ang)


def _project_latent(x, mod_l, g, w_all, cos, sin, att_w, kv_w, rw_w, tm):
    b, s, d = x.shape
    n_all = w_all.shape[1]
    kern = functools.partial(_proj_kernel, att_w=att_w, kv_w=kv_w)
    return pl.pallas_call(
        kern,
        grid=(b, s // tm),
        in_specs=[pl.BlockSpec((None, tm, d), lambda bi, i: (bi, i, 0)),
                  pl.BlockSpec((None, 6, d), lambda bi, i: (bi, 0, 0)),
                  _const_spec((1, d)),
                  _const_spec((d, n_all)),
                  pl.BlockSpec((tm, LANES), lambda bi, i: (i, 0)),
                  pl.BlockSpec((tm, LANES), lambda bi, i: (i, 0))],
        out_specs=[pl.BlockSpec((None, tm, att_w), lambda bi, i: (bi, i, 0)),
                   pl.BlockSpec((None, tm, kv_w), lambda bi, i: (bi, i, 0)),
                   pl.BlockSpec((None, tm, kv_w), lambda bi, i: (bi, i, 0)),
                   pl.BlockSpec((None, tm, rw_w), lambda bi, i: (bi, i, 0))],
        out_shape=[jax.ShapeDtypeStruct((b, s, att_w), BF16),
                   jax.ShapeDtypeStruct((b, s, kv_w), BF16),
                   jax.ShapeDtypeStruct((b, s, kv_w), BF16),
                   jax.ShapeDtypeStruct((b, s, rw_w), F32)],
        compiler_params=_cparams("parallel", "parallel"),
    )(x, mod_l, g, w_all, cos, sin)


def _project_ctx(ctx, mod_c, g, w_ctx, kv_w, rw_w):
    b, l, d = ctx.shape
    kern = functools.partial(_proj_ctx_kernel, kv_w=kv_w)
    return pl.pallas_call(
        kern,
        grid=(b,),
        in_specs=[pl.BlockSpec((None, l, d), lambda bi: (bi, 0, 0)),
                  _const_spec((6, d)),
                  _const_spec((1, d)),
                  _const_spec((d, w_ctx.shape[1]))],
        out_specs=[pl.BlockSpec((None, l, kv_w), lambda bi: (bi, 0, 0)),
                   pl.BlockSpec((None, l, kv_w), lambda bi: (bi, 0, 0)),
                   pl.BlockSpec((None, l, rw_w), lambda bi: (bi, 0, 0))],
        out_shape=[jax.ShapeDtypeStruct((b, l, kv_w), BF16),
                   jax.ShapeDtypeStruct((b, l, kv_w), BF16),
                   jax.ShapeDtypeStruct((b, l, rw_w), F32)],
        compiler_params=_cparams("parallel"),
    )(ctx, mod_c, g, w_ctx)


def _attn_kernel(sink_ref, q_ref, kp_ref, kc_ref, kn_ref, vp_ref, vc_ref, vn_ref, kx_ref, vx_ref, o_ref, *, nsub):
    i = pl.program_id(1)
    nb = pl.num_programs(1)
    blk = ATT_BLOCK
    kall = jnp.concatenate([kp_ref[...], kc_ref[...], kn_ref[...]], axis=0)
    vall = jnp.concatenate([vp_ref[...], vc_ref[...], vn_ref[...]], axis=0)
    vall = jnp.concatenate([vall, jnp.ones_like(vall)], axis=1)
    kctx = kx_ref[...]
    vctx = jnp.concatenate([vx_ref[...], jnp.ones_like(vx_ref[...])], axis=1)
    row = lax.broadcasted_iota(jnp.int32, (2 * blk, 3 * blk), 0) % blk
    col = lax.broadcasted_iota(jnp.int32, (2 * blk, 3 * blk), 1)
    band = jnp.abs(row + blk - col) <= WINDOW
    even = lax.broadcasted_iota(jnp.int32, (blk, LANES), 1) < HEAD_DIM
    zero = jnp.zeros((), q_ref.dtype)
    keys = [(a, j) for a in range(nsub) for j in range(q_ref.shape[1] // LANES)]
    qs, valid, kloc, vloc = [], [], [], []
    for a in range(nsub):
        va = band
        if a == 0:
            va = va & jnp.logical_not((i == 0) & (col < blk))
        if a == nsub - 1:
            va = va & jnp.logical_not((i == nb - 1) & (col >= 2 * blk))
        valid.append(va)
        kloc.append(kall[a * blk:(a + 3) * blk])
        vloc.append(vall[a * blk:(a + 3) * blk])
    for a, j in keys:
        qj = q_ref[a * blk:(a + 1) * blk, j * LANES:(j + 1) * LANES]
        qs.append(jnp.concatenate([jnp.where(even, qj, zero), jnp.where(even, zero, qj)], axis=0))
    def band_masked(s, va):
        return jnp.concatenate([jnp.where(va[:, :blk], s[:, :blk], -1e30), s[:, blk:2 * blk],
                                jnp.where(va[:, 2 * blk:], s[:, 2 * blk:], -1e30)], axis=1)

    s_loc = [band_masked(_dot_nt(qj, kloc[a]), valid[a]) for (a, j), qj in zip(keys, qs)]
    s_ctx = [_dot_nt(qj, kctx) for qj in qs]
    sinks = [sink_ref[j] for a, j in keys]
    mx = [jnp.maximum(jnp.maximum(jnp.max(sl, axis=-1, keepdims=True), jnp.max(sc, axis=-1, keepdims=True)), sk)
          for sl, sc, sk in zip(s_loc, s_ctx, sinks)]
    p_loc = [jnp.exp((sl - m).astype(BF16)) for sl, m in zip(s_loc, mx)]
    p_ctx = [jnp.exp((sc - m).astype(BF16)) for sc, m in zip(s_ctx, mx)]
    num = [jnp.dot(pa, vloc[a], preferred_element_type=F32) + jnp.dot(pc, vctx, preferred_element_type=F32)
           for (a, j), pa, pc in zip(keys, p_loc, p_ctx)]
    outs = [n[:, :LANES] / (n[:, LANES:] + jnp.exp(sk - m)) for n, sk, m in zip(num, sinks, mx)]
    for (a, j), o in zip(keys, outs):
        o_ref[a * blk:(a + 1) * blk, j * LANES:(j + 1) * LANES] = jnp.where(even, o[:blk], o[blk:]).astype(o_ref.dtype)


def _attention(q, k, v, kc, vc, sink_rows):
    b, s, att_w = q.shape
    kv_w = k.shape[2]
    l = kc.shape[1]
    blk = ATT_BLOCK
    nsub = _pick_tile(s // blk, ATT_SUBBLOCKS)
    nb = s // (nsub * blk)
    kern = functools.partial(_attn_kernel, nsub=nsub)
    kv_prev = pl.BlockSpec((None, blk, kv_w), lambda bi, i: (bi, jnp.maximum(i * nsub - 1, 0), 0))
    kv_cur = pl.BlockSpec((None, nsub * blk, kv_w), lambda bi, i: (bi, i, 0))
    kv_next = pl.BlockSpec((None, blk, kv_w), lambda bi, i: (bi, jnp.minimum((i + 1) * nsub, nb * nsub - 1), 0))
    kv_ctx = pl.BlockSpec((None, l, kv_w), lambda bi, i: (bi, 0, 0))
    return pl.pallas_call(
        kern,
        grid=(b, nb),
        in_specs=[_const_spec(sink_rows.shape),
                  pl.BlockSpec((None, nsub * blk, att_w), lambda bi, i: (bi, i, 0)),
                  kv_prev, kv_cur, kv_next, kv_prev, kv_cur, kv_next, kv_ctx, kv_ctx],
        out_specs=pl.BlockSpec((None, nsub * blk, att_w), lambda bi, i: (bi, i, 0)),
        out_shape=jax.ShapeDtypeStruct((b, s, att_w), BF16),
        compiler_params=_cparams("parallel", "parallel"),
    )(sink_rows, q, k, k, k, v, v, v, kc, vc)


def _stack_heads(x, even):
    return jnp.concatenate([jnp.where(even, x, 0.0), jnp.where(even, 0.0, x)], axis=0)


def _chunk_summaries(insts):
    c2 = 2 * CHUNK
    even = lax.broadcasted_iota(jnp.int32, (CHUNK, LANES), 1) < HEAD_DIM
    rt = lax.broadcasted_iota(jnp.int32, (CHUNK, LANES), 0)
    ct = lax.broadcasted_iota(jnp.int32, (CHUNK, LANES), 1) % HEAD_DIM
    masks = {True: (ct < rt, ct <= rt), False: (ct > rt, ct >= rt)}
    eye = lax.broadcasted_iota(jnp.int32, (LANES, LANES), 0) == lax.broadcasted_iota(jnp.int32, (LANES, LANES), 1)

    prep = []
    for rr, vv, aa, bb, kd, cs, lw, tot, forward in insts:
        g_inv = jnp.exp(-cs)
        g_rem = jnp.exp(tot - cs)
        a_t = aa * jnp.exp(cs - lw)
        r_t = rr * jnp.exp(cs)
        v_s = _stack_heads(vv, even).astype(BF16)
        bk_s = jnp.concatenate([_stack_heads(bb * g_rem, even), _stack_heads(kd * g_rem, even)], axis=0).astype(BF16)
        lhs = jnp.concatenate([a_t, r_t], axis=0).astype(BF16)
        rhs = jnp.concatenate([_stack_heads(bb * g_inv, even), _stack_heads(kd * g_inv, even)], axis=0).astype(BF16)
        prep.append((_stack_heads(a_t, even), _stack_heads(r_t, even), v_s, bk_s, lhs, rhs, jnp.exp(tot), masks[forward]))

    gs = [_dot_nt(p[4], p[5]) for p in prep]
    zero = jnp.zeros((), BF16)
    tri, lfs = [], []
    for g, p in zip(gs, prep):
        strict, incl = p[7]
        lfs.append(jnp.where(strict, g[:CHUNK, :c2], 0.0).astype(BF16))
        tri.append(tuple(jnp.concatenate([jnp.where(even, f, zero), jnp.where(even, zero, f)], axis=0) for f in (
            jnp.where(strict, g[:CHUNK, c2:], 0.0).astype(BF16),
            jnp.where(incl, g[CHUNK:, :c2], 0.0).astype(BF16), jnp.where(incl, g[CHUNK:, c2:], 0.0).astype(BF16))))
    own = (lax.broadcasted_iota(jnp.int32, (c2, LANES), 0) < CHUNK) == (
        lax.broadcasted_iota(jnp.int32, (c2, LANES), 1) < HEAD_DIM)
    xs = [p[0] + pltpu.roll(_dot(t[0], p[2]), HEAD_DIM, 1) for p, t in zip(prep, tri)]

    def stacked(f):
        f = f.astype(BF16)
        return jnp.concatenate([jnp.where(even, f, zero), jnp.where(even, zero, f)], axis=0)

    ts = [jnp.where((rt // 2 == ct // 2), lf.astype(F32), 0.0) + jnp.where(rt == ct, 1.0, 0.0) for lf in lfs]
    size = 2
    while size < CHUNK:
        couple = (rt // (2 * size) == ct // (2 * size)) & (rt // size != ct // size)
        os_ = [jnp.where(couple, lf, zero) for lf in lfs]
        ots = [_dot(o, stacked(t)) for o, t in zip(os_, ts)]
        ts = [t + _dot(t, stacked(ot)) for t, ot in zip(ts, ots)]
        size *= 2
    xs = [_dot(stacked(t), x) for t, x in zip(ts, xs)]
    rys = [p[1] + pltpu.roll(_dot(t[2], p[2]), HEAD_DIM, 1) + _dot(t[1], x)
           for p, t, x in zip(prep, tri, xs)]
    out = []
    for p, x, ry in zip(prep, xs, rys):
        v_s, bk_s, g_tot = p[2], p[3], p[6]
        x_a = jnp.where(own, x, 0.0).astype(BF16)
        x_u = pltpu.roll(jnp.where(own, 0.0, x), HEAD_DIM, 1).astype(BF16)
        pt = jnp.where(eye, g_tot, 0.0) + _dot_tn(bk_s[:c2], x_a)
        qt = _dot_tn(bk_s, jnp.concatenate([x_u, v_s], axis=0))
        rh = jnp.where(even, ry[:CHUNK], ry[CHUNK:])
        yh = pltpu.roll(jnp.where(even, ry[CHUNK:], ry[:CHUNK]), HEAD_DIM, 1)
        out.append((pt[:CHUNK] + pt[CHUNK:], qt[:CHUNK] + qt[CHUNK:], rh, yh))
    return out


def _feat_kernel(pr_ref, hp_ref, hn_ref, mu_ref, w0_ref, a0_ref, w2_ref, a2_ref, g2_ref, kkp_ref, ka_ref, rk_ref,
                 ones_ref, pt_ref, qt_ref, rh_ref, yh_ref, bonus_ref, gate_ref, *, rw_w):
    i = pl.program_id(1)
    last = pl.num_programs(1) - 1
    tt = pr_ref.shape[0]
    pr = pr_ref[...]
    ridx = lax.broadcasted_iota(jnp.int32, (tt, 1), 0)
    edge_prev = jnp.where(i == 0, 0.0, hp_ref[SUBLANES - 1:SUBLANES, :])
    edge_next = jnp.where(i == last, 0.0, hn_ref[0:1, :])
    prev = jnp.where(ridx == 0, edge_prev, pltpu.roll(pr, 1, 0))
    nxt = jnp.where(ridx == tt - 1, edge_next, pltpu.roll(pr, tt - 1, 0))
    x = pr + mu_ref[...] * (0.5 * (prev + nxt) - pr)

    r = x[:, 0:rw_w]
    k = x[:, rw_w:2 * rw_w]
    v = x[:, 2 * rw_w:3 * rw_w]
    o = 3 * rw_w
    wd = x[:, o:o + LANES]
    ad = x[:, o + LANES:o + 2 * LANES]
    gd = x[:, o + 2 * LANES:o + 3 * LANES]

    ones_bd = ones_ref[...]
    logw = -DECAY_SCALE * _sigmoid(_dot(jnp.tanh(wd), w2_ref[...]) + w0_ref[...])
    iclr = _sigmoid(_dot(ad, a2_ref[...]) + a0_ref[...])
    gate_ref[...] = _dot(_sigmoid(gd), g2_ref[...])

    kkv = k * kkp_ref[...]
    kk = kkv / jnp.maximum(jnp.sqrt(_head_sum(kkv * kkv, ones_bd)), 1e-12)
    ka = ka_ref[...]
    k_dir = [k * (1.0 + (iclr[:, d * rw_w:(d + 1) * rw_w] - 1.0) * ka) for d in range(2)]
    bonus_ref[...] = _head_sum(r * (k_dir[0] + k_dir[1]) * rk_ref[...], ones_bd) * v

    row = lax.broadcasted_iota(jnp.int32, (tt, tt), 0)
    col = lax.broadcasted_iota(jnp.int32, (tt, tt), 1)
    same = (row // CHUNK) == (col // CHUNK)
    tri = [jnp.where(same & (col <= row), 1.0, 0.0).astype(BF16),
           jnp.where(same & (col >= row), 1.0, 0.0).astype(BF16)]
    neg_kk = -kk
    lw_dir, cs_dir, b_dir = [], [], []
    for d in range(2):
        lw_d = logw[:, d * rw_w:(d + 1) * rw_w]
        lw_dir.append(lw_d)
        cs_dir.append(sum(jnp.dot(tri[d], p, preferred_element_type=F32) for p in _split3(lw_d)))
        b_dir.append(kk * iclr[:, d * rw_w:(d + 1) * rw_w])
    nchunks = tt // CHUNK
    group = _pick_tile(nchunks, FEAT_CHUNK_GROUP)
    for c0 in range(0, nchunks, group):
        keys, insts = [], []
        for c in range(c0, c0 + group):
            rows = slice(c * CHUNK, (c + 1) * CHUNK)
            for d in range(2):
                end = (c + 1) * CHUNK - 1 if d == 0 else c * CHUNK
                for p in range(rw_w // LANES):
                    ln = slice(p * LANES, (p + 1) * LANES)
                    keys.append((c, d, p, rows, ln))
                    insts.append((r[rows, ln], v[rows, ln], neg_kk[rows, ln], b_dir[d][rows, ln], k_dir[d][rows, ln],
                                  cs_dir[d][rows, ln], lw_dir[d][rows, ln], cs_dir[d][end:end + 1, ln], d == 0))
        for (c, d, p, rows, ln), (pt, qt, rh, yh) in zip(keys, _chunk_summaries(insts)):
            pt_ref[c, d, p] = pt
            qt_ref[c, d, p] = qt
            rh_ref[d, rows, ln] = rh.astype(rh_ref.dtype)
            yh_ref[d, rows, ln] = yh


def _rwkv_features(pr, fp, tt):
    b, t, w_all = pr.shape
    rw_w = fp["kkp"].shape[1]
    npair = rw_w // LANES
    nt = t // tt
    cpt = tt // CHUNK
    hb = tt // SUBLANES
    kern = functools.partial(_feat_kernel, rw_w=rw_w)
    names = ("mu", "w0", "a0", "w2", "a2", "g2", "kkp", "ka", "rk", "ones")
    return pl.pallas_call(
        kern,
        grid=(b, nt),
        in_specs=[pl.BlockSpec((None, tt, w_all), lambda bi, i: (bi, i, 0)),
                  pl.BlockSpec((None, SUBLANES, w_all), lambda bi, i: (bi, jnp.maximum(i * hb - 1, 0), 0)),
                  pl.BlockSpec((None, SUBLANES, w_all), lambda bi, i: (bi, jnp.minimum((i + 1) * hb, nt * hb - 1), 0))]
                 + [_const_spec(fp[n].shape) for n in names],
        out_specs=[pl.BlockSpec((None, cpt, 2, npair, CHUNK, LANES), lambda bi, i: (bi, i, 0, 0, 0, 0)),
                   pl.BlockSpec((None, cpt, 2, npair, CHUNK, LANES), lambda bi, i: (bi, i, 0, 0, 0, 0)),
                   pl.BlockSpec((None, 2, tt, rw_w), lambda bi, i: (bi, 0, i, 0)),
                   pl.BlockSpec((None, 2, tt, rw_w), lambda bi, i: (bi, 0, i, 0)),
                   pl.BlockSpec((None, tt, rw_w), lambda bi, i: (bi, i, 0)),
                   pl.BlockSpec((None, tt, rw_w), lambda bi, i: (bi, i, 0))],
        out_shape=[jax.ShapeDtypeStruct((b, t // CHUNK, 2, npair, CHUNK, LANES), F32),
                   jax.ShapeDtypeStruct((b, t // CHUNK, 2, npair, CHUNK, LANES), F32),
                   jax.ShapeDtypeStruct((b, 2, t, rw_w), BF16),
                   jax.ShapeDtypeStruct((b, 2, t, rw_w), F32),
                   jax.ShapeDtypeStruct((b, t, rw_w), F32),
                   jax.ShapeDtypeStruct((b, t, rw_w), F32)],
        compiler_params=_cparams("parallel", "parallel"),
    )(pr, pr, pr, *[fp[n] for n in names])


def _scan_kernel(s0_ref, ptf_ref, qtf_ref, ptb_ref, qtb_ref, rhf_ref, yhf_ref, rhb_ref, yhb_ref,
                 yf_ref, yb_ref, sfin_ref, st_ref, *, cps, npair):
    i = pl.program_id(1)

    @pl.when(i == 0)
    def _():
        st_ref[...] = s0_ref[...]

    even = lax.broadcasted_iota(jnp.int32, (CHUNK, LANES), 1) < HEAD_DIM
    zero = jnp.zeros((), BF16)
    dirs = ((ptf_ref, qtf_ref, rhf_ref, yhf_ref, yf_ref), (ptb_ref, qtb_ref, rhb_ref, yhb_ref, yb_ref))
    keys = [(d, p) for d in range(2) for p in range(npair)]
    st = [st_ref[d, p] for d, p in keys]
    for step in range(cps):
        hl = []
        for s in st:
            hi = s.astype(BF16)
            lo = (s - hi.astype(F32)).astype(BF16)
            hl.append(jnp.concatenate([jnp.concatenate([jnp.where(even, hi, zero), jnp.where(even, zero, hi)], axis=0),
                                       jnp.concatenate([jnp.where(even, lo, zero), jnp.where(even, zero, lo)], axis=0)],
                                      axis=1))
        new = []
        for (d, p), s2 in zip(keys, hl):
            pt_ref, qt_ref, rh_ref, yh_ref, y_ref = dirs[d]
            c = step if d == 0 else cps - 1 - step
            rows = slice(c * CHUNK, (c + 1) * CHUNK)
            ln = slice(p * LANES, (p + 1) * LANES)
            y_ref[rows, ln] = jnp.dot(rh_ref[rows, ln], s2[:, :LANES], preferred_element_type=F32) + yh_ref[rows, ln]
            pt_hi, pt_lo = _split2(pt_ref[c, p])
            n2 = jnp.dot(pt_hi, s2, preferred_element_type=F32)
            new.append(n2[:, :LANES] + n2[:, LANES:] + jnp.dot(pt_lo, s2[:, :LANES], preferred_element_type=F32)
                       + qt_ref[c, p])
        st = new
    for (d, p), s in zip(keys, st):
        st_ref[d, p] = s

    @pl.when(i == pl.num_programs(1) - 1)
    def _():
        sfin_ref[...] = st_ref[...]


def _rwkv_scan(s0, pt, qt, rh, yh, cps):
    b, nc, _, npair, _, _ = pt.shape
    t, rw_w = rh.shape[2], rh.shape[3]
    ns = nc // cps
    ts = cps * CHUNK
    kern = functools.partial(_scan_kernel, cps=cps, npair=npair)
    mat_f = pl.BlockSpec((None, cps, None, npair, CHUNK, LANES), lambda bi, i: (bi, i, 0, 0, 0, 0))
    mat_b = pl.BlockSpec((None, cps, None, npair, CHUNK, LANES), lambda bi, i: (bi, ns - 1 - i, 1, 0, 0, 0))
    tok_f = pl.BlockSpec((None, None, ts, rw_w), lambda bi, i: (bi, 0, i, 0))
    tok_b = pl.BlockSpec((None, None, ts, rw_w), lambda bi, i: (bi, 1, ns - 1 - i, 0))
    state = pl.BlockSpec((None, 2, npair, CHUNK, LANES), lambda bi, i: (bi, 0, 0, 0, 0))
    return pl.pallas_call(
        kern,
        grid=(b, ns),
        in_specs=[state, mat_f, mat_f, mat_b, mat_b, tok_f, tok_f, tok_b, tok_b],
        out_specs=[pl.BlockSpec((None, ts, rw_w), lambda bi, i: (bi, i, 0)),
                   pl.BlockSpec((None, ts, rw_w), lambda bi, i: (bi, ns - 1 - i, 0)),
                   state],
        out_shape=[jax.ShapeDtypeStruct((b, t, rw_w), F32),
                   jax.ShapeDtypeStruct((b, t, rw_w), F32),
                   jax.ShapeDtypeStruct((b, 2, npair, CHUNK, LANES), F32)],
        scratch_shapes=[pltpu.VMEM((2, npair, CHUNK, LANES), F32)],
        compiler_params=_cparams("parallel", "arbitrary"),
    )(s0, pt, qt, pt, qt, rh, yh, rh, yh)


def _residual_mlp(xl, yl, m, ng, w1_ref, w2_ref):
    x2 = xl + _rms(yl, m[2:3] * ng[1:2])
    hm = (_rms(x2, ng[2:3] * (1.0 + m[4:5])) + m[3:4]).astype(BF16)
    out = None
    for j in range(w1_ref.shape[1] // MLP_HIDDEN_BLOCK):
        cols = slice(j * MLP_HIDDEN_BLOCK, (j + 1) * MLP_HIDDEN_BLOCK)
        hid = jnp.maximum(jnp.dot(hm, w1_ref[:, cols], preferred_element_type=F32), 0.0)
        part = jnp.dot((hid * hid).astype(BF16), w2_ref[cols, :], preferred_element_type=F32)
        out = part if out is None else out + part
    return x2 + _rms(out, m[5:6] * ng[3:4])


def _mix_out_kernel(x_ref, att_ref, yf_ref, yb_ref, bonus_ref, gate_ref, mod_ref, ng_ref, lnw_ref, lnb_ref, ones_ref,
                    wo_ref, w1_ref, w2_ref, o_ref, *, att_w):
    ones_bd = ones_ref[...]
    y = yf_ref[...] + yb_ref[...]
    mean = _head_sum(y, ones_bd) * (1.0 / HEAD_DIM)
    yc = y - mean
    var = _head_sum(yc * yc, ones_bd) * (1.0 / HEAD_DIM)
    yn = yc * lax.rsqrt(var + GN_EPS) * lnw_ref[...] + lnb_ref[...]
    rw = (yn + bonus_ref[...]) * gate_ref[...]
    yl = (jnp.dot(att_ref[...], wo_ref[0:att_w, :], preferred_element_type=F32)
          + jnp.dot(rw.astype(BF16), wo_ref[att_w:, :], preferred_element_type=F32))
    o_ref[...] = _residual_mlp(x_ref[...], yl, mod_ref[...], ng_ref[...], w1_ref, w2_ref)


def _mix_out_mlp(x, att, yf, yb, bonus, gate, mod_l, ng, lnw, lnb, ones_bd, wo, w1, w2, tm):
    b, s, d = x.shape
    att_w = att.shape[2]
    rw_w = yf.shape[2]
    kern = functools.partial(_mix_out_kernel, att_w=att_w)
    tok = lambda w: pl.BlockSpec((None, tm, w), lambda bi, i: (bi, i, 0))
    return pl.pallas_call(
        kern,
        grid=(b, s // tm),
        in_specs=[tok(d), tok(att_w), tok(rw_w), tok(rw_w), tok(rw_w), tok(rw_w),
                  pl.BlockSpec((None, 6, d), lambda bi, i: (bi, 0, 0)),
                  _const_spec(ng.shape), _const_spec(lnw.shape), _const_spec(lnb.shape), _const_spec(ones_bd.shape),
                  _const_spec(wo.shape), _const_spec(w1.shape), _const_spec(w2.shape)],
        out_specs=tok(d),
        out_shape=jax.ShapeDtypeStruct((b, s, d), F32),
        compiler_params=_cparams("parallel", "parallel"),
    )(x, att, yf, yb, bonus, gate, mod_l, ng, lnw, lnb, ones_bd, wo, w1, w2)


def _conv_kernel(x_ref, xp_ref, xn_ref, mod_ref, ng_ref, wi_ref, cw_ref, wo_ref, w1_ref, w2_ref, o_ref):
    i = pl.program_id(1)
    last = pl.num_programs(1) - 1
    d = x_ref.shape[1]
    tm = x_ref.shape[0]
    m = mod_ref[...]
    ng = ng_ref[...]

    def modnorm(u):
        return (_rms(u, ng[0:1] * (1.0 + m[1:2])) + m[0:1]).astype(BF16)

    x = x_ref[...]
    rows = jnp.concatenate([x, xp_ref[...], xn_ref[...]], axis=0)
    proj = jnp.dot(modnorm(rows), wi_ref[...], preferred_element_type=F32)
    z_all = proj[:, d:2 * d] * proj[:, 2 * d:]
    z = z_all[:tm]
    zp = jnp.where(i == 0, 0.0, z_all[tm + SUBLANES - 1:tm + SUBLANES, :])
    zn = jnp.where(i == last, 0.0, z_all[tm + SUBLANES:tm + SUBLANES + 1, :])
    ridx = lax.broadcasted_iota(jnp.int32, (tm, 1), 0)
    prev = jnp.where(ridx == 0, zp, pltpu.roll(z, 1, 0))
    nxt = jnp.where(ridx == tm - 1, zn, pltpu.roll(z, tm - 1, 0))
    cw = cw_ref[...]
    y = proj[:tm, :d] * (prev * cw[0:1] + z * cw[1:2] + nxt * cw[2:3])
    yl = jnp.dot(y.astype(BF16), wo_ref[...], preferred_element_type=F32)
    o_ref[...] = _residual_mlp(x, yl, m, ng, w1_ref, w2_ref)


def _conv_mlp(x, mod_l, ng, wi, cw, wo, w1, w2, tm):
    b, s, d = x.shape
    hb = tm // SUBLANES
    nt = s // tm
    return pl.pallas_call(
        _conv_kernel,
        grid=(b, nt),
        in_specs=[pl.BlockSpec((None, tm, d), lambda bi, i: (bi, i, 0)),
                  pl.BlockSpec((None, SUBLANES, d), lambda bi, i: (bi, jnp.maximum(i * hb - 1, 0), 0)),
                  pl.BlockSpec((None, SUBLANES, d), lambda bi, i: (bi, jnp.minimum((i + 1) * hb, nt * hb - 1), 0)),
                  pl.BlockSpec((None, 6, d), lambda bi, i: (bi, 0, 0)),
                  _const_spec(ng.shape), _const_spec(wi.shape), _const_spec(cw.shape), _const_spec(wo.shape),
                  _const_spec(w1.shape), _const_spec(w2.shape)],
        out_specs=pl.BlockSpec((None, tm, d), lambda bi, i: (bi, i, 0)),
        out_shape=jax.ShapeDtypeStruct((b, s, d), F32),
        compiler_params=_cparams("parallel", "parallel"),
    )(x, x, x, mod_l, ng, wi, cw, wo, w1, w2)


def _block_diag2(m):
    z = jnp.zeros_like(m[0])
    return jnp.concatenate([jnp.concatenate([m[0], z], axis=1), jnp.concatenate([z, m[1]], axis=1)], axis=0)


def _pick_tile(n, want):
    t = min(n, want)
    while n % t:
        t //= 2
    return t


def kernel(x, c, ctx, c_ctx, mod_w, mod_b, norm_g, mlp_w1, mlp_w2, ab_w_in, ab_w_out, att_sink, rwkv_mu, rwkv_w0,
           rwkv_w2, rwkv_a0, rwkv_a2, rwkv_g2, rwkv_kk, rwkv_ka, rwkv_rk, rwkv_ln_w, rwkv_ln_b, conv_w_in, conv_w,
           conv_w_out):
    b, s, d = x.shape
    l = ctx.shape[1]
    depth = mod_w.shape[0]
    assert depth == 2, "layer schedule below is written for one attention/RWKV layer followed by one conv layer"
    att_w = att_sink.shape[1] * HEAD_DIM
    kv_w = att_w // ATT_GROUP
    rw_w = rwkv_kk.shape[1]
    rw_in = rwkv_mu.shape[1]
    assert s % ATT_BLOCK == 0 and s % CHUNK == 0 and l % CHUNK == 0 and kv_w % LANES == 0 and rw_w % LANES == 0

    rows = -(-(b + 1) // SUBLANES) * SUBLANES
    cv = jnp.concatenate([c, c_ctx[None, :], jnp.zeros((rows - b - 1, d), F32)], axis=0)
    mod = _modulation(cv, mod_w, mod_b)
    mod_l = [mod[i, :b].reshape(b, 6, d) for i in range(depth)]
    mod_c0 = mod[0, b].reshape(6, d)

    w_in = ab_w_in[0]
    wq, wk, wv, wr = (w_in[:, :att_w], w_in[:, att_w:att_w + kv_w], w_in[:, att_w + kv_w:att_w + 2 * kv_w],
                      w_in[:, att_w + 2 * kv_w:])
    n_heads = att_w // HEAD_DIM
    perm = jnp.arange(n_heads).reshape(n_heads // ATT_GROUP, ATT_GROUP).T.reshape(-1)
    wq = wq.reshape(d, n_heads, HEAD_DIM)[:, perm].reshape(d, att_w)
    w_lat = jnp.concatenate([wq, _rot_cols(wq), wk, _rot_cols(wk), wv, wr], axis=1).astype(BF16)
    w_ctx = jnp.concatenate([wk, wv, wr], axis=1).astype(BF16)
    cos, sin = _rope_tables(s)
    g0 = norm_g[0, 0].reshape(1, d)
    q, k, v, pr = _project_latent(x, mod_l[0], g0, w_lat, cos, sin, att_w, kv_w, rw_in, _pick_tile(s, PROJ_ROWS))
    kc, vc, prc = _project_ctx(ctx, mod_c0, g0, w_ctx, kv_w, rw_in)
    sink_rows = jnp.repeat(att_sink[0][perm].reshape(att_w // LANES, LANES // HEAD_DIM), ATT_BLOCK, axis=1)[..., None]
    att = _attention(q, k, v, kc, vc, sink_rows)
    w_out = ab_w_out[0]
    w_out = jnp.concatenate([w_out[:att_w].reshape(n_heads, HEAD_DIM, d)[perm].reshape(att_w, d), w_out[att_w:]], axis=0)

    head_id = jnp.arange(rw_w) // HEAD_DIM
    ones_bd = (head_id[:, None] == head_id[None, :]).astype(BF16)
    fp = dict(mu=rwkv_mu[0].reshape(1, rw_in),
              w0=rwkv_w0[0].reshape(1, 2 * rw_w), a0=rwkv_a0[0].reshape(1, 2 * rw_w),
              w2=_block_diag2(rwkv_w2[0]).astype(BF16), a2=_block_diag2(rwkv_a2[0]).astype(BF16),
              g2=rwkv_g2[0].astype(BF16),
              kkp=rwkv_kk[0].reshape(1, rw_w), ka=rwkv_ka[0].reshape(1, rw_w), rk=rwkv_rk[0].reshape(1, rw_w),
              ones=ones_bd)
    npair = rw_w // LANES
    ptc, qtc, rhc, yhc, _, _ = _rwkv_features(prc, fp, _pick_tile(l, FEAT_ROWS))
    zero_state = jnp.zeros((b, 2, npair, CHUNK, LANES), F32)
    _, _, s_ctx = _rwkv_scan(zero_state, ptc, qtc, rhc, yhc, _pick_tile(l // CHUNK, SCAN_CHUNKS))
    pt, qt, rh, yh, bonus, gate = _rwkv_features(pr, fp, _pick_tile(s, FEAT_ROWS))
    yf, yb, _ = _rwkv_scan(s_ctx, pt, qt, rh, yh, _pick_tile(s // CHUNK, SCAN_CHUNKS))

    xl = _mix_out_mlp(x, att, yf, yb, bonus, gate, mod_l[0], norm_g[0],
                      rwkv_ln_w[0].reshape(1, rw_w), rwkv_ln_b[0].reshape(1, rw_w), ones_bd,
                      w_out.astype(BF16), mlp_w1[0].astype(BF16), mlp_w2[0].astype(BF16), _pick_tile(s, MLP_ROWS))

    return _conv_mlp(xl, mod_l[1], norm_g[1], conv_w_in[0].astype(BF16), conv_w[0], conv_w_out[0].astype(BF16),
                     mlp_w1[1].astype(BF16), mlp_w2[1].astype(BF16), _pick_tile(s, MLP_ROWS))
```

```python
import functools
import math

import jax
import jax.numpy as jnp
from jax import lax
from jax.experimental import pallas as pl
from jax.experimental.pallas import tpu as pltpu

F32 = jnp.float32
BF16 = jnp.bfloat16

HEAD_DIM = 64
GRID_W = 64
WINDOW = 128
ATT_BLOCK = 128
ATT_GROUP = 4
ROPE_BASE = 10000.0
NORM_EPS = 1e-6
GN_EPS = 64e-5
LANES = 128
SUBLANES = 8
CHUNK = 64
DECAY_SCALE = math.exp(-0.5)
PROJ_ROWS = 512
FEAT_ROWS = 256
FEAT_CHUNK_GROUP = 2
SCAN_CHUNKS = 8
ATT_SUBBLOCKS = 8
MLP_ROWS = 512
MLP_HIDDEN_BLOCK = 2048
VMEM_LIMIT = 56 * 1024 * 1024


def _cparams(*sem):
    return pltpu.CompilerParams(dimension_semantics=sem, vmem_limit_bytes=VMEM_LIMIT)


def _const_spec(shape):
    nd = len(shape)
    return pl.BlockSpec(shape, lambda *_: (0,) * nd, pipeline_mode=pl.Buffered(1))


def _dot(a, b):
    return jnp.dot(a.astype(BF16), b.astype(BF16), preferred_element_type=F32)


def _dot_nt(a, b):
    return lax.dot_general(a.astype(BF16), b.astype(BF16), (((1,), (1,)), ((), ())), preferred_element_type=F32)


def _dot_tn(a, b):
    return lax.dot_general(a.astype(BF16), b.astype(BF16), (((0,), (0,)), ((), ())), preferred_element_type=F32)


def _split2(x):
    hi = x.astype(BF16)
    lo = (x - hi.astype(F32)).astype(BF16)
    return hi, lo


def _split3(x):
    hi = x.astype(BF16)
    r1 = x - hi.astype(F32)
    mid = r1.astype(BF16)
    lo = (r1 - mid.astype(F32)).astype(BF16)
    return hi, mid, lo


def _head_sum(x, ones_bd):
    hi, lo = _split2(x)
    return (jnp.dot(hi, ones_bd, preferred_element_type=F32)
            + jnp.dot(lo, ones_bd, preferred_element_type=F32))


def _rms(u, g):
    return u * lax.rsqrt(jnp.mean(u * u, axis=-1, keepdims=True) + NORM_EPS) * g


def _sigmoid(z):
    return 1.0 / (1.0 + jnp.exp(-z))


def _mod_kernel(cv_ref, w_ref, b_ref, o_ref):
    cv = cv_ref[...]
    s = cv * _sigmoid(cv)
    o_ref[...] = _dot(s, w_ref[...]) + b_ref[...]


def _modulation(cv, mod_w, mod_b):
    depth, d, six_d = mod_w.shape
    rows = cv.shape[0]
    nj = six_d // d
    return pl.pallas_call(
        _mod_kernel,
        grid=(depth, nj),
        in_specs=[pl.BlockSpec((rows, d), lambda l, j: (0, 0)),
                  pl.BlockSpec((None, d, d), lambda l, j: (l, 0, j)),
                  pl.BlockSpec((None, 1, d), lambda l, j: (l, 0, j))],
        out_specs=pl.BlockSpec((None, rows, d), lambda l, j: (l, 0, j)),
        out_shape=jax.ShapeDtypeStruct((depth, rows, six_d), F32),
        compiler_params=_cparams("arbitrary", "arbitrary"),
    )(cv, mod_w, mod_b.reshape(depth, 1, six_d))


def _proj_kernel(x_ref, mod_ref, g_ref, w_ref, cos_ref, sin_ref, q_ref, k_ref, v_ref, pr_ref, *, att_w, kv_w):
    m = mod_ref[...]
    h = (_rms(x_ref[...], g_ref[...] * (1.0 + m[1:2])) + m[0:1]).astype(BF16)
    cos, sin = cos_ref[...], sin_ref[...]
    scale = HEAD_DIM ** -0.5
    p = jnp.dot(h, w_ref[...], preferred_element_type=F32)
    o = 0
    for j in range(att_w // LANES):
        u = p[:, o + j * LANES:o + (j + 1) * LANES]
        ur = p[:, o + att_w + j * LANES:o + att_w + (j + 1) * LANES]
        q_ref[:, j * LANES:(j + 1) * LANES] = ((u * cos + ur * sin) * scale).astype(q_ref.dtype)
    o += 2 * att_w
    for j in range(kv_w // LANES):
        u = p[:, o + j * LANES:o + (j + 1) * LANES]
        ur = p[:, o + kv_w + j * LANES:o + kv_w + (j + 1) * LANES]
        k_ref[:, j * LANES:(j + 1) * LANES] = (u * cos + ur * sin).astype(k_ref.dtype)
    o += 2 * kv_w
    v_ref[...] = p[:, o:o + kv_w].astype(v_ref.dtype)
    o += kv_w
    pr_ref[...] = p[:, o:]


def _proj_ctx_kernel(x_ref, mod_ref, g_ref, w_ref, k_ref, v_ref, pr_ref, *, kv_w):
    m = mod_ref[...]
    h = (_rms(x_ref[...], g_ref[...] * (1.0 + m[1:2])) + m[0:1]).astype(BF16)
    k_ref[...] = jnp.dot(h, w_ref[:, 0:kv_w], preferred_element_type=F32).astype(k_ref.dtype)
    v_ref[...] = jnp.dot(h, w_ref[:, kv_w:2 * kv_w], preferred_element_type=F32).astype(v_ref.dtype)
    pr_ref[...] = jnp.dot(h, w_ref[:, 2 * kv_w:], preferred_element_type=F32)


def _rot_cols(w):
    d, n = w.shape
    m = HEAD_DIM // 4
    w4 = w.reshape(d, n // (2 * m), 2, m)
    return jnp.stack([-w4[:, :, 1], w4[:, :, 0]], axis=2).reshape(d, n)


def _rope_tables(seq):
    m = HEAD_DIM // 4
    t = jnp.arange(seq)
    inv = ROPE_BASE ** (-jnp.arange(m, dtype=F32) / m)
    ang_r = (t // GRID_W).astype(F32)[:, None] * inv[None, :]
    ang_c = (t % GRID_W).astype(F32)[:, None] * inv[None, :]
    ang = jnp.concatenate([ang_r, ang_r, ang_c, ang_c], axis=-1)
    ang = jnp.tile(ang, (1, LANES // HEAD_DIM))
    return jnp.cos(ang), jnp.sin(ang)


def _project_latent(x, mod_l, g, w_all, cos, sin, att_w, kv_w, rw_w, tm):
    b, s, d = x.shape
    n_all = w_all.shape[1]
    kern = functools.partial(_proj_kernel, att_w=att_w, kv_w=kv_w)
    return pl.pallas_call(
        kern,
        grid=(b, s // tm),
        in_specs=[pl.BlockSpec((None, tm, d), lambda bi, i: (bi, i, 0)),
                  pl.BlockSpec((None, 6, d), lambda bi, i: (bi, 0, 0)),
                  _const_spec((1, d)),
                  _const_spec((d, n_all)),
                  pl.BlockSpec((tm, LANES), lambda bi, i: (i, 0)),
                  pl.BlockSpec((tm, LANES), lambda bi, i: (i, 0))],
        out_specs=[pl.BlockSpec((None, tm, att_w), lambda bi, i: (bi, i, 0)),
                   pl.BlockSpec((None, tm, kv_w), lambda bi, i: (bi, i, 0)),
                   pl.BlockSpec((None, tm, kv_w), lambda bi, i: (bi, i, 0)),
                   pl.BlockSpec((None, tm, rw_w), lambda bi, i: (bi, i, 0))],
        out_shape=[jax.ShapeDtypeStruct((b, s, att_w), BF16),
                   jax.ShapeDtypeStruct((b, s, kv_w), BF16),
                   jax.ShapeDtypeStruct((b, s, kv_w), BF16),
                   jax.ShapeDtypeStruct((b, s, rw_w), F32)],
        compiler_params=_cparams("parallel", "parallel"),
    )(x, mod_l, g, w_all, cos, sin)


def _project_ctx(ctx, mod_c, g, w_ctx, kv_w, rw_w):
    b, l, d = ctx.shape
    kern = functools.partial(_proj_ctx_kernel, kv_w=kv_w)
    return pl.pallas_call(
        kern,
        grid=(b,),
        in_specs=[pl.BlockSpec((None, l, d), lambda bi: (bi, 0, 0)),
                  _const_spec((6, d)),
                  _const_spec((1, d)),
                  _const_spec((d, w_ctx.shape[1]))],
        out_specs=[pl.BlockSpec((None, l, kv_w), lambda bi: (bi, 0, 0)),
                   pl.BlockSpec((None, l, kv_w), lambda bi: (bi, 0, 0)),
                   pl.BlockSpec((None, l, rw_w), lambda bi: (bi, 0, 0))],
        out_shape=[jax.ShapeDtypeStruct((b, l, kv_w), BF16),
                   jax.ShapeDtypeStruct((b, l, kv_w), BF16),
                   jax.ShapeDtypeStruct((b, l, rw_w), F32)],
        compiler_params=_cparams("parallel"),
    )(ctx, mod_c, g, w_ctx)


def _attn_kernel(sink_ref, q_ref, kp_ref, kc_ref, kn_ref, vp_ref, vc_ref, vn_ref, kx_ref, vx_ref, o_ref, *, nsub):
    i = pl.program_id(1)
    nb = pl.num_programs(1)
    blk = ATT_BLOCK
    kall = jnp.concatenate([kp_ref[...], kc_ref[...], kn_ref[...]], axis=0)
    vall = jnp.concatenate([vp_ref[...], vc_ref[...], vn_ref[...]], axis=0)
    vall = jnp.concatenate([vall, jnp.ones_like(vall)], axis=1)
    kctx = kx_ref[...]
    vctx = jnp.concatenate([vx_ref[...], jnp.ones_like(vx_ref[...])], axis=1)
    row = lax.broadcasted_iota(jnp.int32, (2 * blk, 3 * blk), 0) % blk
    col = lax.broadcasted_iota(jnp.int32, (2 * blk, 3 * blk), 1)
    band = jnp.abs(row + blk - col) <= WINDOW
    even = lax.broadcasted_iota(jnp.int32, (blk, LANES), 1) < HEAD_DIM
    zero = jnp.zeros((), q_ref.dtype)
    tiles = range(q_ref.shape[1] // LANES)
    sinks = [sink_ref[j] for j in tiles]

    def band_masked(s, va):
        return jnp.concatenate([jnp.where(va[:, :blk], s[:, :blk], -1e30), s[:, blk:2 * blk],
                                jnp.where(va[:, 2 * blk:], s[:, 2 * blk:], -1e30)], axis=1)

    def scores(a):
        va = band
        if a == 0:
            va = va & jnp.logical_not((i == 0) & (col < blk))
        if a == nsub - 1:
            va = va & jnp.logical_not((i == nb - 1) & (col >= 2 * blk))
        kloc = kall[a * blk:(a + 3) * blk]
        qs = []
        for j in tiles:
            qj = q_ref[a * blk:(a + 1) * blk, j * LANES:(j + 1) * LANES]
            qs.append(jnp.concatenate([jnp.where(even, qj, zero), jnp.where(even, zero, qj)], axis=0))
        return [band_masked(_dot_nt(qj, kloc), va) for qj in qs], [_dot_nt(qj, kctx) for qj in qs]

    def row_max(sl, sc, sk):
        t = sk
        for s in (sl, sc):
            for c0 in range(0, s.shape[1], LANES):
                t = jnp.maximum(t, s[:, c0:c0 + LANES])
        return jnp.max(t, axis=-1, keepdims=True)

    def finish(a, s_loc, s_ctx):
        vloc = vall[a * blk:(a + 3) * blk]
        mx = [row_max(sl, sc, sk) for sl, sc, sk in zip(s_loc, s_ctx, sinks)]
        p_loc = [jnp.exp((sl - m).astype(BF16)) for sl, m in zip(s_loc, mx)]
        p_ctx = [jnp.exp((sc - m).astype(BF16)) for sc, m in zip(s_ctx, mx)]
        num = [jnp.dot(pa, vloc, preferred_element_type=F32) + jnp.dot(pc, vctx, preferred_element_type=F32)
               for pa, pc in zip(p_loc, p_ctx)]
        for j, n, sk, m in zip(tiles, num, sinks, mx):
            o = n[:, :LANES] / (n[:, LANES:] + jnp.exp(sk - m))
            o_ref[a * blk:(a + 1) * blk, j * LANES:(j + 1) * LANES] = jnp.where(even, o[:blk], o[blk:]).astype(o_ref.dtype)

    pending = scores(0)
    for a in range(nsub):
        ahead = scores(a + 1) if a + 1 < nsub else None
        finish(a, *pending)
        pending = ahead


def _attention(q, k, v, kc, vc, sink_rows):
    b, s, att_w = q.shape
    kv_w = k.shape[2]
    l = kc.shape[1]
    blk = ATT_BLOCK
    nsub = _pick_tile(s // blk, ATT_SUBBLOCKS)
    nb = s // (nsub * blk)
    kern = functools.partial(_attn_kernel, nsub=nsub)
    kv_prev = pl.BlockSpec((None, blk, kv_w), lambda bi, i: (bi, jnp.maximum(i * nsub - 1, 0), 0))
    kv_cur = pl.BlockSpec((None, nsub * blk, kv_w), lambda bi, i: (bi, i, 0))
    kv_next = pl.BlockSpec((None, blk, kv_w), lambda bi, i: (bi, jnp.minimum((i + 1) * nsub, nb * nsub - 1), 0))
    kv_ctx = pl.BlockSpec((None, l, kv_w), lambda bi, i: (bi, 0, 0))
    return pl.pallas_call(
        kern,
        grid=(b, nb),
        in_specs=[_const_spec(sink_rows.shape),
                  pl.BlockSpec((None, nsub * blk, att_w), lambda bi, i: (bi, i, 0)),
                  kv_prev, kv_cur, kv_next, kv_prev, kv_cur, kv_next, kv_ctx, kv_ctx],
        out_specs=pl.BlockSpec((None, nsub * blk, att_w), lambda bi, i: (bi, i, 0)),
        out_shape=jax.ShapeDtypeStruct((b, s, att_w), BF16),
        compiler_params=_cparams("parallel", "parallel"),
    )(sink_rows, q, k, k, k, v, v, v, kc, vc)


def _stack_heads(x, even):
    return jnp.concatenate([jnp.where(even, x, 0.0), jnp.where(even, 0.0, x)], axis=0)


def _chunk_summaries(insts):
    c2 = 2 * CHUNK
    even = lax.broadcasted_iota(jnp.int32, (CHUNK, LANES), 1) < HEAD_DIM
    rt = lax.broadcasted_iota(jnp.int32, (CHUNK, LANES), 0)
    ct = lax.broadcasted_iota(jnp.int32, (CHUNK, LANES), 1) % HEAD_DIM
    masks = {True: (ct < rt, ct <= rt), False: (ct > rt, ct >= rt)}
    eye = lax.broadcasted_iota(jnp.int32, (LANES, LANES), 0) == lax.broadcasted_iota(jnp.int32, (LANES, LANES), 1)

    zero = jnp.zeros((), BF16)

    def stacked(f):
        f = f.astype(BF16)
        return jnp.concatenate([jnp.where(even, f, zero), jnp.where(even, zero, f)], axis=0)

    prep = []
    for rr, vv, aa, bb, kd, cs, lw, tot, forward in insts:
        g_inv = jnp.exp(-cs)
        g_rem = jnp.exp(tot - cs)
        a_t = aa * jnp.exp(cs - lw)
        r_t = rr * jnp.exp(cs)
        v_s = stacked(vv)
        bk_s = jnp.concatenate([stacked(bb * g_rem), stacked(kd * g_rem)], axis=0)
        lhs = jnp.concatenate([a_t, r_t], axis=0).astype(BF16)
        rhs = jnp.concatenate([stacked(bb * g_inv), stacked(kd * g_inv)], axis=0)
        prep.append((_stack_heads(a_t, even), _stack_heads(r_t, even), v_s, bk_s, lhs, rhs, jnp.exp(tot), masks[forward]))

    gs = [_dot_nt(p[4], p[5]) for p in prep]
    tri, lfs = [], []
    for g, p in zip(gs, prep):
        strict, incl = p[7]
        lfs.append(jnp.where(strict, g[:CHUNK, :c2], 0.0).astype(BF16))
        tri.append((stacked(jnp.where(strict, g[:CHUNK, c2:], 0.0)), stacked(jnp.where(incl, g[CHUNK:, :c2], 0.0)),
                    stacked(jnp.where(incl, g[CHUNK:, c2:], 0.0))))
    own = (lax.broadcasted_iota(jnp.int32, (c2, LANES), 0) < CHUNK) == (
        lax.broadcasted_iota(jnp.int32, (c2, LANES), 1) < HEAD_DIM)
    xs = [p[0] + pltpu.roll(_dot(t[0], p[2]), HEAD_DIM, 1) for p, t in zip(prep, tri)]

    ts = [jnp.where((rt // 2 == ct // 2), lf.astype(F32), 0.0) + jnp.where(rt == ct, 1.0, 0.0) for lf in lfs]
    size = 2
    while size < CHUNK:
        couple = (rt // (2 * size) == ct // (2 * size)) & (rt // size != ct // size)
        os_ = [jnp.where(couple, lf, zero) for lf in lfs]
        ots = [_dot(o, stacked(t)) for o, t in zip(os_, ts)]
        ts = [t + _dot(t, stacked(ot)) for t, ot in zip(ts, ots)]
        size *= 2
    xs = [_dot(stacked(t), x) for t, x in zip(ts, xs)]
    rys = [p[1] + pltpu.roll(_dot(t[2], p[2]), HEAD_DIM, 1) + _dot(t[1], x)
           for p, t, x in zip(prep, tri, xs)]
    out = []
    for p, x, ry in zip(prep, xs, rys):
        v_s, bk_s, g_tot = p[2], p[3], p[6]
        x_a = jnp.where(own, x, 0.0).astype(BF16)
        x_u = pltpu.roll(jnp.where(own, 0.0, x), HEAD_DIM, 1).astype(BF16)
        pt = jnp.where(eye, g_tot, 0.0) + _dot_tn(bk_s[:c2], x_a)
        qt = _dot_tn(bk_s, jnp.concatenate([x_u, v_s], axis=0))
        rh = jnp.where(even, ry[:CHUNK], ry[CHUNK:])
        yh = pltpu.roll(jnp.where(even, ry[CHUNK:], ry[:CHUNK]), HEAD_DIM, 1)
        out.append((pt[:CHUNK] + pt[CHUNK:], qt[:CHUNK] + qt[CHUNK:], rh, yh))
    return out


def _feat_kernel(pr_ref, hp_ref, hn_ref, mu_ref, w0_ref, a0_ref, w2_ref, a2_ref, g2_ref, kkp_ref, ka_ref, rk_ref,
                 ones_ref, pt_ref, qt_ref, rh_ref, yh_ref, bonus_ref, gate_ref, *, rw_w):
    i = pl.program_id(1)
    last = pl.num_programs(1) - 1
    tt = pr_ref.shape[0]
    pr = pr_ref[...]
    ridx = lax.broadcasted_iota(jnp.int32, (tt, 1), 0)
    edge_prev = jnp.where(i == 0, 0.0, hp_ref[SUBLANES - 1:SUBLANES, :])
    edge_next = jnp.where(i == last, 0.0, hn_ref[0:1, :])
    prev = jnp.where(ridx == 0, edge_prev, pltpu.roll(pr, 1, 0))
    nxt = jnp.where(ridx == tt - 1, edge_next, pltpu.roll(pr, tt - 1, 0))
    x = pr + mu_ref[...] * (0.5 * (prev + nxt) - pr)

    r = x[:, 0:rw_w]
    k = x[:, rw_w:2 * rw_w]
    v = x[:, 2 * rw_w:3 * rw_w]
    o = 3 * rw_w
    wd = x[:, o:o + LANES]
    ad = x[:, o + LANES:o + 2 * LANES]
    gd = x[:, o + 2 * LANES:o + 3 * LANES]

    ones_bd = ones_ref[...]
    logw = -DECAY_SCALE * _sigmoid(_dot(jnp.tanh(wd), w2_ref[...]) + w0_ref[...])
    iclr = _sigmoid(_dot(ad, a2_ref[...]) + a0_ref[...])
    gate_ref[...] = _dot(_sigmoid(gd), g2_ref[...])

    kkv = k * kkp_ref[...]
    kk = kkv / jnp.maximum(jnp.sqrt(_head_sum(kkv * kkv, ones_bd)), 1e-12)
    ka = ka_ref[...]
    k_dir = [k * (1.0 + (iclr[:, d * rw_w:(d + 1) * rw_w] - 1.0) * ka) for d in range(2)]
    bonus_ref[...] = _head_sum(r * (k_dir[0] + k_dir[1]) * rk_ref[...], ones_bd) * v

    row = lax.broadcasted_iota(jnp.int32, (tt, tt), 0)
    col = lax.broadcasted_iota(jnp.int32, (tt, tt), 1)
    same = (row // CHUNK) == (col // CHUNK)
    tri = [jnp.where(same & (col <= row), 1.0, 0.0).astype(BF16),
           jnp.where(same & (col >= row), 1.0, 0.0).astype(BF16)]
    neg_kk = -kk
    lw_dir, cs_dir, b_dir = [], [], []
    for d in range(2):
        lw_d = logw[:, d * rw_w:(d + 1) * rw_w]
        lw_dir.append(lw_d)
        cs_dir.append(sum(jnp.dot(tri[d], p, preferred_element_type=F32) for p in _split3(lw_d)))
        b_dir.append(kk * iclr[:, d * rw_w:(d + 1) * rw_w])
    nchunks = tt // CHUNK
    group = _pick_tile(nchunks, FEAT_CHUNK_GROUP)
    for c0 in range(0, nchunks, group):
        keys, insts = [], []
        for c in range(c0, c0 + group):
            rows = slice(c * CHUNK, (c + 1) * CHUNK)
            for d in range(2):
                end = (c + 1) * CHUNK - 1 if d == 0 else c * CHUNK
                for p in range(rw_w // LANES):
                    ln = slice(p * LANES, (p + 1) * LANES)
                    keys.append((c, d, p, rows, ln))
                    insts.append((r[rows, ln], v[rows, ln], neg_kk[rows, ln], b_dir[d][rows, ln], k_dir[d][rows, ln],
                                  cs_dir[d][rows, ln], lw_dir[d][rows, ln], cs_dir[d][end:end + 1, ln], d == 0))
        for (c, d, p, rows, ln), (pt, qt, rh, yh) in zip(keys, _chunk_summaries(insts)):
            pt_ref[c, d, p] = pt
            qt_ref[c, d, p] = qt
            rh_ref[d, rows, ln] = rh.astype(rh_ref.dtype)
            yh_ref[d, rows, ln] = yh


def _rwkv_features(pr, fp, tt):
    b, t, w_all = pr.shape
    rw_w = fp["kkp"].shape[1]
    npair = rw_w // LANES
    nt = t // tt
    cpt = tt // CHUNK
    hb = tt // SUBLANES
    kern = functools.partial(_feat_kernel, rw_w=rw_w)
    names = ("mu", "w0", "a0", "w2", "a2", "g2", "kkp", "ka", "rk", "ones")
    return pl.pallas_call(
        kern,
        grid=(b, nt),
        in_specs=[pl.BlockSpec((None, tt, w_all), lambda bi, i: (bi, i, 0)),
                  pl.BlockSpec((None, SUBLANES, w_all), lambda bi, i: (bi, jnp.maximum(i * hb - 1, 0), 0)),
                  pl.BlockSpec((None, SUBLANES, w_all), lambda bi, i: (bi, jnp.minimum((i + 1) * hb, nt * hb - 1), 0))]
                 + [_const_spec(fp[n].shape) for n in names],
        out_specs=[pl.BlockSpec((None, cpt, 2, npair, CHUNK, LANES), lambda bi, i: (bi, i, 0, 0, 0, 0)),
                   pl.BlockSpec((None, cpt, 2, npair, CHUNK, LANES), lambda bi, i: (bi, i, 0, 0, 0, 0)),
                   pl.BlockSpec((None, 2, tt, rw_w), lambda bi, i: (bi, 0, i, 0)),
                   pl.BlockSpec((None, 2, tt, rw_w), lambda bi, i: (bi, 0, i, 0)),
                   pl.BlockSpec((None, tt, rw_w), lambda bi, i: (bi, i, 0)),
                   pl.BlockSpec((None, tt, rw_w), lambda bi, i: (bi, i, 0))],
        out_shape=[jax.ShapeDtypeStruct((b, t // CHUNK, 2, npair, CHUNK, LANES), F32),
                   jax.ShapeDtypeStruct((b, t // CHUNK, 2, npair, CHUNK, LANES), F32),
                   jax.ShapeDtypeStruct((b, 2, t, rw_w), BF16),
                   jax.ShapeDtypeStruct((b, 2, t, rw_w), F32),
                   jax.ShapeDtypeStruct((b, t, rw_w), F32),
                   jax.ShapeDtypeStruct((b, t, rw_w), F32)],
        compiler_params=_cparams("parallel", "parallel"),
    )(pr, pr, pr, *[fp[n] for n in names])


def _scan_kernel(s0_ref, ptf_ref, qtf_ref, ptb_ref, qtb_ref, rhf_ref, yhf_ref, rhb_ref, yhb_ref,
                 yf_ref, yb_ref, sfin_ref, st_ref, *, cps, npair):
    i = pl.program_id(1)

    @pl.when(i == 0)
    def _():
        st_ref[...] = s0_ref[...]

    even = lax.broadcasted_iota(jnp.int32, (CHUNK, LANES), 1) < HEAD_DIM
    zero = jnp.zeros((), BF16)
    dirs = ((ptf_ref, qtf_ref, rhf_ref, yhf_ref, yf_ref), (ptb_ref, qtb_ref, rhb_ref, yhb_ref, yb_ref))
    keys = [(d, p) for d in range(2) for p in range(npair)]
    st = [st_ref[d, p] for d, p in keys]
    for step in range(cps):
        hl = []
        for s in st:
            hi = s.astype(BF16)
            lo = (s - hi.astype(F32)).astype(BF16)
            hl.append(jnp.concatenate([jnp.concatenate([jnp.where(even, hi, zero), jnp.where(even, zero, hi)], axis=0),
                                       jnp.concatenate([jnp.where(even, lo, zero), jnp.where(even, zero, lo)], axis=0)],
                                      axis=1))
        new = []
        for (d, p), s2 in zip(keys, hl):
            pt_ref, qt_ref, rh_ref, yh_ref, y_ref = dirs[d]
            c = step if d == 0 else cps - 1 - step
            rows = slice(c * CHUNK, (c + 1) * CHUNK)
            ln = slice(p * LANES, (p + 1) * LANES)
            y_ref[rows, ln] = jnp.dot(rh_ref[rows, ln], s2[:, :LANES], preferred_element_type=F32) + yh_ref[rows, ln]
            pt_hi, pt_lo = _split2(pt_ref[c, p])
            n2 = jnp.dot(pt_hi, s2, preferred_element_type=F32)
            new.append(n2[:, :LANES] + n2[:, LANES:] + jnp.dot(pt_lo, s2[:, :LANES], preferred_element_type=F32)
                       + qt_ref[c, p])
        st = new
    for (d, p), s in zip(keys, st):
        st_ref[d, p] = s

    @pl.when(i == pl.num_programs(1) - 1)
    def _():
        sfin_ref[...] = st_ref[...]


def _rwkv_scan(s0, pt, qt, rh, yh, cps):
    b, nc, _, npair, _, _ = pt.shape
    t, rw_w = rh.shape[2], rh.shape[3]
    ns = nc // cps
    ts = cps * CHUNK
    kern = functools.partial(_scan_kernel, cps=cps, npair=npair)
    mat_f = pl.BlockSpec((None, cps, None, npair, CHUNK, LANES), lambda bi, i: (bi, i, 0, 0, 0, 0))
    mat_b = pl.BlockSpec((None, cps, None, npair, CHUNK, LANES), lambda bi, i: (bi, ns - 1 - i, 1, 0, 0, 0))
    tok_f = pl.BlockSpec((None, None, ts, rw_w), lambda bi, i: (bi, 0, i, 0))
    tok_b = pl.BlockSpec((None, None, ts, rw_w), lambda bi, i: (bi, 1, ns - 1 - i, 0))
    state = pl.BlockSpec((None, 2, npair, CHUNK, LANES), lambda bi, i: (bi, 0, 0, 0, 0))
    return pl.pallas_call(
        kern,
        grid=(b, ns),
        in_specs=[state, mat_f, mat_f, mat_b, mat_b, tok_f, tok_f, tok_b, tok_b],
        out_specs=[pl.BlockSpec((None, ts, rw_w), lambda bi, i: (bi, i, 0)),
                   pl.BlockSpec((None, ts, rw_w), lambda bi, i: (bi, ns - 1 - i, 0)),
                   state],
        out_shape=[jax.ShapeDtypeStruct((b, t, rw_w), F32),
                   jax.ShapeDtypeStruct((b, t, rw_w), F32),
                   jax.ShapeDtypeStruct((b, 2, npair, CHUNK, LANES), F32)],
        scratch_shapes=[pltpu.VMEM((2, npair, CHUNK, LANES), F32)],
        compiler_params=_cparams("parallel", "arbitrary"),
    )(s0, pt, qt, pt, qt, rh, yh, rh, yh)


def _residual_mlp(xl, yl, m, ng, w1_ref, w2_ref):
    x2 = xl + _rms(yl, m[2:3] * ng[1:2])
    hm = (_rms(x2, ng[2:3] * (1.0 + m[4:5])) + m[3:4]).astype(BF16)
    out = None
    for j in range(w1_ref.shape[1] // MLP_HIDDEN_BLOCK):
        cols = slice(j * MLP_HIDDEN_BLOCK, (j + 1) * MLP_HIDDEN_BLOCK)
        hid = jnp.maximum(jnp.dot(hm, w1_ref[:, cols], preferred_element_type=F32), 0.0)
        part = jnp.dot((hid * hid).astype(BF16), w2_ref[cols, :], preferred_element_type=F32)
        out = part if out is None else out + part
    return x2 + _rms(out, m[5:6] * ng[3:4])


def _mix_out_kernel(x_ref, att_ref, yf_ref, yb_ref, bonus_ref, gate_ref, mod_ref, ng_ref, lnw_ref, lnb_ref, ones_ref,
                    wo_ref, w1_ref, w2_ref, o_ref, *, att_w):
    ones_bd = ones_ref[...]
    y = yf_ref[...] + yb_ref[...]
    mean = _head_sum(y, ones_bd) * (1.0 / HEAD_DIM)
    yc = y - mean
    var = jnp.dot((yc * yc).astype(BF16), ones_bd, preferred_element_type=F32) * (1.0 / HEAD_DIM)
    yn = yc * lax.rsqrt(var + GN_EPS) * lnw_ref[...] + lnb_ref[...]
    rw = (yn + bonus_ref[...]) * gate_ref[...]
    yl = (jnp.dot(att_ref[...], wo_ref[0:att_w, :], preferred_element_type=F32)
          + jnp.dot(rw.astype(BF16), wo_ref[att_w:, :], preferred_element_type=F32))
    o_ref[...] = _residual_mlp(x_ref[...], yl, mod_ref[...], ng_ref[...], w1_ref, w2_ref)


def _mix_out_mlp(x, att, yf, yb, bonus, gate, mod_l, ng, lnw, lnb, ones_bd, wo, w1, w2, tm):
    b, s, d = x.shape
    att_w = att.shape[2]
    rw_w = yf.shape[2]
    kern = functools.partial(_mix_out_kernel, att_w=att_w)
    tok = lambda w: pl.BlockSpec((None, tm, w), lambda bi, i: (bi, i, 0))
    return pl.pallas_call(
        kern,
        grid=(b, s // tm),
        in_specs=[tok(d), tok(att_w), tok(rw_w), tok(rw_w), tok(rw_w), tok(rw_w),
                  pl.BlockSpec((None, 6, d), lambda bi, i: (bi, 0, 0)),
                  _const_spec(ng.shape), _const_spec(lnw.shape), _const_spec(lnb.shape), _const_spec(ones_bd.shape),
                  _const_spec(wo.shape), _const_spec(w1.shape), _const_spec(w2.shape)],
        out_specs=tok(d),
        out_shape=jax.ShapeDtypeStruct((b, s, d), F32),
        compiler_params=_cparams("parallel", "parallel"),
    )(x, att, yf, yb, bonus, gate, mod_l, ng, lnw, lnb, ones_bd, wo, w1, w2)


def _conv_kernel(x_ref, xp_ref, xn_ref, mod_ref, ng_ref, wi_ref, cw_ref, wo_ref, w1_ref, w2_ref, o_ref):
    i = pl.program_id(1)
    last = pl.num_programs(1) - 1
    d = x_ref.shape[1]
    tm = x_ref.shape[0]
    m = mod_ref[...]
    ng = ng_ref[...]

    def modnorm(u):
        return (_rms(u, ng[0:1] * (1.0 + m[1:2])) + m[0:1]).astype(BF16)

    x = x_ref[...]
    rows = jnp.concatenate([x, xp_ref[...], xn_ref[...]], axis=0)
    proj = jnp.dot(modnorm(rows), wi_ref[...], preferred_element_type=F32)
    z_all = proj[:, d:2 * d] * proj[:, 2 * d:]
    z = z_all[:tm]
    zp = jnp.where(i == 0, 0.0, z_all[tm + SUBLANES - 1:tm + SUBLANES, :])
    zn = jnp.where(i == last, 0.0, z_all[tm + SUBLANES:tm + SUBLANES + 1, :])
    ridx = lax.broadcasted_iota(jnp.int32, (tm, 1), 0)
    prev = jnp.where(ridx == 0, zp, pltpu.roll(z, 1, 0))
    nxt = jnp.where(ridx == tm - 1, zn, pltpu.roll(z, tm - 1, 0))
    cw = cw_ref[...]
    y = proj[:tm, :d] * (prev * cw[0:1] + z * cw[1:2] + nxt * cw[2:3])
    yl = jnp.dot(y.astype(BF16), wo_ref[...], preferred_element_type=F32)
    o_ref[...] = _residual_mlp(x, yl, m, ng, w1_ref, w2_ref)


def _conv_mlp(x, mod_l, ng, wi, cw, wo, w1, w2, tm):
    b, s, d = x.shape
    hb = tm // SUBLANES
    nt = s // tm
    return pl.pallas_call(
        _conv_kernel,
        grid=(b, nt),
        in_specs=[pl.BlockSpec((None, tm, d), lambda bi, i: (bi, i, 0)),
                  pl.BlockSpec((None, SUBLANES, d), lambda bi, i: (bi, jnp.maximum(i * hb - 1, 0), 0)),
                  pl.BlockSpec((None, SUBLANES, d), lambda bi, i: (bi, jnp.minimum((i + 1) * hb, nt * hb - 1), 0)),
                  pl.BlockSpec((None, 6, d), lambda bi, i: (bi, 0, 0)),
                  _const_spec(ng.shape), _const_spec(wi.shape), _const_spec(cw.shape), _const_spec(wo.shape),
                  _const_spec(w1.shape), _const_spec(w2.shape)],
        out_specs=pl.BlockSpec((None, tm, d), lambda bi, i: (bi, i, 0)),
        out_shape=jax.ShapeDtypeStruct((b, s, d), F32),
        compiler_params=_cparams("parallel", "parallel"),
    )(x, x, x, mod_l, ng, wi, cw, wo, w1, w2)


def _block_diag2(m):
    z = jnp.zeros_like(m[0])
    return jnp.concatenate([jnp.concatenate([m[0], z], axis=1), jnp.concatenate([z, m[1]], axis=1)], axis=0)


def _pick_tile(n, want):
    t = min(n, want)
    while n % t:
        t //= 2
    return t


def kernel(x, c, ctx, c_ctx, mod_w, mod_b, norm_g, mlp_w1, mlp_w2, ab_w_in, ab_w_out, att_sink, rwkv_mu, rwkv_w0,
           rwkv_w2, rwkv_a0, rwkv_a2, rwkv_g2, rwkv_kk, rwkv_ka, rwkv_rk, rwkv_ln_w, rwkv_ln_b, conv_w_in, conv_w,
           conv_w_out):
    b, s, d = x.shape
    l = ctx.shape[1]
    depth = mod_w.shape[0]
    assert depth == 2, "layer schedule below is written for one attention/RWKV layer followed by one conv layer"
    att_w = att_sink.shape[1] * HEAD_DIM
    kv_w = att_w // ATT_GROUP
    rw_w = rwkv_kk.shape[1]
    rw_in = rwkv_mu.shape[1]
    assert s % ATT_BLOCK == 0 and s % CHUNK == 0 and l % CHUNK == 0 and kv_w % LANES == 0 and rw_w % LANES == 0

    rows = -(-(b + 1) // SUBLANES) * SUBLANES
    cv = jnp.concatenate([c, c_ctx[None, :], jnp.zeros((rows - b - 1, d), F32)], axis=0)
    mod = _modulation(cv, mod_w, mod_b)
    mod_l = [mod[i, :b].reshape(b, 6, d) for i in range(depth)]
    mod_c0 = mod[0, b].reshape(6, d)

    w_in = ab_w_in[0]
    wq, wk, wv, wr = (w_in[:, :att_w], w_in[:, att_w:att_w + kv_w], w_in[:, att_w + kv_w:att_w + 2 * kv_w],
                      w_in[:, att_w + 2 * kv_w:])
    n_heads = att_w // HEAD_DIM
    perm = jnp.arange(n_heads).reshape(n_heads // ATT_GROUP, ATT_GROUP).T.reshape(-1)
    wq = wq.reshape(d, n_heads, HEAD_DIM)[:, perm].reshape(d, att_w)
    w_lat = jnp.concatenate([wq, _rot_cols(wq), wk, _rot_cols(wk), wv, wr], axis=1).astype(BF16)
    w_ctx = jnp.concatenate([wk, wv, wr], axis=1).astype(BF16)
    cos, sin = _rope_tables(s)
    g0 = norm_g[0, 0].reshape(1, d)
    q, k, v, pr = _project_latent(x, mod_l[0], g0, w_lat, cos, sin, att_w, kv_w, rw_in, _pick_tile(s, PROJ_ROWS))
    kc, vc, prc = _project_ctx(ctx, mod_c0, g0, w_ctx, kv_w, rw_in)
    sink_rows = jnp.repeat(att_sink[0][perm].reshape(att_w // LANES, LANES // HEAD_DIM), ATT_BLOCK, axis=1)
    sink_rows = jnp.broadcast_to(sink_rows[..., None], sink_rows.shape + (LANES,))
    att = _attention(q, k, v, kc, vc, sink_rows)
    w_out = ab_w_out[0]
    w_out = jnp.concatenate([w_out[:att_w].reshape(n_heads, HEAD_DIM, d)[perm].reshape(att_w, d), w_out[att_w:]], axis=0)

    head_id = jnp.arange(rw_w) // HEAD_DIM
    ones_bd = (head_id[:, None] == head_id[None, :]).astype(BF16)
    fp = dict(mu=rwkv_mu[0].reshape(1, rw_in),
              w0=rwkv_w0[0].reshape(1, 2 * rw_w), a0=rwkv_a0[0].reshape(1, 2 * rw_w),
              w2=_block_diag2(rwkv_w2[0]).astype(BF16), a2=_block_diag2(rwkv_a2[0]).astype(BF16),
              g2=rwkv_g2[0].astype(BF16),
              kkp=rwkv_kk[0].reshape(1, rw_w), ka=rwkv_ka[0].reshape(1, rw_w), rk=rwkv_rk[0].reshape(1, rw_w),
              ones=ones_bd)
    npair = rw_w // LANES
    ptc, qtc, rhc, yhc, _, _ = _rwkv_features(prc, fp, _pick_tile(l, FEAT_ROWS))
    zero_state = jnp.zeros((b, 2, npair, CHUNK, LANES), F32)
    _, _, s_ctx = _rwkv_scan(zero_state, ptc, qtc, rhc, yhc, _pick_tile(l // CHUNK, SCAN_CHUNKS))
    pt, qt, rh, yh, bonus, gate = _rwkv_features(pr, fp, _pick_tile(s, FEAT_ROWS))
    yf, yb, _ = _rwkv_scan(s_ctx, pt, qt, rh, yh, _pick_tile(s // CHUNK, SCAN_CHUNKS))

    xl = _mix_out_mlp(x, att, yf, yb, bonus, gate, mod_l[0], norm_g[0],
                      rwkv_ln_w[0].reshape(1, rw_w), rwkv_ln_b[0].reshape(1, rw_w), ones_bd,
                      w_out.astype(BF16), mlp_w1[0].astype(BF16), mlp_w2[0].astype(BF16), _pick_tile(s, MLP_ROWS))

    return _conv_mlp(xl, mod_l[1], norm_g[1], conv_w_in[0].astype(BF16), conv_w[0], conv_w_out[0].astype(BF16),
                     mlp_w1[1].astype(BF16), mlp_w2[1].astype(BF16), _pick_tile(s, MLP_ROWS))
```

```python
import functools
import math

import jax
import jax.numpy as jnp
from jax import lax
from jax.experimental import pallas as pl
from jax.experimental.pallas import tpu as pltpu

F32 = jnp.float32
BF16 = jnp.bfloat16

HEAD_DIM = 64
GRID_W = 64
WINDOW = 128
ATT_BLOCK = 128
ATT_GROUP = 4
ROPE_BASE = 10000.0
NORM_EPS = 1e-6
GN_EPS = 64e-5
LANES = 128
SUBLANES = 8
CHUNK = 64
DECAY_SCALE = math.exp(-0.5)
PROJ_ROWS = 1024
FEAT_ROWS = 256
FEAT_CHUNK_GROUP = 2
SCAN_CHUNKS = 8
ATT_SUBBLOCKS = 16
MLP_ROWS = 512
MLP_HIDDEN_BLOCK = 2048
VMEM_LIMIT = 56 * 1024 * 1024


def _cparams(*sem):
    return pltpu.CompilerParams(dimension_semantics=sem, vmem_limit_bytes=VMEM_LIMIT)


def _const_spec(shape):
    nd = len(shape)
    return pl.BlockSpec(shape, lambda *_: (0,) * nd, pipeline_mode=pl.Buffered(1))


def _dot(a, b):
    return jnp.dot(a.astype(BF16), b.astype(BF16), preferred_element_type=F32)


def _dot_nt(a, b):
    return lax.dot_general(a.astype(BF16), b.astype(BF16), (((1,), (1,)), ((), ())), preferred_element_type=F32)


def _dot_tn(a, b):
    return lax.dot_general(a.astype(BF16), b.astype(BF16), (((0,), (0,)), ((), ())), preferred_element_type=F32)


def _split2(x):
    hi = x.astype(BF16)
    lo = (x - hi.astype(F32)).astype(BF16)
    return hi, lo


def _split3(x):
    hi = x.astype(BF16)
    r1 = x - hi.astype(F32)
    mid = r1.astype(BF16)
    lo = (r1 - mid.astype(F32)).astype(BF16)
    return hi, mid, lo


def _head_sum(x, ones_bd):
    hi, lo = _split2(x)
    return (jnp.dot(hi, ones_bd, preferred_element_type=F32)
            + jnp.dot(lo, ones_bd, preferred_element_type=F32))


def _rms(u, g):
    return u * lax.rsqrt(jnp.mean(u * u, axis=-1, keepdims=True) + NORM_EPS) * g


def _sigmoid(z):
    return 1.0 / (1.0 + jnp.exp(-z))


def _mod_kernel(cv_ref, w_ref, b_ref, o_ref):
    cv = cv_ref[...]
    s = cv * _sigmoid(cv)
    o_ref[...] = _dot(s, w_ref[...]) + b_ref[...]


def _modulation(cv, mod_w, mod_b):
    depth, d, six_d = mod_w.shape
    rows = cv.shape[0]
    nj = six_d // d
    return pl.pallas_call(
        _mod_kernel,
        grid=(depth, nj),
        in_specs=[pl.BlockSpec((rows, d), lambda l, j: (0, 0)),
                  pl.BlockSpec((None, d, d), lambda l, j: (l, 0, j)),
                  pl.BlockSpec((None, 1, d), lambda l, j: (l, 0, j))],
        out_specs=pl.BlockSpec((None, rows, d), lambda l, j: (l, 0, j)),
        out_shape=jax.ShapeDtypeStruct((depth, rows, six_d), F32),
        compiler_params=_cparams("arbitrary", "arbitrary"),
    )(cv, mod_w, mod_b.reshape(depth, 1, six_d))


def _proj_kernel(x_ref, mod_ref, g_ref, w_ref, cos_ref, sin_ref, q_ref, k_ref, v_ref, pr_ref, *, att_w, kv_w):
    m = mod_ref[...]
    h = (_rms(x_ref[...], g_ref[...] * (1.0 + m[1:2])) + m[0:1]).astype(BF16)
    cos, sin = cos_ref[...], sin_ref[...]
    scale = HEAD_DIM ** -0.5
    p = jnp.dot(h, w_ref[...], preferred_element_type=F32)
    o = 0
    for j in range(att_w // LANES):
        u = p[:, o + j * LANES:o + (j + 1) * LANES]
        ur = p[:, o + att_w + j * LANES:o + att_w + (j + 1) * LANES]
        q_ref[:, j * LANES:(j + 1) * LANES] = ((u * cos + ur * sin) * scale).astype(q_ref.dtype)
    o += 2 * att_w
    for j in range(kv_w // LANES):
        u = p[:, o + j * LANES:o + (j + 1) * LANES]
        ur = p[:, o + kv_w + j * LANES:o + kv_w + (j + 1) * LANES]
        k_ref[:, j * LANES:(j + 1) * LANES] = (u * cos + ur * sin).astype(k_ref.dtype)
    o += 2 * kv_w
    v_ref[...] = p[:, o:o + kv_w].astype(v_ref.dtype)
    o += kv_w
    pr_ref[...] = p[:, o:]


def _proj_ctx_kernel(x_ref, mod_ref, g_ref, w_ref, k_ref, v_ref, pr_ref, *, kv_w):
    m = mod_ref[...]
    h = (_rms(x_ref[...], g_ref[...] * (1.0 + m[1:2])) + m[0:1]).astype(BF16)
    k_ref[...] = jnp.dot(h, w_ref[:, 0:kv_w], preferred_element_type=F32).astype(k_ref.dtype)
    v_ref[...] = jnp.dot(h, w_ref[:, kv_w:2 * kv_w], preferred_element_type=F32).astype(v_ref.dtype)
    pr_ref[...] = jnp.dot(h, w_ref[:, 2 * kv_w:], preferred_element_type=F32)


def _rot_cols(w):
    d, n = w.shape
    m = HEAD_DIM // 4
    w4 = w.reshape(d, n // (2 * m), 2, m)
    return jnp.stack([-w4[:, :, 1], w4[:, :, 0]], axis=2).reshape(d, n)


def _rope_tables(seq):
    m = HEAD_DIM // 4
    t = jnp.arange(seq)
    inv = ROPE_BASE ** (-jnp.arange(m, dtype=F32) / m)
    ang_r = (t // GRID_W).astype(F32)[:, None] * inv[None, :]
    ang_c = (t % GRID_W).astype(F32)[:, None] * inv[None, :]
    ang = jnp.concatenate([ang_r, ang_r, ang_c, ang_c], axis=-1)
    ang = jnp.tile(ang, (1, LANES // HEAD_DIM))
    return jnp.cos(ang), jnp.sin(ang)


def _project_latent(x, mod_l, g, w_all, cos, sin, att_w, kv_w, rw_w, tm):
    b, s, d = x.shape
    n_all = w_all.shape[1]
    kern = functools.partial(_proj_kernel, att_w=att_w, kv_w=kv_w)
    return pl.pallas_call(
        kern,
        grid=(b, s // tm),
        in_specs=[pl.BlockSpec((None, tm, d), lambda bi, i: (bi, i, 0)),
                  pl.BlockSpec((None, 6, d), lambda bi, i: (bi, 0, 0)),
                  _const_spec((1, d)),
                  _const_spec((d, n_all)),
                  pl.BlockSpec((tm, LANES), lambda bi, i: (i, 0)),
                  pl.BlockSpec((tm, LANES), lambda bi, i: (i, 0))],
        out_specs=[pl.BlockSpec((None, tm, att_w), lambda bi, i: (bi, i, 0)),
                   pl.BlockSpec((None, tm, kv_w), lambda bi, i: (bi, i, 0)),
                   pl.BlockSpec((None, tm, kv_w), lambda bi, i: (bi, i, 0)),
                   pl.BlockSpec((None, tm, rw_w), lambda bi, i: (bi, i, 0))],
        out_shape=[jax.ShapeDtypeStruct((b, s, att_w), BF16),
                   jax.ShapeDtypeStruct((b, s, kv_w), BF16),
                   jax.ShapeDtypeStruct((b, s, kv_w), BF16),
                   jax.ShapeDtypeStruct((b, s, rw_w), F32)],
        compiler_params=_cparams("parallel", "parallel"),
    )(x, mod_l, g, w_all, cos, sin)


def _project_ctx(ctx, mod_c, g, w_ctx, kv_w, rw_w):
    b, l, d = ctx.shape
    kern = functools.partial(_proj_ctx_kernel, kv_w=kv_w)
    return pl.pallas_call(
        kern,
        grid=(b,),
        in_specs=[pl.BlockSpec((None, l, d), lambda bi: (bi, 0, 0)),
                  _const_spec((6, d)),
                  _const_spec((1, d)),
                  _const_spec((d, w_ctx.shape[1]))],
        out_specs=[pl.BlockSpec((None, l, kv_w), lambda bi: (bi, 0, 0)),
                   pl.BlockSpec((None, l, kv_w), lambda bi: (bi, 0, 0)),
                   pl.BlockSpec((None, l, rw_w), lambda bi: (bi, 0, 0))],
        out_shape=[jax.ShapeDtypeStruct((b, l, kv_w), BF16),
                   jax.ShapeDtypeStruct((b, l, kv_w), BF16),
                   jax.ShapeDtypeStruct((b, l, rw_w), F32)],
        compiler_params=_cparams("parallel"),
    )(ctx, mod_c, g, w_ctx)


def _attn_kernel(sink_ref, q_ref, kp_ref, kc_ref, kn_ref, vp_ref, vc_ref, vn_ref, kx_ref, vx_ref, o_ref, *, nsub):
    i = pl.program_id(1)
    nb = pl.num_programs(1)
    blk = ATT_BLOCK
    kall = jnp.concatenate([kp_ref[...], kc_ref[...], kn_ref[...]], axis=0)
    vall = jnp.concatenate([vp_ref[...], vc_ref[...], vn_ref[...]], axis=0)
    vall = jnp.concatenate([vall, jnp.ones_like(vall)], axis=1)
    kctx = kx_ref[...]
    vctx = jnp.concatenate([vx_ref[...], jnp.ones_like(vx_ref[...])], axis=1)
    row = lax.broadcasted_iota(jnp.int32, (2 * blk, 3 * blk), 0) % blk
    col = lax.broadcasted_iota(jnp.int32, (2 * blk, 3 * blk), 1)
    band = jnp.abs(row + blk - col) <= WINDOW
    even = lax.broadcasted_iota(jnp.int32, (blk, LANES), 1) < HEAD_DIM
    zero = jnp.zeros((), q_ref.dtype)
    tiles = range(q_ref.shape[1] // LANES)
    sinks = [sink_ref[j] for j in tiles]

    def band_masked(s, va):
        return jnp.concatenate([jnp.where(va[:, :blk], s[:, :blk], -1e30), s[:, blk:2 * blk],
                                jnp.where(va[:, 2 * blk:], s[:, 2 * blk:], -1e30)], axis=1)

    def scores(a):
        va = band
        if a == 0:
            va = va & jnp.logical_not((i == 0) & (col < blk))
        if a == nsub - 1:
            va = va & jnp.logical_not((i == nb - 1) & (col >= 2 * blk))
        kloc = kall[a * blk:(a + 3) * blk]
        qs = []
        for j in tiles:
            qj = q_ref[a * blk:(a + 1) * blk, j * LANES:(j + 1) * LANES]
            qs.append(jnp.concatenate([jnp.where(even, qj, zero), jnp.where(even, zero, qj)], axis=0))
        return [band_masked(_dot_nt(qj, kloc), va) for qj in qs], [_dot_nt(qj, kctx) for qj in qs]

    def row_max(sl, sc, sk):
        t = sk
        for s in (sl, sc):
            for c0 in range(0, s.shape[1], LANES):
                t = jnp.maximum(t, s[:, c0:c0 + LANES])
        return jnp.max(t, axis=-1, keepdims=True)

    def finish(a, s_loc, s_ctx):
        vloc = vall[a * blk:(a + 3) * blk]
        mx = [row_max(sl, sc, sk) for sl, sc, sk in zip(s_loc, s_ctx, sinks)]
        p_loc = [jnp.exp((sl - m).astype(BF16)) for sl, m in zip(s_loc, mx)]
        p_ctx = [jnp.exp((sc - m).astype(BF16)) for sc, m in zip(s_ctx, mx)]
        num = [jnp.dot(pa, vloc, preferred_element_type=F32) + jnp.dot(pc, vctx, preferred_element_type=F32)
               for pa, pc in zip(p_loc, p_ctx)]
        for j, n, sk, m in zip(tiles, num, sinks, mx):
            o = n[:, :LANES] / (n[:, LANES:] + jnp.exp(sk - m))
            o_ref[a * blk:(a + 1) * blk, j * LANES:(j + 1) * LANES] = jnp.where(even, o[:blk], o[blk:]).astype(o_ref.dtype)

    pending = scores(0)
    for a in range(nsub):
        ahead = scores(a + 1) if a + 1 < nsub else None
        finish(a, *pending)
        pending = ahead


def _attention(q, k, v, kc, vc, sink_rows):
    b, s, att_w = q.shape
    kv_w = k.shape[2]
    l = kc.shape[1]
    blk = ATT_BLOCK
    nsub = _pick_tile(s // blk, ATT_SUBBLOCKS)
    nb = s // (nsub * blk)
    kern = functools.partial(_attn_kernel, nsub=nsub)
    kv_prev = pl.BlockSpec((None, blk, kv_w), lambda bi, i: (bi, jnp.maximum(i * nsub - 1, 0), 0))
    kv_cur = pl.BlockSpec((None, nsub * blk, kv_w), lambda bi, i: (bi, i, 0))
    kv_next = pl.BlockSpec((None, blk, kv_w), lambda bi, i: (bi, jnp.minimum((i + 1) * nsub, nb * nsub - 1), 0))
    kv_ctx = pl.BlockSpec((None, l, kv_w), lambda bi, i: (bi, 0, 0))
    return pl.pallas_call(
        kern,
        grid=(b, nb),
        in_specs=[_const_spec(sink_rows.shape),
                  pl.BlockSpec((None, nsub * blk, att_w), lambda bi, i: (bi, i, 0)),
                  kv_prev, kv_cur, kv_next, kv_prev, kv_cur, kv_next, kv_ctx, kv_ctx],
        out_specs=pl.BlockSpec((None, nsub * blk, att_w), lambda bi, i: (bi, i, 0)),
        out_shape=jax.ShapeDtypeStruct((b, s, att_w), BF16),
        compiler_params=_cparams("parallel", "parallel"),
    )(sink_rows, q, k, k, k, v, v, v, kc, vc)


def _stack_heads(x, even):
    return jnp.concatenate([jnp.where(even, x, 0.0), jnp.where(even, 0.0, x)], axis=0)


def _chunk_summaries(insts):
    c2 = 2 * CHUNK
    even = lax.broadcasted_iota(jnp.int32, (CHUNK, LANES), 1) < HEAD_DIM
    rt = lax.broadcasted_iota(jnp.int32, (CHUNK, LANES), 0)
    ct = lax.broadcasted_iota(jnp.int32, (CHUNK, LANES), 1) % HEAD_DIM
    masks = {True: (ct < rt, ct <= rt), False: (ct > rt, ct >= rt)}
    eye = lax.broadcasted_iota(jnp.int32, (LANES, LANES), 0) == lax.broadcasted_iota(jnp.int32, (LANES, LANES), 1)

    zero = jnp.zeros((), BF16)

    def stacked(f):
        f = f.astype(BF16)
        return jnp.concatenate([jnp.where(even, f, zero), jnp.where(even, zero, f)], axis=0)

    prep = []
    for rr, vv, aa, bb, kd, cs, lw, tot, forward in insts:
        g_inv = jnp.exp(-cs)
        g_rem = jnp.exp(tot - cs)
        a_t = aa * jnp.exp(cs - lw)
        r_t = rr * jnp.exp(cs)
        v_s = stacked(vv)
        bk_s = jnp.concatenate([stacked(bb * g_rem), stacked(kd * g_rem)], axis=0)
        lhs = jnp.concatenate([a_t, r_t], axis=0).astype(BF16)
        rhs = jnp.concatenate([stacked(bb * g_inv), stacked(kd * g_inv)], axis=0)
        prep.append((_stack_heads(a_t, even), _stack_heads(r_t, even), v_s, bk_s, lhs, rhs, jnp.exp(tot), masks[forward]))

    gs = [_dot_nt(p[4], p[5]) for p in prep]
    tri, lfs = [], []
    for g, p in zip(gs, prep):
        strict, incl = p[7]
        lfs.append(jnp.where(strict, g[:CHUNK, :c2], 0.0).astype(BF16))
        tri.append((stacked(jnp.where(strict, g[:CHUNK, c2:], 0.0)), stacked(jnp.where(incl, g[CHUNK:, :c2], 0.0)),
                    stacked(jnp.where(incl, g[CHUNK:, c2:], 0.0))))
    own = (lax.broadcasted_iota(jnp.int32, (c2, LANES), 0) < CHUNK) == (
        lax.broadcasted_iota(jnp.int32, (c2, LANES), 1) < HEAD_DIM)
    xs = [p[0] + pltpu.roll(_dot(t[0], p[2]), HEAD_DIM, 1) for p, t in zip(prep, tri)]

    ts = [jnp.where((rt // 2 == ct // 2), lf.astype(F32), 0.0) + jnp.where(rt == ct, 1.0, 0.0) for lf in lfs]
    size = 2
    while size < CHUNK:
        couple = (rt // (2 * size) == ct // (2 * size)) & (rt // size != ct // size)
        os_ = [jnp.where(couple, lf, zero) for lf in lfs]
        ots = [_dot(o, stacked(t)) for o, t in zip(os_, ts)]
        ts = [t + _dot(t, stacked(ot)) for t, ot in zip(ts, ots)]
        size *= 2
    xs = [_dot(stacked(t), x) for t, x in zip(ts, xs)]
    rys = [p[1] + pltpu.roll(_dot(t[2], p[2]), HEAD_DIM, 1) + _dot(t[1], x)
           for p, t, x in zip(prep, tri, xs)]
    out = []
    for p, x, ry in zip(prep, xs, rys):
        v_s, bk_s, g_tot = p[2], p[3], p[6]
        x_a = jnp.where(own, x, 0.0).astype(BF16)
        x_u = pltpu.roll(jnp.where(own, 0.0, x), HEAD_DIM, 1).astype(BF16)
        pt = jnp.where(eye, g_tot, 0.0) + _dot_tn(bk_s[:c2], x_a)
        qt = _dot_tn(bk_s, jnp.concatenate([x_u, v_s], axis=0))
        rh = jnp.where(even, ry[:CHUNK], ry[CHUNK:])
        yh = pltpu.roll(jnp.where(even, ry[CHUNK:], ry[:CHUNK]), HEAD_DIM, 1)
        out.append((pt[:CHUNK] + pt[CHUNK:], qt[:CHUNK] + qt[CHUNK:], rh, yh))
    return out


def _feat_kernel(pr_ref, hp_ref, hn_ref, mu_ref, w0_ref, a0_ref, w2_ref, a2_ref, g2_ref, kkp_ref, ka_ref, rk_ref,
                 ones_ref, pt_ref, qt_ref, rh_ref, yh_ref, bonus_ref, gate_ref, *, rw_w):
    i = pl.program_id(1)
    last = pl.num_programs(1) - 1
    tt = pr_ref.shape[0]
    pr = pr_ref[...]
    ridx = lax.broadcasted_iota(jnp.int32, (tt, 1), 0)
    edge_prev = jnp.where(i == 0, 0.0, hp_ref[SUBLANES - 1:SUBLANES, :])
    edge_next = jnp.where(i == last, 0.0, hn_ref[0:1, :])
    prev = jnp.where(ridx == 0, edge_prev, pltpu.roll(pr, 1, 0))
    nxt = jnp.where(ridx == tt - 1, edge_next, pltpu.roll(pr, tt - 1, 0))
    x = pr + mu_ref[...] * (0.5 * (prev + nxt) - pr)

    r = x[:, 0:rw_w]
    k = x[:, rw_w:2 * rw_w]
    v = x[:, 2 * rw_w:3 * rw_w]
    o = 3 * rw_w
    wd = x[:, o:o + LANES]
    ad = x[:, o + LANES:o + 2 * LANES]
    gd = x[:, o + 2 * LANES:o + 3 * LANES]

    ones_bd = ones_ref[...]
    logw = -DECAY_SCALE * _sigmoid(_dot(jnp.tanh(wd), w2_ref[...]) + w0_ref[...])
    iclr = _sigmoid(_dot(ad, a2_ref[...]) + a0_ref[...])
    gate_ref[...] = _dot(_sigmoid(gd), g2_ref[...])

    kkv = k * kkp_ref[...]
    kk = kkv / jnp.maximum(jnp.sqrt(_head_sum(kkv * kkv, ones_bd)), 1e-12)
    ka = ka_ref[...]
    k_dir = [k * (1.0 + (iclr[:, d * rw_w:(d + 1) * rw_w] - 1.0) * ka) for d in range(2)]
    bonus_ref[...] = _head_sum(r * (k_dir[0] + k_dir[1]) * rk_ref[...], ones_bd) * v

    row = lax.broadcasted_iota(jnp.int32, (tt, tt), 0)
    col = lax.broadcasted_iota(jnp.int32, (tt, tt), 1)
    same = (row // CHUNK) == (col // CHUNK)
    tri = [jnp.where(same & (col <= row), 1.0, 0.0).astype(BF16),
           jnp.where(same & (col >= row), 1.0, 0.0).astype(BF16)]
    neg_kk = -kk
    lw_dir, cs_dir, b_dir = [], [], []
    for d in range(2):
        lw_d = logw[:, d * rw_w:(d + 1) * rw_w]
        lw_dir.append(lw_d)
        cs_dir.append(sum(jnp.dot(tri[d], p, preferred_element_type=F32) for p in _split3(lw_d)))
        b_dir.append(kk * iclr[:, d * rw_w:(d + 1) * rw_w])
    nchunks = tt // CHUNK
    group = _pick_tile(nchunks, FEAT_CHUNK_GROUP)
    for c0 in range(0, nchunks, group):
        keys, insts = [], []
        for c in range(c0, c0 + group):
            rows = slice(c * CHUNK, (c + 1) * CHUNK)
            for d in range(2):
                end = (c + 1) * CHUNK - 1 if d == 0 else c * CHUNK
                for p in range(rw_w // LANES):
                    ln = slice(p * LANES, (p + 1) * LANES)
                    keys.append((c, d, p, rows, ln))
                    insts.append((r[rows, ln], v[rows, ln], neg_kk[rows, ln], b_dir[d][rows, ln], k_dir[d][rows, ln],
                                  cs_dir[d][rows, ln], lw_dir[d][rows, ln], cs_dir[d][end:end + 1, ln], d == 0))
        for (c, d, p, rows, ln), (pt, qt, rh, yh) in zip(keys, _chunk_summaries(insts)):
            pt_ref[c, d, p] = pt
            qt_ref[c, d, p] = qt
            rh_ref[d, rows, ln] = rh.astype(rh_ref.dtype)
            yh_ref[d, rows, ln] = yh.astype(yh_ref.dtype)


def _rwkv_features(pr, fp, tt):
    b, t, w_all = pr.shape
    rw_w = fp["kkp"].shape[1]
    npair = rw_w // LANES
    nt = t // tt
    cpt = tt // CHUNK
    hb = tt // SUBLANES
    kern = functools.partial(_feat_kernel, rw_w=rw_w)
    names = ("mu", "w0", "a0", "w2", "a2", "g2", "kkp", "ka", "rk", "ones")
    return pl.pallas_call(
        kern,
        grid=(b, nt),
        in_specs=[pl.BlockSpec((None, tt, w_all), lambda bi, i: (bi, i, 0)),
                  pl.BlockSpec((None, SUBLANES, w_all), lambda bi, i: (bi, jnp.maximum(i * hb - 1, 0), 0)),
                  pl.BlockSpec((None, SUBLANES, w_all), lambda bi, i: (bi, jnp.minimum((i + 1) * hb, nt * hb - 1), 0))]
                 + [_const_spec(fp[n].shape) for n in names],
        out_specs=[pl.BlockSpec((None, cpt, 2, npair, CHUNK, LANES), lambda bi, i: (bi, i, 0, 0, 0, 0)),
                   pl.BlockSpec((None, cpt, 2, npair, CHUNK, LANES), lambda bi, i: (bi, i, 0, 0, 0, 0)),
                   pl.BlockSpec((None, 2, tt, rw_w), lambda bi, i: (bi, 0, i, 0)),
                   pl.BlockSpec((None, 2, tt, rw_w), lambda bi, i: (bi, 0, i, 0)),
                   pl.BlockSpec((None, tt, rw_w), lambda bi, i: (bi, i, 0)),
                   pl.BlockSpec((None, tt, rw_w), lambda bi, i: (bi, i, 0))],
        out_shape=[jax.ShapeDtypeStruct((b, t // CHUNK, 2, npair, CHUNK, LANES), F32),
                   jax.ShapeDtypeStruct((b, t // CHUNK, 2, npair, CHUNK, LANES), F32),
                   jax.ShapeDtypeStruct((b, 2, t, rw_w), BF16),
                   jax.ShapeDtypeStruct((b, 2, t, rw_w), BF16),
                   jax.ShapeDtypeStruct((b, t, rw_w), F32),
                   jax.ShapeDtypeStruct((b, t, rw_w), F32)],
        compiler_params=_cparams("parallel", "parallel"),
    )(pr, pr, pr, *[fp[n] for n in names])


def _scan_kernel(s0_ref, ptf_ref, qtf_ref, ptb_ref, qtb_ref, rhf_ref, yhf_ref, rhb_ref, yhb_ref,
                 yf_ref, yb_ref, sfin_ref, st_ref, *, cps, npair):
    i = pl.program_id(1)

    @pl.when(i == 0)
    def _():
        st_ref[...] = s0_ref[...]

    even = lax.broadcasted_iota(jnp.int32, (CHUNK, LANES), 1) < HEAD_DIM
    zero = jnp.zeros((), BF16)
    dirs = ((ptf_ref, qtf_ref, rhf_ref, yhf_ref, yf_ref), (ptb_ref, qtb_ref, rhb_ref, yhb_ref, yb_ref))
    keys = [(d, p) for d in range(2) for p in range(npair)]
    st = [st_ref[d, p] for d, p in keys]
    for step in range(cps):
        hl = []
        for s in st:
            hi = s.astype(BF16)
            lo = (s - hi.astype(F32)).astype(BF16)
            hl.append(jnp.concatenate([jnp.concatenate([jnp.where(even, hi, zero), jnp.where(even, zero, hi)], axis=0),
                                       jnp.concatenate([jnp.where(even, lo, zero), jnp.where(even, zero, lo)], axis=0)],
                                      axis=1))
        new = []
        for (d, p), s2 in zip(keys, hl):
            pt_ref, qt_ref, rh_ref, yh_ref, y_ref = dirs[d]
            c = step if d == 0 else cps - 1 - step
            rows = slice(c * CHUNK, (c + 1) * CHUNK)
            ln = slice(p * LANES, (p + 1) * LANES)
            y = jnp.dot(rh_ref[rows, ln], s2[:, :LANES], preferred_element_type=F32) + yh_ref[rows, ln].astype(F32)
            y_ref[rows, ln] = y.astype(y_ref.dtype)
            pt_hi, pt_lo = _split2(pt_ref[c, p])
            n2 = jnp.dot(pt_hi, s2, preferred_element_type=F32)
            new.append(n2[:, :LANES] + n2[:, LANES:] + jnp.dot(pt_lo, s2[:, :LANES], preferred_element_type=F32)
                       + qt_ref[c, p])
        st = new
    for (d, p), s in zip(keys, st):
        st_ref[d, p] = s

    @pl.when(i == pl.num_programs(1) - 1)
    def _():
        sfin_ref[...] = st_ref[...]


def _rwkv_scan(s0, pt, qt, rh, yh, cps):
    b, nc, _, npair, _, _ = pt.shape
    t, rw_w = rh.shape[2], rh.shape[3]
    ns = nc // cps
    ts = cps * CHUNK
    kern = functools.partial(_scan_kernel, cps=cps, npair=npair)
    mat_f = pl.BlockSpec((None, cps, None, npair, CHUNK, LANES), lambda bi, i: (bi, i, 0, 0, 0, 0))
    mat_b = pl.BlockSpec((None, cps, None, npair, CHUNK, LANES), lambda bi, i: (bi, ns - 1 - i, 1, 0, 0, 0))
    tok_f = pl.BlockSpec((None, None, ts, rw_w), lambda bi, i: (bi, 0, i, 0))
    tok_b = pl.BlockSpec((None, None, ts, rw_w), lambda bi, i: (bi, 1, ns - 1 - i, 0))
    state = pl.BlockSpec((None, 2, npair, CHUNK, LANES), lambda bi, i: (bi, 0, 0, 0, 0))
    return pl.pallas_call(
        kern,
        grid=(b, ns),
        in_specs=[state, mat_f, mat_f, mat_b, mat_b, tok_f, tok_f, tok_b, tok_b],
        out_specs=[pl.BlockSpec((None, ts, rw_w), lambda bi, i: (bi, i, 0)),
                   pl.BlockSpec((None, ts, rw_w), lambda bi, i: (bi, ns - 1 - i, 0)),
                   state],
        out_shape=[jax.ShapeDtypeStruct((b, t, rw_w), BF16),
                   jax.ShapeDtypeStruct((b, t, rw_w), BF16),
                   jax.ShapeDtypeStruct((b, 2, npair, CHUNK, LANES), F32)],
        scratch_shapes=[pltpu.VMEM((2, npair, CHUNK, LANES), F32)],
        compiler_params=_cparams("parallel", "arbitrary"),
    )(s0, pt, qt, pt, qt, rh, yh, rh, yh)


def _residual_mlp(xl, yl, m, ng, w1_ref, w2_ref):
    x2 = xl + _rms(yl, m[2:3] * ng[1:2])
    hm = (_rms(x2, ng[2:3] * (1.0 + m[4:5])) + m[3:4]).astype(BF16)
    out = None
    for j in range(w1_ref.shape[1] // MLP_HIDDEN_BLOCK):
        cols = slice(j * MLP_HIDDEN_BLOCK, (j + 1) * MLP_HIDDEN_BLOCK)
        hid = jnp.maximum(jnp.dot(hm, w1_ref[:, cols], preferred_element_type=F32), 0.0)
        part = jnp.dot((hid * hid).astype(BF16), w2_ref[cols, :], preferred_element_type=F32)
        out = part if out is None else out + part
    return x2 + _rms(out, m[5:6] * ng[3:4])


def _mix_out_kernel(x_ref, att_ref, yf_ref, yb_ref, bonus_ref, gate_ref, mod_ref, ng_ref, lnw_ref, lnb_ref, ones_ref,
                    wo_ref, w1_ref, w2_ref, o_ref, *, att_w):
    ones_bd = ones_ref[...]
    y = yf_ref[...].astype(F32) + yb_ref[...].astype(F32)
    mean = _head_sum(y, ones_bd) * (1.0 / HEAD_DIM)
    yc = y - mean
    var = jnp.dot((yc * yc).astype(BF16), ones_bd, preferred_element_type=F32) * (1.0 / HEAD_DIM)
    yn = yc * lax.rsqrt(var + GN_EPS) * lnw_ref[...] + lnb_ref[...]
    rw = (yn + bonus_ref[...]) * gate_ref[...]
    yl = (jnp.dot(att_ref[...], wo_ref[0:att_w, :], preferred_element_type=F32)
          + jnp.dot(rw.astype(BF16), wo_ref[att_w:, :], preferred_element_type=F32))
    o_ref[...] = _residual_mlp(x_ref[...], yl, mod_ref[...], ng_ref[...], w1_ref, w2_ref)


def _mix_out_mlp(x, att, yf, yb, bonus, gate, mod_l, ng, lnw, lnb, ones_bd, wo, w1, w2, tm):
    b, s, d = x.shape
    att_w = att.shape[2]
    rw_w = yf.shape[2]
    kern = functools.partial(_mix_out_kernel, att_w=att_w)
    tok = lambda w: pl.BlockSpec((None, tm, w), lambda bi, i: (bi, i, 0))
    return pl.pallas_call(
        kern,
        grid=(b, s // tm),
        in_specs=[tok(d), tok(att_w), tok(rw_w), tok(rw_w), tok(rw_w), tok(rw_w),
                  pl.BlockSpec((None, 6, d), lambda bi, i: (bi, 0, 0)),
                  _const_spec(ng.shape), _const_spec(lnw.shape), _const_spec(lnb.shape), _const_spec(ones_bd.shape),
                  _const_spec(wo.shape), _const_spec(w1.shape), _const_spec(w2.shape)],
        out_specs=tok(d),
        out_shape=jax.ShapeDtypeStruct((b, s, d), F32),
        compiler_params=_cparams("parallel", "parallel"),
    )(x, att, yf, yb, bonus, gate, mod_l, ng, lnw, lnb, ones_bd, wo, w1, w2)


def _conv_kernel(x_ref, xp_ref, xn_ref, mod_ref, ng_ref, wi_ref, cw_ref, wo_ref, w1_ref, w2_ref, o_ref):
    i = pl.program_id(1)
    last = pl.num_programs(1) - 1
    d = x_ref.shape[1]
    tm = x_ref.shape[0]
    m = mod_ref[...]
    ng = ng_ref[...]

    def modnorm(u):
        return (_rms(u, ng[0:1] * (1.0 + m[1:2])) + m[0:1]).astype(BF16)

    x = x_ref[...]
    rows = jnp.concatenate([x, xp_ref[...], xn_ref[...]], axis=0)
    proj = jnp.dot(modnorm(rows), wi_ref[...], preferred_element_type=F32)
    z_all = proj[:, d:2 * d] * proj[:, 2 * d:]
    z = z_all[:tm]
    zp = jnp.where(i == 0, 0.0, z_all[tm + SUBLANES - 1:tm + SUBLANES, :])
    zn = jnp.where(i == last, 0.0, z_all[tm + SUBLANES:tm + SUBLANES + 1, :])
    ridx = lax.broadcasted_iota(jnp.int32, (tm, 1), 0)
    prev = jnp.where(ridx == 0, zp, pltpu.roll(z, 1, 0))
    nxt = jnp.where(ridx == tm - 1, zn, pltpu.roll(z, tm - 1, 0))
    cw = cw_ref[...]
    y = proj[:tm, :d] * (prev * cw[0:1] + z * cw[1:2] + nxt * cw[2:3])
    yl = jnp.dot(y.astype(BF16), wo_ref[...], preferred_element_type=F32)
    o_ref[...] = _residual_mlp(x, yl, m, ng, w1_ref, w2_ref)


def _conv_mlp(x, mod_l, ng, wi, cw, wo, w1, w2, tm):
    b, s, d = x.shape
    hb = tm // SUBLANES
    nt = s // tm
    return pl.pallas_call(
        _conv_kernel,
        grid=(b, nt),
        in_specs=[pl.BlockSpec((None, tm, d), lambda bi, i: (bi, i, 0)),
                  pl.BlockSpec((None, SUBLANES, d), lambda bi, i: (bi, jnp.maximum(i * hb - 1, 0), 0)),
                  pl.BlockSpec((None, SUBLANES, d), lambda bi, i: (bi, jnp.minimum((i + 1) * hb, nt * hb - 1), 0)),
                  pl.BlockSpec((None, 6, d), lambda bi, i: (bi, 0, 0)),
                  _const_spec(ng.shape), _const_spec(wi.shape), _const_spec(cw.shape), _const_spec(wo.shape),
                  _const_spec(w1.shape), _const_spec(w2.shape)],
        out_specs=pl.BlockSpec((None, tm, d), lambda bi, i: (bi, i, 0)),
        out_shape=jax.ShapeDtypeStruct((b, s, d), F32),
        compiler_params=_cparams("parallel", "parallel"),
    )(x, x, x, mod_l, ng, wi, cw, wo, w1, w2)


def _block_diag2(m):
    z = jnp.zeros_like(m[0])
    return jnp.concatenate([jnp.concatenate([m[0], z], axis=1), jnp.concatenate([z, m[1]], axis=1)], axis=0)


def _pick_tile(n, want):
    t = min(n, want)
    while n % t:
        t //= 2
    return t


def kernel(x, c, ctx, c_ctx, mod_w, mod_b, norm_g, mlp_w1, mlp_w2, ab_w_in, ab_w_out, att_sink, rwkv_mu, rwkv_w0,
           rwkv_w2, rwkv_a0, rwkv_a2, rwkv_g2, rwkv_kk, rwkv_ka, rwkv_rk, rwkv_ln_w, rwkv_ln_b, conv_w_in, conv_w,
           conv_w_out):
    b, s, d = x.shape
    l = ctx.shape[1]
    depth = mod_w.shape[0]
    assert depth == 2, "layer schedule below is written for one attention/RWKV layer followed by one conv layer"
    att_w = att_sink.shape[1] * HEAD_DIM
    kv_w = att_w // ATT_GROUP
    rw_w = rwkv_kk.shape[1]
    rw_in = rwkv_mu.shape[1]
    assert s % ATT_BLOCK == 0 and s % CHUNK == 0 and l % CHUNK == 0 and rw_w % LANES == 0
    assert kv_w == LANES and WINDOW == ATT_BLOCK

    rows = -(-(b + 1) // SUBLANES) * SUBLANES
    cv = jnp.concatenate([c, c_ctx[None, :], jnp.zeros((rows - b - 1, d), F32)], axis=0)
    mod = _modulation(cv, mod_w, mod_b)
    mod_l = [mod[i, :b].reshape(b, 6, d) for i in range(depth)]
    mod_c0 = mod[0, b].reshape(6, d)

    w_in = ab_w_in[0]
    wq, wk, wv, wr = (w_in[:, :att_w], w_in[:, att_w:att_w + kv_w], w_in[:, att_w + kv_w:att_w + 2 * kv_w],
                      w_in[:, att_w + 2 * kv_w:])
    n_heads = att_w // HEAD_DIM
    perm = jnp.arange(n_heads).reshape(n_heads // ATT_GROUP, ATT_GROUP).T.reshape(-1)
    wq = wq.reshape(d, n_heads, HEAD_DIM)[:, perm].reshape(d, att_w)
    w_lat = jnp.concatenate([wq, _rot_cols(wq), wk, _rot_cols(wk), wv, wr], axis=1).astype(BF16)
    w_ctx = jnp.concatenate([wk, wv, wr], axis=1).astype(BF16)
    cos, sin = _rope_tables(s)
    g0 = norm_g[0, 0].reshape(1, d)
    q, k, v, pr = _project_latent(x, mod_l[0], g0, w_lat, cos, sin, att_w, kv_w, rw_in, _pick_tile(s, PROJ_ROWS))
    kc, vc, prc = _project_ctx(ctx, mod_c0, g0, w_ctx, kv_w, rw_in)
    sink_rows = jnp.repeat(att_sink[0][perm].reshape(att_w // LANES, LANES // HEAD_DIM), ATT_BLOCK, axis=1)
    sink_rows = jnp.broadcast_to(sink_rows[..., None], sink_rows.shape + (LANES,))
    att = _attention(q, k, v, kc, vc, sink_rows)
    w_out = ab_w_out[0]
    w_out = jnp.concatenate([w_out[:att_w].reshape(n_heads, HEAD_DIM, d)[perm].reshape(att_w, d), w_out[att_w:]], axis=0)

    head_id = jnp.arange(rw_w) // HEAD_DIM
    ones_bd = (head_id[:, None] == head_id[None, :]).astype(BF16)
    fp = dict(mu=rwkv_mu[0].reshape(1, rw_in),
              w0=rwkv_w0[0].reshape(1, 2 * rw_w), a0=rwkv_a0[0].reshape(1, 2 * rw_w),
              w2=_block_diag2(rwkv_w2[0]).astype(BF16), a2=_block_diag2(rwkv_a2[0]).astype(BF16),
              g2=rwkv_g2[0].astype(BF16),
              kkp=rwkv_kk[0].reshape(1, rw_w), ka=rwkv_ka[0].reshape(1, rw_w), rk=rwkv_rk[0].reshape(1, rw_w),
              ones=ones_bd)
    npair = rw_w // LANES
    ptc, qtc, rhc, yhc, _, _ = _rwkv_features(prc, fp, _pick_tile(l, FEAT_ROWS))
    zero_state = jnp.zeros((b, 2, npair, CHUNK, LANES), F32)
    _, _, s_ctx = _rwkv_scan(zero_state, ptc, qtc, rhc, yhc, _pick_tile(l // CHUNK, SCAN_CHUNKS))
    pt, qt, rh, yh, bonus, gate = _rwkv_features(pr, fp, _pick_tile(s, FEAT_ROWS))
    yf, yb, _ = _rwkv_scan(s_ctx, pt, qt, rh, yh, _pick_tile(s // CHUNK, SCAN_CHUNKS))

    xl = _mix_out_mlp(x, att, yf, yb, bonus, gate, mod_l[0], norm_g[0],
                      rwkv_ln_w[0].reshape(1, rw_w), rwkv_ln_b[0].reshape(1, rw_w), ones_bd,
                      w_out.astype(BF16), mlp_w1[0].astype(BF16), mlp_w2[0].astype(BF16), _pick_tile(s, MLP_ROWS))

    return _conv_mlp(xl, mod_l[1], norm_g[1], conv_w_in[0].astype(BF16), conv_w[0], conv_w_out[0].astype(BF16),
                     mlp_w1[1].astype(BF16), mlp_w2[1].astype(BF16), _pick_tile(s, MLP_ROWS))
```

```python
import functools
import math

import jax
import jax.numpy as jnp
from jax import lax
from jax.experimental import pallas as pl
from jax.experimental.pallas import tpu as pltpu

F32 = jnp.float32
BF16 = jnp.bfloat16

HEAD_DIM = 64
GRID_W = 64
WINDOW = 128
ATT_BLOCK = 128
ATT_GROUP = 4
ROPE_BASE = 10000.0
NORM_EPS = 1e-6
GN_EPS = 64e-5
LANES = 128
SUBLANES = 8
CHUNK = 64
DECAY_SCALE = math.exp(-0.5)
PROJ_ROWS = 1024
FEAT_ROWS = 256
FEAT_CHUNK_GROUP = 2
SCAN_CHUNKS = 8
ATT_SUBBLOCKS = 16
MLP_ROWS = 512
MLP_HIDDEN_BLOCK = 2048
VMEM_LIMIT = 56 * 1024 * 1024


def _cparams(*sem):
    return pltpu.CompilerParams(dimension_semantics=sem, vmem_limit_bytes=VMEM_LIMIT)


def _const_spec(shape):
    nd = len(shape)
    return pl.BlockSpec(shape, lambda *_: (0,) * nd, pipeline_mode=pl.Buffered(1))


def _dot(a, b):
    return jnp.dot(a.astype(BF16), b.astype(BF16), preferred_element_type=F32)


def _dot_nt(a, b):
    return lax.dot_general(a.astype(BF16), b.astype(BF16), (((1,), (1,)), ((), ())), preferred_element_type=F32)


def _dot_tn(a, b):
    return lax.dot_general(a.astype(BF16), b.astype(BF16), (((0,), (0,)), ((), ())), preferred_element_type=F32)


def _split2(x):
    hi = x.astype(BF16)
    lo = (x - hi.astype(F32)).astype(BF16)
    return hi, lo


def _head_sum(x, ones_bd):
    hi, lo = _split2(x)
    return (jnp.dot(hi, ones_bd, preferred_element_type=F32)
            + jnp.dot(lo, ones_bd, preferred_element_type=F32))


def _rms(u, g):
    return u * lax.rsqrt(jnp.mean(u * u, axis=-1, keepdims=True) + NORM_EPS) * g


def _sigmoid(z):
    return 1.0 / (1.0 + jnp.exp(-z))


def _mod_kernel(cv_ref, w_ref, b_ref, o_ref):
    cv = cv_ref[...]
    s = cv * _sigmoid(cv)
    o_ref[...] = _dot(s, w_ref[...]) + b_ref[...]


def _modulation(cv, mod_w, mod_b):
    depth, d, six_d = mod_w.shape
    rows = cv.shape[0]
    nj = six_d // d
    return pl.pallas_call(
        _mod_kernel,
        grid=(depth, nj),
        in_specs=[pl.BlockSpec((rows, d), lambda l, j: (0, 0)),
                  pl.BlockSpec((None, d, d), lambda l, j: (l, 0, j)),
                  pl.BlockSpec((None, 1, d), lambda l, j: (l, 0, j))],
        out_specs=pl.BlockSpec((None, rows, d), lambda l, j: (l, 0, j)),
        out_shape=jax.ShapeDtypeStruct((depth, rows, six_d), F32),
        compiler_params=_cparams("arbitrary", "arbitrary"),
    )(cv, mod_w, mod_b.reshape(depth, 1, six_d))


def _proj_kernel(x_ref, mod_ref, g_ref, w_ref, cos_ref, sin_ref, q_ref, k_ref, v_ref, pr_ref, *, att_w, kv_w):
    m = mod_ref[...]
    h = (_rms(x_ref[...], g_ref[...] * (1.0 + m[1:2])) + m[0:1]).astype(BF16)
    cos, sin = cos_ref[...], sin_ref[...]
    scale = HEAD_DIM ** -0.5
    p = jnp.dot(h, w_ref[...], preferred_element_type=F32)
    o = 0
    for j in range(att_w // LANES):
        u = p[:, o + j * LANES:o + (j + 1) * LANES]
        ur = p[:, o + att_w + j * LANES:o + att_w + (j + 1) * LANES]
        q_ref[:, j * LANES:(j + 1) * LANES] = ((u * cos + ur * sin) * scale).astype(q_ref.dtype)
    o += 2 * att_w
    for j in range(kv_w // LANES):
        u = p[:, o + j * LANES:o + (j + 1) * LANES]
        ur = p[:, o + kv_w + j * LANES:o + kv_w + (j + 1) * LANES]
        k_ref[:, j * LANES:(j + 1) * LANES] = (u * cos + ur * sin).astype(k_ref.dtype)
    o += 2 * kv_w
    v_ref[...] = p[:, o:o + kv_w].astype(v_ref.dtype)
    o += kv_w
    pr_ref[...] = p[:, o:]


def _proj_ctx_kernel(x_ref, mod_ref, g_ref, w_ref, k_ref, v_ref, pr_ref, *, kv_w):
    m = mod_ref[...]
    h = (_rms(x_ref[...], g_ref[...] * (1.0 + m[1:2])) + m[0:1]).astype(BF16)
    k_ref[...] = jnp.dot(h, w_ref[:, 0:kv_w], preferred_element_type=F32).astype(k_ref.dtype)
    v_ref[...] = jnp.dot(h, w_ref[:, kv_w:2 * kv_w], preferred_element_type=F32).astype(v_ref.dtype)
    pr_ref[...] = jnp.dot(h, w_ref[:, 2 * kv_w:], preferred_element_type=F32)


def _rot_cols(w):
    d, n = w.shape
    m = HEAD_DIM // 4
    w4 = w.reshape(d, n // (2 * m), 2, m)
    return jnp.stack([-w4[:, :, 1], w4[:, :, 0]], axis=2).reshape(d, n)


def _rope_tables(seq):
    m = HEAD_DIM // 4
    t = jnp.arange(seq)
    inv = ROPE_BASE ** (-jnp.arange(m, dtype=F32) / m)
    ang_r = (t // GRID_W).astype(F32)[:, None] * inv[None, :]
    ang_c = (t % GRID_W).astype(F32)[:, None] * inv[None, :]
    ang = jnp.concatenate([ang_r, ang_r, ang_c, ang_c], axis=-1)
    ang = jnp.tile(ang, (1, LANES // HEAD_DIM))
    return jnp.cos(ang), jnp.sin(ang)


def _project_latent(x, mod_l, g, w_all, cos, sin, att_w, kv_w, rw_w, tm):
    b, s, d = x.shape
    n_all = w_all.shape[1]
    kern = functools.partial(_proj_kernel, att_w=att_w, kv_w=kv_w)
    return pl.pallas_call(
        kern,
        grid=(b, s // tm),
        in_specs=[pl.BlockSpec((None, tm, d), lambda bi, i: (bi, i, 0)),
                  pl.BlockSpec((None, 6, d), lambda bi, i: (bi, 0, 0)),
                  _const_spec((1, d)),
                  _const_spec((d, n_all)),
                  pl.BlockSpec((tm, LANES), lambda bi, i: (i, 0)),
                  pl.BlockSpec((tm, LANES), lambda bi, i: (i, 0))],
        out_specs=[pl.BlockSpec((None, tm, att_w), lambda bi, i: (bi, i, 0)),
                   pl.BlockSpec((None, tm, kv_w), lambda bi, i: (bi, i, 0)),
                   pl.BlockSpec((None, tm, kv_w), lambda bi, i: (bi, i, 0)),
                   pl.BlockSpec((None, tm, rw_w), lambda bi, i: (bi, i, 0))],
        out_shape=[jax.ShapeDtypeStruct((b, s, att_w), BF16),
                   jax.ShapeDtypeStruct((b, s, kv_w), BF16),
                   jax.ShapeDtypeStruct((b, s, kv_w), BF16),
                   jax.ShapeDtypeStruct((b, s, rw_w), F32)],
        compiler_params=_cparams("parallel", "parallel"),
    )(x, mod_l, g, w_all, cos, sin)


def _project_ctx(ctx, mod_c, g, w_ctx, kv_w, rw_w):
    b, l, d = ctx.shape
    kern = functools.partial(_proj_ctx_kernel, kv_w=kv_w)
    return pl.pallas_call(
        kern,
        grid=(b,),
        in_specs=[pl.BlockSpec((None, l, d), lambda bi: (bi, 0, 0)),
                  _const_spec((6, d)),
                  _const_spec((1, d)),
                  _const_spec((d, w_ctx.shape[1]))],
        out_specs=[pl.BlockSpec((None, l, kv_w), lambda bi: (bi, 0, 0)),
                   pl.BlockSpec((None, l, kv_w), lambda bi: (bi, 0, 0)),
                   pl.BlockSpec((None, l, rw_w), lambda bi: (bi, 0, 0))],
        out_shape=[jax.ShapeDtypeStruct((b, l, kv_w), BF16),
                   jax.ShapeDtypeStruct((b, l, kv_w), BF16),
                   jax.ShapeDtypeStruct((b, l, rw_w), F32)],
        compiler_params=_cparams("parallel"),
    )(ctx, mod_c, g, w_ctx)


def _attn_kernel(sink_ref, q_ref, kp_ref, kc_ref, kn_ref, vp_ref, vc_ref, vn_ref, kx_ref, vx_ref, o_ref, *, nsub):
    i = pl.program_id(1)
    nb = pl.num_programs(1)
    blk = ATT_BLOCK
    kall = jnp.concatenate([kp_ref[...], kc_ref[...], kn_ref[...]], axis=0)
    vall = jnp.concatenate([vp_ref[...], vc_ref[...], vn_ref[...]], axis=0)
    vall = jnp.concatenate([vall, jnp.ones_like(vall)], axis=1)
    kctx = kx_ref[...]
    vctx = jnp.concatenate([vx_ref[...], jnp.ones_like(vx_ref[...])], axis=1)
    row = lax.broadcasted_iota(jnp.int32, (2 * blk, 3 * blk), 0) % blk
    col = lax.broadcasted_iota(jnp.int32, (2 * blk, 3 * blk), 1)
    band = jnp.abs(row + blk - col) <= WINDOW
    even = lax.broadcasted_iota(jnp.int32, (blk, LANES), 1) < HEAD_DIM
    zero = jnp.zeros((), q_ref.dtype)
    tiles = range(q_ref.shape[1] // LANES)
    sinks = [sink_ref[j] for j in tiles]

    def band_masked(s, va):
        return jnp.concatenate([jnp.where(va[:, :blk], s[:, :blk], -1e30), s[:, blk:2 * blk],
                                jnp.where(va[:, 2 * blk:], s[:, 2 * blk:], -1e30)], axis=1)

    def scores(a):
        va = band
        if a == 0:
            va = va & jnp.logical_not((i == 0) & (col < blk))
        if a == nsub - 1:
            va = va & jnp.logical_not((i == nb - 1) & (col >= 2 * blk))
        kloc = kall[a * blk:(a + 3) * blk]
        qs = []
        for j in tiles:
            qj = q_ref[a * blk:(a + 1) * blk, j * LANES:(j + 1) * LANES]
            qs.append(jnp.concatenate([jnp.where(even, qj, zero), jnp.where(even, zero, qj)], axis=0))
        return [band_masked(_dot_nt(qj, kloc), va) for qj in qs], [_dot_nt(qj, kctx) for qj in qs]

    def row_max(sl, sc, sk):
        t = sk
        for s in (sl, sc):
            for c0 in range(0, s.shape[1], LANES):
                t = jnp.maximum(t, s[:, c0:c0 + LANES])
        return jnp.max(t, axis=-1, keepdims=True)

    def finish(a, s_loc, s_ctx):
        vloc = vall[a * blk:(a + 3) * blk]
        mx = [row_max(sl, sc, sk) for sl, sc, sk in zip(s_loc, s_ctx, sinks)]
        p_loc = [jnp.exp((sl - m).astype(BF16)) for sl, m in zip(s_loc, mx)]
        p_ctx = [jnp.exp((sc - m).astype(BF16)) for sc, m in zip(s_ctx, mx)]
        num = [jnp.dot(pa, vloc, preferred_element_type=F32) + jnp.dot(pc, vctx, preferred_element_type=F32)
               for pa, pc in zip(p_loc, p_ctx)]
        for j, n, sk, m in zip(tiles, num, sinks, mx):
            o = n[:, :LANES] / (n[:, LANES:] + jnp.exp(sk - m))
            o_ref[a * blk:(a + 1) * blk, j * LANES:(j + 1) * LANES] = jnp.where(even, o[:blk], o[blk:]).astype(o_ref.dtype)

    pending = scores(0)
    for a in range(nsub):
        ahead = scores(a + 1) if a + 1 < nsub else None
        finish(a, *pending)
        pending = ahead


def _attention(q, k, v, kc, vc, sink_rows):
    b, s, att_w = q.shape
    kv_w = k.shape[2]
    l = kc.shape[1]
    blk = ATT_BLOCK
    nsub = _pick_tile(s // blk, ATT_SUBBLOCKS)
    nb = s // (nsub * blk)
    kern = functools.partial(_attn_kernel, nsub=nsub)
    kv_prev = pl.BlockSpec((None, blk, kv_w), lambda bi, i: (bi, jnp.maximum(i * nsub - 1, 0), 0))
    kv_cur = pl.BlockSpec((None, nsub * blk, kv_w), lambda bi, i: (bi, i, 0))
    kv_next = pl.BlockSpec((None, blk, kv_w), lambda bi, i: (bi, jnp.minimum((i + 1) * nsub, nb * nsub - 1), 0))
    kv_ctx = pl.BlockSpec((None, l, kv_w), lambda bi, i: (bi, 0, 0))
    return pl.pallas_call(
        kern,
        grid=(b, nb),
        in_specs=[_const_spec(sink_rows.shape),
                  pl.BlockSpec((None, nsub * blk, att_w), lambda bi, i: (bi, i, 0)),
                  kv_prev, kv_cur, kv_next, kv_prev, kv_cur, kv_next, kv_ctx, kv_ctx],
        out_specs=pl.BlockSpec((None, nsub * blk, att_w), lambda bi, i: (bi, i, 0)),
        out_shape=jax.ShapeDtypeStruct((b, s, att_w), BF16),
        compiler_params=_cparams("parallel", "parallel"),
    )(sink_rows, q, k, k, k, v, v, v, kc, vc)


def _stack_heads(x, even):
    return jnp.concatenate([jnp.where(even, x, 0.0), jnp.where(even, 0.0, x)], axis=0)


def _chunk_summaries(insts):
    c2 = 2 * CHUNK
    even = lax.broadcasted_iota(jnp.int32, (CHUNK, LANES), 1) < HEAD_DIM
    rt = lax.broadcasted_iota(jnp.int32, (CHUNK, LANES), 0)
    ct = lax.broadcasted_iota(jnp.int32, (CHUNK, LANES), 1) % HEAD_DIM
    masks = {True: (ct < rt, ct <= rt), False: (ct > rt, ct >= rt)}
    eye = lax.broadcasted_iota(jnp.int32, (LANES, LANES), 0) == lax.broadcasted_iota(jnp.int32, (LANES, LANES), 1)

    zero = jnp.zeros((), BF16)

    def stacked(f):
        f = f.astype(BF16)
        return jnp.concatenate([jnp.where(even, f, zero), jnp.where(even, zero, f)], axis=0)

    prep = []
    for rr, vv, aa, bb, kd, cs, lw, tot, forward in insts:
        g_inv = jnp.exp(-cs)
        g_rem = jnp.exp(tot - cs)
        a_t = aa * jnp.exp(cs - lw)
        r_t = rr * jnp.exp(cs)
        v_s = stacked(vv)
        bk_s = jnp.concatenate([stacked(bb * g_rem), stacked(kd * g_rem)], axis=0)
        lhs = jnp.concatenate([a_t, r_t], axis=0).astype(BF16)
        rhs = jnp.concatenate([stacked(bb * g_inv), stacked(kd * g_inv)], axis=0)
        prep.append((_stack_heads(a_t, even), _stack_heads(r_t, even), v_s, bk_s, lhs, rhs, jnp.exp(tot), masks[forward]))

    gs = [_dot_nt(p[4], p[5]) for p in prep]
    tri, lfs = [], []
    for g, p in zip(gs, prep):
        strict, incl = p[7]
        lfs.append(jnp.where(strict, g[:CHUNK, :c2], 0.0).astype(BF16))
        tri.append((stacked(jnp.where(strict, g[:CHUNK, c2:], 0.0)), stacked(jnp.where(incl, g[CHUNK:, :c2], 0.0)),
                    stacked(jnp.where(incl, g[CHUNK:, c2:], 0.0))))
    own = (lax.broadcasted_iota(jnp.int32, (c2, LANES), 0) < CHUNK) == (
        lax.broadcasted_iota(jnp.int32, (c2, LANES), 1) < HEAD_DIM)
    xs = [p[0] + pltpu.roll(_dot(t[0], p[2]), HEAD_DIM, 1) for p, t in zip(prep, tri)]

    ts = [jnp.where((rt // 2 == ct // 2), lf.astype(F32), 0.0) + jnp.where(rt == ct, 1.0, 0.0) for lf in lfs]
    size = 2
    while size < CHUNK:
        couple = (rt // (2 * size) == ct // (2 * size)) & (rt // size != ct // size)
        os_ = [jnp.where(couple, lf, zero) for lf in lfs]
        ots = [_dot(o, stacked(t)) for o, t in zip(os_, ts)]
        ts = [t + _dot(t, stacked(ot)) for t, ot in zip(ts, ots)]
        size *= 2
    xs = [_dot(stacked(t), x) for t, x in zip(ts, xs)]
    rys = [p[1] + pltpu.roll(_dot(t[2], p[2]), HEAD_DIM, 1) + _dot(t[1], x)
           for p, t, x in zip(prep, tri, xs)]
    out = []
    for p, x, ry in zip(prep, xs, rys):
        v_s, bk_s, g_tot = p[2], p[3], p[6]
        x_a = jnp.where(own, x, 0.0).astype(BF16)
        x_u = pltpu.roll(jnp.where(own, 0.0, x), HEAD_DIM, 1).astype(BF16)
        pt = jnp.where(eye, g_tot, 0.0) + _dot_tn(bk_s[:c2], x_a)
        qt = _dot_tn(bk_s, jnp.concatenate([x_u, v_s], axis=0))
        rh = jnp.where(even, ry[:CHUNK], ry[CHUNK:])
        yh = pltpu.roll(jnp.where(even, ry[CHUNK:], ry[:CHUNK]), HEAD_DIM, 1)
        out.append((pt[:CHUNK] + pt[CHUNK:], qt[:CHUNK] + qt[CHUNK:], rh, yh))
    return out


def _feat_kernel(pr_ref, hp_ref, hn_ref, mu_ref, w0_ref, a0_ref, w2_ref, a2_ref, g2_ref, kkp_ref, ka_ref, rk_ref,
                 ones_ref, pt_ref, qt_ref, rh_ref, yh_ref, bonus_ref, gate_ref, *, rw_w):
    i = pl.program_id(1)
    last = pl.num_programs(1) - 1
    tt = pr_ref.shape[0]
    pr = pr_ref[...]
    ridx = lax.broadcasted_iota(jnp.int32, (tt, 1), 0)
    edge_prev = jnp.where(i == 0, 0.0, hp_ref[SUBLANES - 1:SUBLANES, :])
    edge_next = jnp.where(i == last, 0.0, hn_ref[0:1, :])
    prev = jnp.where(ridx == 0, edge_prev, pltpu.roll(pr, 1, 0))
    nxt = jnp.where(ridx == tt - 1, edge_next, pltpu.roll(pr, tt - 1, 0))
    x = pr + mu_ref[...] * (0.5 * (prev + nxt) - pr)

    r = x[:, 0:rw_w]
    k = x[:, rw_w:2 * rw_w]
    v = x[:, 2 * rw_w:3 * rw_w]
    o = 3 * rw_w
    wd = x[:, o:o + LANES]
    ad = x[:, o + LANES:o + 2 * LANES]
    gd = x[:, o + 2 * LANES:o + 3 * LANES]

    ones_bd = ones_ref[...]
    logw = -DECAY_SCALE * _sigmoid(_dot(jnp.tanh(wd), w2_ref[...]) + w0_ref[...])
    iclr = _sigmoid(_dot(ad, a2_ref[...]) + a0_ref[...])
    gate_ref[...] = _dot(_sigmoid(gd), g2_ref[...])

    kkv = k * kkp_ref[...]
    kk_sq = jnp.dot((kkv * kkv).astype(BF16), ones_bd, preferred_element_type=F32)
    kk = kkv / jnp.maximum(jnp.sqrt(kk_sq), 1e-12)
    ka = ka_ref[...]
    k_dir = [k * (1.0 + (iclr[:, d * rw_w:(d + 1) * rw_w] - 1.0) * ka) for d in range(2)]
    bonus_ref[...] = _head_sum(r * (k_dir[0] + k_dir[1]) * rk_ref[...], ones_bd) * v

    row = lax.broadcasted_iota(jnp.int32, (tt, tt), 0)
    col = lax.broadcasted_iota(jnp.int32, (tt, tt), 1)
    same = (row // CHUNK) == (col // CHUNK)
    tri = [jnp.where(same & (col <= row), 1.0, 0.0).astype(BF16),
           jnp.where(same & (col >= row), 1.0, 0.0).astype(BF16)]
    neg_kk = -kk
    lw_dir, cs_dir, b_dir = [], [], []
    for d in range(2):
        lw_d = logw[:, d * rw_w:(d + 1) * rw_w]
        lw_dir.append(lw_d)
        cs_dir.append(sum(jnp.dot(tri[d], p, preferred_element_type=F32) for p in _split2(lw_d)))
        b_dir.append(kk * iclr[:, d * rw_w:(d + 1) * rw_w])
    nchunks = tt // CHUNK
    group = _pick_tile(nchunks, FEAT_CHUNK_GROUP)
    for c0 in range(0, nchunks, group):
        keys, insts = [], []
        for c in range(c0, c0 + group):
            rows = slice(c * CHUNK, (c + 1) * CHUNK)
            for d in range(2):
                end = (c + 1) * CHUNK - 1 if d == 0 else c * CHUNK
                for p in range(rw_w // LANES):
                    ln = slice(p * LANES, (p + 1) * LANES)
                    keys.append((c, d, p, rows, ln))
                    insts.append((r[rows, ln], v[rows, ln], neg_kk[rows, ln], b_dir[d][rows, ln], k_dir[d][rows, ln],
                                  cs_dir[d][rows, ln], lw_dir[d][rows, ln], cs_dir[d][end:end + 1, ln], d == 0))
        for (c, d, p, rows, ln), (pt, qt, rh, yh) in zip(keys, _chunk_summaries(insts)):
            pt_ref[c, d, p] = pt
            qt_ref[c, d, p] = qt
            rh_ref[d, rows, ln] = rh.astype(rh_ref.dtype)
            yh_ref[d, rows, ln] = yh.astype(yh_ref.dtype)


def _rwkv_features(pr, fp, tt):
    b, t, w_all = pr.shape
    rw_w = fp["kkp"].shape[1]
    npair = rw_w // LANES
    nt = t // tt
    cpt = tt // CHUNK
    hb = tt // SUBLANES
    kern = functools.partial(_feat_kernel, rw_w=rw_w)
    names = ("mu", "w0", "a0", "w2", "a2", "g2", "kkp", "ka", "rk", "ones")
    return pl.pallas_call(
        kern,
        grid=(b, nt),
        in_specs=[pl.BlockSpec((None, tt, w_all), lambda bi, i: (bi, i, 0)),
                  pl.BlockSpec((None, SUBLANES, w_all), lambda bi, i: (bi, jnp.maximum(i * hb - 1, 0), 0)),
                  pl.BlockSpec((None, SUBLANES, w_all), lambda bi, i: (bi, jnp.minimum((i + 1) * hb, nt * hb - 1), 0))]
                 + [_const_spec(fp[n].shape) for n in names],
        out_specs=[pl.BlockSpec((None, cpt, 2, npair, CHUNK, LANES), lambda bi, i: (bi, i, 0, 0, 0, 0)),
                   pl.BlockSpec((None, cpt, 2, npair, CHUNK, LANES), lambda bi, i: (bi, i, 0, 0, 0, 0)),
                   pl.BlockSpec((None, 2, tt, rw_w), lambda bi, i: (bi, 0, i, 0)),
                   pl.BlockSpec((None, 2, tt, rw_w), lambda bi, i: (bi, 0, i, 0)),
                   pl.BlockSpec((None, tt, rw_w), lambda bi, i: (bi, i, 0)),
                   pl.BlockSpec((None, tt, rw_w), lambda bi, i: (bi, i, 0))],
        out_shape=[jax.ShapeDtypeStruct((b, t // CHUNK, 2, npair, CHUNK, LANES), F32),
                   jax.ShapeDtypeStruct((b, t // CHUNK, 2, npair, CHUNK, LANES), F32),
                   jax.ShapeDtypeStruct((b, 2, t, rw_w), BF16),
                   jax.ShapeDtypeStruct((b, 2, t, rw_w), BF16),
                   jax.ShapeDtypeStruct((b, t, rw_w), F32),
                   jax.ShapeDtypeStruct((b, t, rw_w), F32)],
        compiler_params=_cparams("parallel", "parallel"),
    )(pr, pr, pr, *[fp[n] for n in names])


def _scan_kernel(s0_ref, ptf_ref, qtf_ref, ptb_ref, qtb_ref, rhf_ref, yhf_ref, rhb_ref, yhb_ref,
                 yf_ref, yb_ref, sfin_ref, st_ref, *, cps, npair):
    i = pl.program_id(1)

    @pl.when(i == 0)
    def _():
        st_ref[...] = s0_ref[...]

    even = lax.broadcasted_iota(jnp.int32, (CHUNK, LANES), 1) < HEAD_DIM
    zero = jnp.zeros((), BF16)
    dirs = ((ptf_ref, qtf_ref, rhf_ref, yhf_ref, yf_ref), (ptb_ref, qtb_ref, rhb_ref, yhb_ref, yb_ref))
    keys = [(d, p) for d in range(2) for p in range(npair)]
    st = [st_ref[d, p] for d, p in keys]
    for step in range(cps):
        hl = []
        for s in st:
            hi = s.astype(BF16)
            lo = (s - hi.astype(F32)).astype(BF16)
            hl.append(jnp.concatenate([jnp.concatenate([jnp.where(even, hi, zero), jnp.where(even, zero, hi)], axis=0),
                                       jnp.concatenate([jnp.where(even, lo, zero), jnp.where(even, zero, lo)], axis=0)],
                                      axis=1))
        new = []
        for (d, p), s2 in zip(keys, hl):
            pt_ref, qt_ref, rh_ref, yh_ref, y_ref = dirs[d]
            c = step if d == 0 else cps - 1 - step
            rows = slice(c * CHUNK, (c + 1) * CHUNK)
            ln = slice(p * LANES, (p + 1) * LANES)
            y = jnp.dot(rh_ref[rows, ln], s2[:, :LANES], preferred_element_type=F32) + yh_ref[rows, ln].astype(F32)
            y_ref[rows, ln] = y.astype(y_ref.dtype)
            pt_hi, pt_lo = _split2(pt_ref[c, p])
            n2 = jnp.dot(pt_hi, s2, preferred_element_type=F32)
            new.append(n2[:, :LANES] + n2[:, LANES:] + jnp.dot(pt_lo, s2[:, :LANES], preferred_element_type=F32)
                       + qt_ref[c, p])
        st = new
    for (d, p), s in zip(keys, st):
        st_ref[d, p] = s

    @pl.when(i == pl.num_programs(1) - 1)
    def _():
        sfin_ref[...] = st_ref[...]


def _rwkv_scan(s0, pt, qt, rh, yh, cps):
    b, nc, _, npair, _, _ = pt.shape
    t, rw_w = rh.shape[2], rh.shape[3]
    ns = nc // cps
    ts = cps * CHUNK
    kern = functools.partial(_scan_kernel, cps=cps, npair=npair)
    mat_f = pl.BlockSpec((None, cps, None, npair, CHUNK, LANES), lambda bi, i: (bi, i, 0, 0, 0, 0))
    mat_b = pl.BlockSpec((None, cps, None, npair, CHUNK, LANES), lambda bi, i: (bi, ns - 1 - i, 1, 0, 0, 0))
    tok_f = pl.BlockSpec((None, None, ts, rw_w), lambda bi, i: (bi, 0, i, 0))
    tok_b = pl.BlockSpec((None, None, ts, rw_w), lambda bi, i: (bi, 1, ns - 1 - i, 0))
    state = pl.BlockSpec((None, 2, npair, CHUNK, LANES), lambda bi, i: (bi, 0, 0, 0, 0))
    return pl.pallas_call(
        kern,
        grid=(b, ns),
        in_specs=[state, mat_f, mat_f, mat_b, mat_b, tok_f, tok_f, tok_b, tok_b],
        out_specs=[pl.BlockSpec((None, ts, rw_w), lambda bi, i: (bi, i, 0)),
                   pl.BlockSpec((None, ts, rw_w), lambda bi, i: (bi, ns - 1 - i, 0)),
                   state],
        out_shape=[jax.ShapeDtypeStruct((b, t, rw_w), BF16),
                   jax.ShapeDtypeStruct((b, t, rw_w), BF16),
                   jax.ShapeDtypeStruct((b, 2, npair, CHUNK, LANES), F32)],
        scratch_shapes=[pltpu.VMEM((2, npair, CHUNK, LANES), F32)],
        compiler_params=_cparams("parallel", "arbitrary"),
    )(s0, pt, qt, pt, qt, rh, yh, rh, yh)


def _residual_mlp(xl, yl, m, ng, w1_ref, w2_ref):
    x2 = xl + _rms(yl, m[2:3] * ng[1:2])
    hm = (_rms(x2, ng[2:3] * (1.0 + m[4:5])) + m[3:4]).astype(BF16)
    out = None
    for j in range(w1_ref.shape[1] // MLP_HIDDEN_BLOCK):
        cols = slice(j * MLP_HIDDEN_BLOCK, (j + 1) * MLP_HIDDEN_BLOCK)
        hid = jnp.maximum(jnp.dot(hm, w1_ref[:, cols], preferred_element_type=F32), 0.0)
        part = jnp.dot((hid * hid).astype(BF16), w2_ref[cols, :], preferred_element_type=F32)
        out = part if out is None else out + part
    return x2 + _rms(out, m[5:6] * ng[3:4])


def _mix_out_kernel(x_ref, att_ref, yf_ref, yb_ref, bonus_ref, gate_ref, mod_ref, ng_ref, lnw_ref, lnb_ref, ones_ref,
                    wo_ref, w1_ref, w2_ref, o_ref, *, att_w):
    ones_bd = ones_ref[...]
    y = yf_ref[...].astype(F32) + yb_ref[...].astype(F32)
    mean = _head_sum(y, ones_bd) * (1.0 / HEAD_DIM)
    yc = y - mean
    var = jnp.dot((yc * yc).astype(BF16), ones_bd, preferred_element_type=F32) * (1.0 / HEAD_DIM)
    yn = yc * lax.rsqrt(var + GN_EPS) * lnw_ref[...] + lnb_ref[...]
    rw = (yn + bonus_ref[...]) * gate_ref[...]
    yl = (jnp.dot(att_ref[...], wo_ref[0:att_w, :], preferred_element_type=F32)
          + jnp.dot(rw.astype(BF16), wo_ref[att_w:, :], preferred_element_type=F32))
    o_ref[...] = _residual_mlp(x_ref[...], yl, mod_ref[...], ng_ref[...], w1_ref, w2_ref)


def _mix_out_mlp(x, att, yf, yb, bonus, gate, mod_l, ng, lnw, lnb, ones_bd, wo, w1, w2, tm):
    b, s, d = x.shape
    att_w = att.shape[2]
    rw_w = yf.shape[2]
    kern = functools.partial(_mix_out_kernel, att_w=att_w)
    tok = lambda w: pl.BlockSpec((None, tm, w), lambda bi, i: (bi, i, 0))
    return pl.pallas_call(
        kern,
        grid=(b, s // tm),
        in_specs=[tok(d), tok(att_w), tok(rw_w), tok(rw_w), tok(rw_w), tok(rw_w),
                  pl.BlockSpec((None, 6, d), lambda bi, i: (bi, 0, 0)),
                  _const_spec(ng.shape), _const_spec(lnw.shape), _const_spec(lnb.shape), _const_spec(ones_bd.shape),
                  _const_spec(wo.shape), _const_spec(w1.shape), _const_spec(w2.shape)],
        out_specs=tok(d),
        out_shape=jax.ShapeDtypeStruct((b, s, d), F32),
        compiler_params=_cparams("parallel", "parallel"),
    )(x, att, yf, yb, bonus, gate, mod_l, ng, lnw, lnb, ones_bd, wo, w1, w2)


def _conv_kernel(x_ref, xp_ref, xn_ref, mod_ref, ng_ref, wi_ref, cw_ref, wo_ref, w1_ref, w2_ref, o_ref):
    i = pl.program_id(1)
    last = pl.num_programs(1) - 1
    d = x_ref.shape[1]
    tm = x_ref.shape[0]
    m = mod_ref[...]
    ng = ng_ref[...]

    def modnorm(u):
        return (_rms(u, ng[0:1] * (1.0 + m[1:2])) + m[0:1]).astype(BF16)

    x = x_ref[...]
    rows = jnp.concatenate([x, xp_ref[...], xn_ref[...]], axis=0)
    proj = jnp.dot(modnorm(rows), wi_ref[...], preferred_element_type=F32)
    z_all = proj[:, d:2 * d] * proj[:, 2 * d:]
    z = z_all[:tm]
    zp = jnp.where(i == 0, 0.0, z_all[tm + SUBLANES - 1:tm + SUBLANES, :])
    zn = jnp.where(i == last, 0.0, z_all[tm + SUBLANES:tm + SUBLANES + 1, :])
    ridx = lax.broadcasted_iota(jnp.int32, (tm, 1), 0)
    prev = jnp.where(ridx == 0, zp, pltpu.roll(z, 1, 0))
    nxt = jnp.where(ridx == tm - 1, zn, pltpu.roll(z, tm - 1, 0))
    cw = cw_ref[...]
    y = proj[:tm, :d] * (prev * cw[0:1] + z * cw[1:2] + nxt * cw[2:3])
    yl = jnp.dot(y.astype(BF16), wo_ref[...], preferred_element_type=F32)
    o_ref[...] = _residual_mlp(x, yl, m, ng, w1_ref, w2_ref)


def _conv_mlp(x, mod_l, ng, wi, cw, wo, w1, w2, tm):
    b, s, d = x.shape
    hb = tm // SUBLANES
    nt = s // tm
    return pl.pallas_call(
        _conv_kernel,
        grid=(b, nt),
        in_specs=[pl.BlockSpec((None, tm, d), lambda bi, i: (bi, i, 0)),
                  pl.BlockSpec((None, SUBLANES, d), lambda bi, i: (bi, jnp.maximum(i * hb - 1, 0), 0)),
                  pl.BlockSpec((None, SUBLANES, d), lambda bi, i: (bi, jnp.minimum((i + 1) * hb, nt * hb - 1), 0)),
                  pl.BlockSpec((None, 6, d), lambda bi, i: (bi, 0, 0)),
                  _const_spec(ng.shape), _const_spec(wi.shape), _const_spec(cw.shape), _const_spec(wo.shape),
                  _const_spec(w1.shape), _const_spec(w2.shape)],
        out_specs=pl.BlockSpec((None, tm, d), lambda bi, i: (bi, i, 0)),
        out_shape=jax.ShapeDtypeStruct((b, s, d), F32),
        compiler_params=_cparams("parallel", "parallel"),
    )(x, x, x, mod_l, ng, wi, cw, wo, w1, w2)


def _block_diag2(m):
    z = jnp.zeros_like(m[0])
    return jnp.concatenate([jnp.concatenate([m[0], z], axis=1), jnp.concatenate([z, m[1]], axis=1)], axis=0)


def _pick_tile(n, want):
    t = min(n, want)
    while n % t:
        t //= 2
    return t


def kernel(x, c, ctx, c_ctx, mod_w, mod_b, norm_g, mlp_w1, mlp_w2, ab_w_in, ab_w_out, att_sink, rwkv_mu, rwkv_w0,
           rwkv_w2, rwkv_a0, rwkv_a2, rwkv_g2, rwkv_kk, rwkv_ka, rwkv_rk, rwkv_ln_w, rwkv_ln_b, conv_w_in, conv_w,
           conv_w_out):
    b, s, d = x.shape
    l = ctx.shape[1]
    depth = mod_w.shape[0]
    assert depth == 2, "layer schedule below is written for one attention/RWKV layer followed by one conv layer"
    att_w = att_sink.shape[1] * HEAD_DIM
    kv_w = att_w // ATT_GROUP
    rw_w = rwkv_kk.shape[1]
    rw_in = rwkv_mu.shape[1]
    assert s % ATT_BLOCK == 0 and s % CHUNK == 0 and l % CHUNK == 0 and rw_w % LANES == 0
    assert kv_w == LANES and WINDOW == ATT_BLOCK

    rows = -(-(b + 1) // SUBLANES) * SUBLANES
    cv = jnp.concatenate([c, c_ctx[None, :], jnp.zeros((rows - b - 1, d), F32)], axis=0)
    mod = _modulation(cv, mod_w, mod_b)
    mod_l = [mod[i, :b].reshape(b, 6, d) for i in range(depth)]
    mod_c0 = mod[0, b].reshape(6, d)

    w_in = ab_w_in[0]
    wq, wk, wv, wr = (w_in[:, :att_w], w_in[:, att_w:att_w + kv_w], w_in[:, att_w + kv_w:att_w + 2 * kv_w],
                      w_in[:, att_w + 2 * kv_w:])
    n_heads = att_w // HEAD_DIM
    perm = jnp.arange(n_heads).reshape(n_heads // ATT_GROUP, ATT_GROUP).T.reshape(-1)
    wq = wq.reshape(d, n_heads, HEAD_DIM)[:, perm].reshape(d, att_w)
    w_lat = jnp.concatenate([wq, _rot_cols(wq), wk, _rot_cols(wk), wv, wr], axis=1).astype(BF16)
    w_ctx = jnp.concatenate([wk, wv, wr], axis=1).astype(BF16)
    cos, sin = _rope_tables(s)
    g0 = norm_g[0, 0].reshape(1, d)
    q, k, v, pr = _project_latent(x, mod_l[0], g0, w_lat, cos, sin, att_w, kv_w, rw_in, _pick_tile(s, PROJ_ROWS))
    kc, vc, prc = _project_ctx(ctx, mod_c0, g0, w_ctx, kv_w, rw_in)
    sink_rows = jnp.repeat(att_sink[0][perm].reshape(att_w // LANES, LANES // HEAD_DIM), ATT_BLOCK, axis=1)
    sink_rows = jnp.broadcast_to(sink_rows[..., None], sink_rows.shape + (LANES,))
    att = _attention(q, k, v, kc, vc, sink_rows)
    w_out = ab_w_out[0]
    w_out = jnp.concatenate([w_out[:att_w].reshape(n_heads, HEAD_DIM, d)[perm].reshape(att_w, d), w_out[att_w:]], axis=0)

    head_id = jnp.arange(rw_w) // HEAD_DIM
    ones_bd = (head_id[:, None] == head_id[None, :]).astype(BF16)
    fp = dict(mu=rwkv_mu[0].reshape(1, rw_in),
              w0=rwkv_w0[0].reshape(1, 2 * rw_w), a0=rwkv_a0[0].reshape(1, 2 * rw_w),
              w2=_block_diag2(rwkv_w2[0]).astype(BF16), a2=_block_diag2(rwkv_a2[0]).astype(BF16),
              g2=rwkv_g2[0].astype(BF16),
              kkp=rwkv_kk[0].reshape(1, rw_w), ka=rwkv_ka[0].reshape(1, rw_w), rk=rwkv_rk[0].reshape(1, rw_w),
              ones=ones_bd)
    npair = rw_w // LANES
    ptc, qtc, rhc, yhc, _, _ = _rwkv_features(prc, fp, _pick_tile(l, FEAT_ROWS))
    zero_state = jnp.zeros((b, 2, npair, CHUNK, LANES), F32)
    _, _, s_ctx = _rwkv_scan(zero_state, ptc, qtc, rhc, yhc, _pick_tile(l // CHUNK, SCAN_CHUNKS))
    pt, qt, rh, yh, bonus, gate = _rwkv_features(pr, fp, _pick_tile(s, FEAT_ROWS))
    yf, yb, _ = _rwkv_scan(s_ctx, pt, qt, rh, yh, _pick_tile(s // CHUNK, SCAN_CHUNKS))

    xl = _mix_out_mlp(x, att, yf, yb, bonus, gate, mod_l[0], norm_g[0],
                      rwkv_ln_w[0].reshape(1, rw_w), rwkv_ln_b[0].reshape(1, rw_w), ones_bd,
                      w_out.astype(BF16), mlp_w1[0].astype(BF16), mlp_w2[0].astype(BF16), _pick_tile(s, MLP_ROWS))

    return _conv_mlp(xl, mod_l[1], norm_g[1], conv_w_in[0].astype(BF16), conv_w[0], conv_w_out[0].astype(BF16),
                     mlp_w1[1].astype(BF16), mlp_w2[1].astype(BF16), _pick_tile(s, MLP_ROWS))
```

```python
import functools
import math

import jax
import jax.numpy as jnp
from jax import lax
from jax.experimental import pallas as pl
from jax.experimental.pallas import tpu as pltpu

F32 = jnp.float32
BF16 = jnp.bfloat16

HEAD_DIM = 64
GRID_W = 64
WINDOW = 128
ATT_BLOCK = 128
ATT_GROUP = 4
ROPE_BASE = 10000.0
NORM_EPS = 1e-6
GN_EPS = 64e-5
LANES = 128
SUBLANES = 8
CHUNK = 64
DECAY_SCALE = math.exp(-0.5)
PROJ_ROWS = 1024
FEAT_ROWS = 256
FEAT_CHUNK_GROUP = 2
SCAN_CHUNKS = 16
ATT_SUBBLOCKS = 16
MLP_ROWS = 512
MLP_HIDDEN_BLOCK = 2048
VMEM_LIMIT = 56 * 1024 * 1024


def _cparams(*sem):
    return pltpu.CompilerParams(dimension_semantics=sem, vmem_limit_bytes=VMEM_LIMIT)


def _const_spec(shape):
    nd = len(shape)
    return pl.BlockSpec(shape, lambda *_: (0,) * nd, pipeline_mode=pl.Buffered(1))


def _dot(a, b):
    return jnp.dot(a.astype(BF16), b.astype(BF16), preferred_element_type=F32)


def _dot_nt(a, b):
    return lax.dot_general(a.astype(BF16), b.astype(BF16), (((1,), (1,)), ((), ())), preferred_element_type=F32)


def _dot_tn(a, b):
    return lax.dot_general(a.astype(BF16), b.astype(BF16), (((0,), (0,)), ((), ())), preferred_element_type=F32)


def _split2(x):
    hi = x.astype(BF16)
    lo = (x - hi.astype(F32)).astype(BF16)
    return hi, lo


def _head_sum(x, ones_bd):
    hi, lo = _split2(x)
    return (jnp.dot(hi, ones_bd, preferred_element_type=F32)
            + jnp.dot(lo, ones_bd, preferred_element_type=F32))


def _rms(u, g):
    return u * lax.rsqrt(jnp.mean(u * u, axis=-1, keepdims=True) + NORM_EPS) * g


def _sigmoid(z):
    return 1.0 / (1.0 + jnp.exp(-z))


def _mod_kernel(cv_ref, w_ref, b_ref, o_ref):
    cv = cv_ref[...]
    s = cv * _sigmoid(cv)
    o_ref[...] = _dot(s, w_ref[...]) + b_ref[...]


def _modulation(cv, mod_w, mod_b):
    depth, d, six_d = mod_w.shape
    rows = cv.shape[0]
    nj = six_d // d
    return pl.pallas_call(
        _mod_kernel,
        grid=(depth, nj),
        in_specs=[pl.BlockSpec((rows, d), lambda l, j: (0, 0)),
                  pl.BlockSpec((None, d, d), lambda l, j: (l, 0, j)),
                  pl.BlockSpec((None, 1, d), lambda l, j: (l, 0, j))],
        out_specs=pl.BlockSpec((None, rows, d), lambda l, j: (l, 0, j)),
        out_shape=jax.ShapeDtypeStruct((depth, rows, six_d), F32),
        compiler_params=_cparams("arbitrary", "arbitrary"),
    )(cv, mod_w, mod_b.reshape(depth, 1, six_d))


def _proj_kernel(x_ref, mod_ref, g_ref, w_ref, cos_ref, sin_ref, q_ref, k_ref, v_ref, pr_ref, *, att_w, kv_w):
    m = mod_ref[...]
    h = (_rms(x_ref[...], g_ref[...] * (1.0 + m[1:2])) + m[0:1]).astype(BF16)
    cos, sin = cos_ref[...], sin_ref[...]
    scale = HEAD_DIM ** -0.5
    p = jnp.dot(h, w_ref[...], preferred_element_type=F32)
    o = 0
    for j in range(att_w // LANES):
        u = p[:, o + j * LANES:o + (j + 1) * LANES]
        ur = p[:, o + att_w + j * LANES:o + att_w + (j + 1) * LANES]
        q_ref[:, j * LANES:(j + 1) * LANES] = ((u * cos + ur * sin) * scale).astype(q_ref.dtype)
    o += 2 * att_w
    for j in range(kv_w // LANES):
        u = p[:, o + j * LANES:o + (j + 1) * LANES]
        ur = p[:, o + kv_w + j * LANES:o + kv_w + (j + 1) * LANES]
        k_ref[:, j * LANES:(j + 1) * LANES] = (u * cos + ur * sin).astype(k_ref.dtype)
    o += 2 * kv_w
    v_ref[...] = p[:, o:o + kv_w].astype(v_ref.dtype)
    o += kv_w
    pr_ref[...] = p[:, o:]


def _proj_ctx_kernel(x_ref, mod_ref, g_ref, w_ref, k_ref, v_ref, pr_ref, *, kv_w):
    m = mod_ref[...]
    h = (_rms(x_ref[...], g_ref[...] * (1.0 + m[1:2])) + m[0:1]).astype(BF16)
    k_ref[...] = jnp.dot(h, w_ref[:, 0:kv_w], preferred_element_type=F32).astype(k_ref.dtype)
    v_ref[...] = jnp.dot(h, w_ref[:, kv_w:2 * kv_w], preferred_element_type=F32).astype(v_ref.dtype)
    pr_ref[...] = jnp.dot(h, w_ref[:, 2 * kv_w:], preferred_element_type=F32)


def _rot_cols(w):
    d, n = w.shape
    m = HEAD_DIM // 4
    w4 = w.reshape(d, n // (2 * m), 2, m)
    return jnp.stack([-w4[:, :, 1], w4[:, :, 0]], axis=2).reshape(d, n)


def _rope_tables(seq):
    m = HEAD_DIM // 4
    t = jnp.arange(seq)
    inv = ROPE_BASE ** (-jnp.arange(m, dtype=F32) / m)
    ang_r = (t // GRID_W).astype(F32)[:, None] * inv[None, :]
    ang_c = (t % GRID_W).astype(F32)[:, None] * inv[None, :]
    ang = jnp.concatenate([ang_r, ang_r, ang_c, ang_c], axis=-1)
    ang = jnp.tile(ang, (1, LANES // HEAD_DIM))
    return jnp.cos(ang), jnp.sin(ang)


def _project_latent(x, mod_l, g, w_all, cos, sin, att_w, kv_w, rw_w, tm):
    b, s, d = x.shape
    n_all = w_all.shape[1]
    kern = functools.partial(_proj_kernel, att_w=att_w, kv_w=kv_w)
    return pl.pallas_call(
        kern,
        grid=(b, s // tm),
        in_specs=[pl.BlockSpec((None, tm, d), lambda bi, i: (bi, i, 0)),
                  pl.BlockSpec((None, 6, d), lambda bi, i: (bi, 0, 0)),
                  _const_spec((1, d)),
                  _const_spec((d, n_all)),
                  pl.BlockSpec((tm, LANES), lambda bi, i: (i, 0)),
                  pl.BlockSpec((tm, LANES), lambda bi, i: (i, 0))],
        out_specs=[pl.BlockSpec((None, tm, att_w), lambda bi, i: (bi, i, 0)),
                   pl.BlockSpec((None, tm, kv_w), lambda bi, i: (bi, i, 0)),
                   pl.BlockSpec((None, tm, kv_w), lambda bi, i: (bi, i, 0)),
                   pl.BlockSpec((None, tm, rw_w), lambda bi, i: (bi, i, 0))],
        out_shape=[jax.ShapeDtypeStruct((b, s, att_w), BF16),
                   jax.ShapeDtypeStruct((b, s, kv_w), BF16),
                   jax.ShapeDtypeStruct((b, s, kv_w), BF16),
                   jax.ShapeDtypeStruct((b, s, rw_w), F32)],
        compiler_params=_cparams("parallel", "parallel"),
    )(x, mod_l, g, w_all, cos, sin)


def _project_ctx(ctx, mod_c, g, w_ctx, kv_w, rw_w):
    b, l, d = ctx.shape
    kern = functools.partial(_proj_ctx_kernel, kv_w=kv_w)
    return pl.pallas_call(
        kern,
        grid=(b,),
        in_specs=[pl.BlockSpec((None, l, d), lambda bi: (bi, 0, 0)),
                  _const_spec((6, d)),
                  _const_spec((1, d)),
                  _const_spec((d, w_ctx.shape[1]))],
        out_specs=[pl.BlockSpec((None, l, kv_w), lambda bi: (bi, 0, 0)),
                   pl.BlockSpec((None, l, kv_w), lambda bi: (bi, 0, 0)),
                   pl.BlockSpec((None, l, rw_w), lambda bi: (bi, 0, 0))],
        out_shape=[jax.ShapeDtypeStruct((b, l, kv_w), BF16),
                   jax.ShapeDtypeStruct((b, l, kv_w), BF16),
                   jax.ShapeDtypeStruct((b, l, rw_w), F32)],
        compiler_params=_cparams("parallel"),
    )(ctx, mod_c, g, w_ctx)


def _attn_kernel(sink_ref, q_ref, kp_ref, kc_ref, kn_ref, vp_ref, vc_ref, vn_ref, kx_ref, vx_ref, o_ref, *, nsub):
    i = pl.program_id(1)
    nb = pl.num_programs(1)
    blk = ATT_BLOCK
    kall = jnp.concatenate([kp_ref[...], kc_ref[...], kn_ref[...]], axis=0)
    vall = jnp.concatenate([vp_ref[...], vc_ref[...], vn_ref[...]], axis=0)
    vall = jnp.concatenate([vall, jnp.ones_like(vall)], axis=1)
    kctx = kx_ref[...]
    vctx = jnp.concatenate([vx_ref[...], jnp.ones_like(vx_ref[...])], axis=1)
    row = lax.broadcasted_iota(jnp.int32, (2 * blk, 3 * blk), 0) % blk
    col = lax.broadcasted_iota(jnp.int32, (2 * blk, 3 * blk), 1)
    band = jnp.abs(row + blk - col) <= WINDOW
    even = lax.broadcasted_iota(jnp.int32, (blk, LANES), 1) < HEAD_DIM
    zero = jnp.zeros((), q_ref.dtype)
    tiles = range(q_ref.shape[1] // LANES)
    sinks = [sink_ref[j] for j in tiles]

    def band_masked(s, va):
        return jnp.concatenate([jnp.where(va[:, :blk], s[:, :blk], -1e30), s[:, blk:2 * blk],
                                jnp.where(va[:, 2 * blk:], s[:, 2 * blk:], -1e30)], axis=1)

    def scores(a):
        va = band
        if a == 0:
            va = va & jnp.logical_not((i == 0) & (col < blk))
        if a == nsub - 1:
            va = va & jnp.logical_not((i == nb - 1) & (col >= 2 * blk))
        kloc = kall[a * blk:(a + 3) * blk]
        qs = []
        for j in tiles:
            qj = q_ref[a * blk:(a + 1) * blk, j * LANES:(j + 1) * LANES]
            qs.append(jnp.concatenate([jnp.where(even, qj, zero), jnp.where(even, zero, qj)], axis=0))
        return [band_masked(_dot_nt(qj, kloc), va) for qj in qs], [_dot_nt(qj, kctx) for qj in qs]

    def row_max(sl, sc, sk):
        t = sk
        for s in (sl, sc):
            for c0 in range(0, s.shape[1], LANES):
                t = jnp.maximum(t, s[:, c0:c0 + LANES])
        return jnp.max(t, axis=-1, keepdims=True)

    def finish(a, s_loc, s_ctx):
        vloc = vall[a * blk:(a + 3) * blk]
        mx = [row_max(sl, sc, sk) for sl, sc, sk in zip(s_loc, s_ctx, sinks)]
        p_loc = [jnp.exp((sl - m).astype(BF16)) for sl, m in zip(s_loc, mx)]
        p_ctx = [jnp.exp((sc - m).astype(BF16)) for sc, m in zip(s_ctx, mx)]
        num = [jnp.dot(pa, vloc, preferred_element_type=F32) + jnp.dot(pc, vctx, preferred_element_type=F32)
               for pa, pc in zip(p_loc, p_ctx)]
        for j, n, sk, m in zip(tiles, num, sinks, mx):
            o = n[:, :LANES] / (n[:, LANES:] + jnp.exp(sk - m))
            o_ref[a * blk:(a + 1) * blk, j * LANES:(j + 1) * LANES] = jnp.where(even, o[:blk], o[blk:]).astype(o_ref.dtype)

    pending = scores(0)
    for a in range(nsub):
        ahead = scores(a + 1) if a + 1 < nsub else None
        finish(a, *pending)
        pending = ahead


def _attention(q, k, v, kc, vc, sink_rows):
    b, s, att_w = q.shape
    kv_w = k.shape[2]
    l = kc.shape[1]
    blk = ATT_BLOCK
    nsub = _pick_tile(s // blk, ATT_SUBBLOCKS)
    nb = s // (nsub * blk)
    kern = functools.partial(_attn_kernel, nsub=nsub)
    kv_prev = pl.BlockSpec((None, blk, kv_w), lambda bi, i: (bi, jnp.maximum(i * nsub - 1, 0), 0))
    kv_cur = pl.BlockSpec((None, nsub * blk, kv_w), lambda bi, i: (bi, i, 0))
    kv_next = pl.BlockSpec((None, blk, kv_w), lambda bi, i: (bi, jnp.minimum((i + 1) * nsub, nb * nsub - 1), 0))
    kv_ctx = pl.BlockSpec((None, l, kv_w), lambda bi, i: (bi, 0, 0))
    return pl.pallas_call(
        kern,
        grid=(b, nb),
        in_specs=[_const_spec(sink_rows.shape),
                  pl.BlockSpec((None, nsub * blk, att_w), lambda bi, i: (bi, i, 0)),
                  kv_prev, kv_cur, kv_next, kv_prev, kv_cur, kv_next, kv_ctx, kv_ctx],
        out_specs=pl.BlockSpec((None, nsub * blk, att_w), lambda bi, i: (bi, i, 0)),
        out_shape=jax.ShapeDtypeStruct((b, s, att_w), BF16),
        compiler_params=_cparams("parallel", "parallel"),
    )(sink_rows, q, k, k, k, v, v, v, kc, vc)


def _stack_heads(x, even):
    return jnp.concatenate([jnp.where(even, x, 0.0), jnp.where(even, 0.0, x)], axis=0)


def _chunk_summaries(insts):
    c2 = 2 * CHUNK
    even = lax.broadcasted_iota(jnp.int32, (CHUNK, LANES), 1) < HEAD_DIM
    rt = lax.broadcasted_iota(jnp.int32, (CHUNK, LANES), 0)
    ct = lax.broadcasted_iota(jnp.int32, (CHUNK, LANES), 1) % HEAD_DIM
    masks = {True: (ct < rt, ct <= rt), False: (ct > rt, ct >= rt)}
    eye = lax.broadcasted_iota(jnp.int32, (LANES, LANES), 0) == lax.broadcasted_iota(jnp.int32, (LANES, LANES), 1)

    zero = jnp.zeros((), BF16)

    def stacked(f):
        f = f.astype(BF16)
        return jnp.concatenate([jnp.where(even, f, zero), jnp.where(even, zero, f)], axis=0)

    prep = []
    for rr, vv, aa, bb, kd, cs, lw, tot, forward in insts:
        g_inv = jnp.exp(-cs)
        g_rem = jnp.exp(tot - cs)
        a_t = aa * jnp.exp(cs - lw)
        r_t = rr * jnp.exp(cs)
        v_s = stacked(vv)
        bk_s = jnp.concatenate([stacked(bb * g_rem), stacked(kd * g_rem)], axis=0)
        lhs = jnp.concatenate([a_t, r_t], axis=0).astype(BF16)
        rhs = jnp.concatenate([stacked(bb * g_inv), stacked(kd * g_inv)], axis=0)
        prep.append((_stack_heads(a_t, even), _stack_heads(r_t, even), v_s, bk_s, lhs, rhs, jnp.exp(tot), masks[forward]))

    gs = [_dot_nt(p[4], p[5]) for p in prep]
    tri, lfs = [], []
    for g, p in zip(gs, prep):
        strict, incl = p[7]
        lfs.append(jnp.where(strict, g[:CHUNK, :c2], 0.0).astype(BF16))
        tri.append((stacked(jnp.where(strict, g[:CHUNK, c2:], 0.0)), stacked(jnp.where(incl, g[CHUNK:, :c2], 0.0)),
                    stacked(jnp.where(incl, g[CHUNK:, c2:], 0.0))))
    own = (lax.broadcasted_iota(jnp.int32, (c2, LANES), 0) < CHUNK) == (
        lax.broadcasted_iota(jnp.int32, (c2, LANES), 1) < HEAD_DIM)
    xs = [p[0] + pltpu.roll(_dot(t[0], p[2]), HEAD_DIM, 1) for p, t in zip(prep, tri)]

    ts = [jnp.where((rt // 2 == ct // 2), lf.astype(F32), 0.0) + jnp.where(rt == ct, 1.0, 0.0) for lf in lfs]
    size = 2
    while size < CHUNK:
        couple = (rt // (2 * size) == ct // (2 * size)) & (rt // size != ct // size)
        os_ = [jnp.where(couple, lf, zero) for lf in lfs]
        ots = [_dot(o, stacked(t)) for o, t in zip(os_, ts)]
        ts = [t + _dot(t, stacked(ot)) for t, ot in zip(ts, ots)]
        size *= 2
    xs = [_dot(stacked(t), x) for t, x in zip(ts, xs)]
    rys = [p[1] + pltpu.roll(_dot(t[2], p[2]), HEAD_DIM, 1) + _dot(t[1], x)
           for p, t, x in zip(prep, tri, xs)]
    out = []
    for p, x, ry in zip(prep, xs, rys):
        v_s, bk_s, g_tot = p[2], p[3], p[6]
        x_a = jnp.where(own, x, 0.0).astype(BF16)
        x_u = pltpu.roll(jnp.where(own, 0.0, x), HEAD_DIM, 1).astype(BF16)
        pt = jnp.where(eye, g_tot, 0.0) + _dot_tn(bk_s[:c2], x_a)
        qt = _dot_tn(bk_s, jnp.concatenate([x_u, v_s], axis=0))
        rh = jnp.where(even, ry[:CHUNK], ry[CHUNK:])
        yh = pltpu.roll(jnp.where(even, ry[CHUNK:], ry[:CHUNK]), HEAD_DIM, 1)
        out.append((pt[:CHUNK] + pt[CHUNK:], qt[:CHUNK] + qt[CHUNK:], rh, yh))
    return out


def _feat_kernel(pr_ref, hp_ref, hn_ref, mu_ref, w0_ref, a0_ref, w2_ref, a2_ref, g2_ref, kkp_ref, ka_ref, rk_ref,
                 ones_ref, pt_ref, qt_ref, rh_ref, yh_ref, bonus_ref, gate_ref, *, rw_w):
    i = pl.program_id(1)
    last = pl.num_programs(1) - 1
    tt = pr_ref.shape[0]
    pr = pr_ref[...]
    ridx = lax.broadcasted_iota(jnp.int32, (tt, 1), 0)
    edge_prev = jnp.where(i == 0, 0.0, hp_ref[SUBLANES - 1:SUBLANES, :])
    edge_next = jnp.where(i == last, 0.0, hn_ref[0:1, :])
    prev = jnp.where(ridx == 0, edge_prev, pltpu.roll(pr, 1, 0))
    nxt = jnp.where(ridx == tt - 1, edge_next, pltpu.roll(pr, tt - 1, 0))
    mu = mu_ref[...]
    x = pr * (1.0 - mu) + (0.5 * mu) * (prev + nxt)

    r = x[:, 0:rw_w]
    k = x[:, rw_w:2 * rw_w]
    v = x[:, 2 * rw_w:3 * rw_w]
    o = 3 * rw_w
    wd = x[:, o:o + LANES]
    ad = x[:, o + LANES:o + 2 * LANES]
    gd = x[:, o + 2 * LANES:o + 3 * LANES]

    ones_bd = ones_ref[...]
    logw = -DECAY_SCALE * _sigmoid(_dot(jnp.tanh(wd), w2_ref[...]) + w0_ref[...])
    iclr = _sigmoid(_dot(ad, a2_ref[...]) + a0_ref[...])
    gate_ref[...] = _dot(_sigmoid(gd), g2_ref[...])

    kkv = k * kkp_ref[...]
    kk_sq = jnp.dot((kkv * kkv).astype(BF16), ones_bd, preferred_element_type=F32)
    kk = kkv / jnp.maximum(jnp.sqrt(kk_sq), 1e-12)
    ka = ka_ref[...]
    k_dir = [k * (1.0 + (iclr[:, d * rw_w:(d + 1) * rw_w] - 1.0) * ka) for d in range(2)]
    bonus_ref[...] = _head_sum(r * (k_dir[0] + k_dir[1]) * rk_ref[...], ones_bd) * v

    row = lax.broadcasted_iota(jnp.int32, (tt, tt), 0)
    col = lax.broadcasted_iota(jnp.int32, (tt, tt), 1)
    same = (row // CHUNK) == (col // CHUNK)
    tri = [jnp.where(same & (col <= row), 1.0, 0.0).astype(BF16),
           jnp.where(same & (col >= row), 1.0, 0.0).astype(BF16)]
    neg_kk = -kk
    lw_dir, cs_dir, b_dir = [], [], []
    for d in range(2):
        lw_d = logw[:, d * rw_w:(d + 1) * rw_w]
        lw_dir.append(lw_d)
        cs_dir.append(sum(jnp.dot(tri[d], p, preferred_element_type=F32) for p in _split2(lw_d)))
        b_dir.append(kk * iclr[:, d * rw_w:(d + 1) * rw_w])
    nchunks = tt // CHUNK
    group = _pick_tile(nchunks, FEAT_CHUNK_GROUP)
    for c0 in range(0, nchunks, group):
        keys, insts = [], []
        for c in range(c0, c0 + group):
            rows = slice(c * CHUNK, (c + 1) * CHUNK)
            for d in range(2):
                end = (c + 1) * CHUNK - 1 if d == 0 else c * CHUNK
                for p in range(rw_w // LANES):
                    ln = slice(p * LANES, (p + 1) * LANES)
                    keys.append((c, d, p, rows, ln))
                    insts.append((r[rows, ln], v[rows, ln], neg_kk[rows, ln], b_dir[d][rows, ln], k_dir[d][rows, ln],
                                  cs_dir[d][rows, ln], lw_dir[d][rows, ln], cs_dir[d][end:end + 1, ln], d == 0))
        for (c, d, p, rows, ln), (pt, qt, rh, yh) in zip(keys, _chunk_summaries(insts)):
            pt_ref[c, d, p] = pt
            qt_ref[c, d, p] = qt
            rh_ref[d, rows, ln] = rh.astype(rh_ref.dtype)
            yh_ref[d, rows, ln] = yh.astype(yh_ref.dtype)


def _rwkv_features(pr, fp, tt):
    b, t, w_all = pr.shape
    rw_w = fp["kkp"].shape[1]
    npair = rw_w // LANES
    nt = t // tt
    cpt = tt // CHUNK
    hb = tt // SUBLANES
    kern = functools.partial(_feat_kernel, rw_w=rw_w)
    names = ("mu", "w0", "a0", "w2", "a2", "g2", "kkp", "ka", "rk", "ones")
    return pl.pallas_call(
        kern,
        grid=(b, nt),
        in_specs=[pl.BlockSpec((None, tt, w_all), lambda bi, i: (bi, i, 0)),
                  pl.BlockSpec((None, SUBLANES, w_all), lambda bi, i: (bi, jnp.maximum(i * hb - 1, 0), 0)),
                  pl.BlockSpec((None, SUBLANES, w_all), lambda bi, i: (bi, jnp.minimum((i + 1) * hb, nt * hb - 1), 0))]
                 + [_const_spec(fp[n].shape) for n in names],
        out_specs=[pl.BlockSpec((None, cpt, 2, npair, CHUNK, LANES), lambda bi, i: (bi, i, 0, 0, 0, 0)),
                   pl.BlockSpec((None, cpt, 2, npair, CHUNK, LANES), lambda bi, i: (bi, i, 0, 0, 0, 0)),
                   pl.BlockSpec((None, 2, tt, rw_w), lambda bi, i: (bi, 0, i, 0)),
                   pl.BlockSpec((None, 2, tt, rw_w), lambda bi, i: (bi, 0, i, 0)),
                   pl.BlockSpec((None, tt, rw_w), lambda bi, i: (bi, i, 0)),
                   pl.BlockSpec((None, tt, rw_w), lambda bi, i: (bi, i, 0))],
        out_shape=[jax.ShapeDtypeStruct((b, t // CHUNK, 2, npair, CHUNK, LANES), F32),
                   jax.ShapeDtypeStruct((b, t // CHUNK, 2, npair, CHUNK, LANES), F32),
                   jax.ShapeDtypeStruct((b, 2, t, rw_w), BF16),
                   jax.ShapeDtypeStruct((b, 2, t, rw_w), BF16),
                   jax.ShapeDtypeStruct((b, t, rw_w), F32),
                   jax.ShapeDtypeStruct((b, t, rw_w), F32)],
        compiler_params=_cparams("parallel", "parallel"),
    )(pr, pr, pr, *[fp[n] for n in names])


def _scan_kernel(s0_ref, ptf_ref, qtf_ref, ptb_ref, qtb_ref, rhf_ref, yhf_ref, rhb_ref, yhb_ref,
                 yf_ref, yb_ref, sfin_ref, st_ref, *, cps, npair):
    i = pl.program_id(1)

    @pl.when(i == 0)
    def _():
        st_ref[...] = s0_ref[...]

    even = lax.broadcasted_iota(jnp.int32, (CHUNK, LANES), 1) < HEAD_DIM
    zero = jnp.zeros((), BF16)
    dirs = ((ptf_ref, qtf_ref, rhf_ref, yhf_ref, yf_ref), (ptb_ref, qtb_ref, rhb_ref, yhb_ref, yb_ref))
    keys = [(d, p) for d in range(2) for p in range(npair)]
    st = [st_ref[d, p] for d, p in keys]
    for step in range(cps):
        hl = []
        for s in st:
            hi = s.astype(BF16)
            lo = (s - hi.astype(F32)).astype(BF16)
            hl.append(jnp.concatenate([jnp.concatenate([jnp.where(even, hi, zero), jnp.where(even, zero, hi)], axis=0),
                                       jnp.concatenate([jnp.where(even, lo, zero), jnp.where(even, zero, lo)], axis=0)],
                                      axis=1))
        new = []
        for (d, p), s2 in zip(keys, hl):
            pt_ref, qt_ref, rh_ref, yh_ref, y_ref = dirs[d]
            c = step if d == 0 else cps - 1 - step
            rows = slice(c * CHUNK, (c + 1) * CHUNK)
            ln = slice(p * LANES, (p + 1) * LANES)
            y = jnp.dot(rh_ref[rows, ln], s2[:, :LANES], preferred_element_type=F32) + yh_ref[rows, ln].astype(F32)
            y_ref[rows, ln] = y.astype(y_ref.dtype)
            pt_hi, pt_lo = _split2(pt_ref[c, p])
            n2 = jnp.dot(pt_hi, s2, preferred_element_type=F32)
            new.append(n2[:, :LANES] + n2[:, LANES:] + jnp.dot(pt_lo, s2[:, :LANES], preferred_element_type=F32)
                       + qt_ref[c, p])
        st = new
    for (d, p), s in zip(keys, st):
        st_ref[d, p] = s

    @pl.when(i == pl.num_programs(1) - 1)
    def _():
        sfin_ref[...] = st_ref[...]


def _rwkv_scan(s0, pt, qt, rh, yh, cps):
    b, nc, _, npair, _, _ = pt.shape
    t, rw_w = rh.shape[2], rh.shape[3]
    ns = nc // cps
    ts = cps * CHUNK
    kern = functools.partial(_scan_kernel, cps=cps, npair=npair)
    mat_f = pl.BlockSpec((None, cps, None, npair, CHUNK, LANES), lambda bi, i: (bi, i, 0, 0, 0, 0))
    mat_b = pl.BlockSpec((None, cps, None, npair, CHUNK, LANES), lambda bi, i: (bi, ns - 1 - i, 1, 0, 0, 0))
    tok_f = pl.BlockSpec((None, None, ts, rw_w), lambda bi, i: (bi, 0, i, 0))
    tok_b = pl.BlockSpec((None, None, ts, rw_w), lambda bi, i: (bi, 1, ns - 1 - i, 0))
    state = pl.BlockSpec((None, 2, npair, CHUNK, LANES), lambda bi, i: (bi, 0, 0, 0, 0))
    return pl.pallas_call(
        kern,
        grid=(b, ns),
        in_specs=[state, mat_f, mat_f, mat_b, mat_b, tok_f, tok_f, tok_b, tok_b],
        out_specs=[pl.BlockSpec((None, ts, rw_w), lambda bi, i: (bi, i, 0)),
                   pl.BlockSpec((None, ts, rw_w), lambda bi, i: (bi, ns - 1 - i, 0)),
                   state],
        out_shape=[jax.ShapeDtypeStruct((b, t, rw_w), BF16),
                   jax.ShapeDtypeStruct((b, t, rw_w), BF16),
                   jax.ShapeDtypeStruct((b, 2, npair, CHUNK, LANES), F32)],
        scratch_shapes=[pltpu.VMEM((2, npair, CHUNK, LANES), F32)],
        compiler_params=_cparams("parallel", "arbitrary"),
    )(s0, pt, qt, pt, qt, rh, yh, rh, yh)


def _residual_mlp(xl, yl, m, ng, w1_ref, w2_ref):
    x2 = xl + _rms(yl, m[2:3] * ng[1:2])
    hm = (_rms(x2, ng[2:3] * (1.0 + m[4:5])) + m[3:4]).astype(BF16)
    out = None
    for j in range(w1_ref.shape[1] // MLP_HIDDEN_BLOCK):
        cols = slice(j * MLP_HIDDEN_BLOCK, (j + 1) * MLP_HIDDEN_BLOCK)
        hid = jnp.maximum(jnp.dot(hm, w1_ref[:, cols], preferred_element_type=F32), 0.0)
        part = jnp.dot((hid * hid).astype(BF16), w2_ref[cols, :], preferred_element_type=F32)
        out = part if out is None else out + part
    return x2 + _rms(out, m[5:6] * ng[3:4])


def _mix_out_kernel(x_ref, att_ref, yf_ref, yb_ref, bonus_ref, gate_ref, mod_ref, ng_ref, lnw_ref, lnb_ref, ones_ref,
                    wo_ref, w1_ref, w2_ref, o_ref, *, att_w):
    ones_bd = ones_ref[...]
    y = yf_ref[...].astype(F32) + yb_ref[...].astype(F32)
    mean = _head_sum(y, ones_bd) * (1.0 / HEAD_DIM)
    yc = y - mean
    var = jnp.dot((yc * yc).astype(BF16), ones_bd, preferred_element_type=F32) * (1.0 / HEAD_DIM)
    yn = yc * lax.rsqrt(var + GN_EPS) * lnw_ref[...] + lnb_ref[...]
    rw = (yn + bonus_ref[...]) * gate_ref[...]
    yl = (jnp.dot(att_ref[...], wo_ref[0:att_w, :], preferred_element_type=F32)
          + jnp.dot(rw.astype(BF16), wo_ref[att_w:, :], preferred_element_type=F32))
    o_ref[...] = _residual_mlp(x_ref[...], yl, mod_ref[...], ng_ref[...], w1_ref, w2_ref)


def _mix_out_mlp(x, att, yf, yb, bonus, gate, mod_l, ng, lnw, lnb, ones_bd, wo, w1, w2, tm):
    b, s, d = x.shape
    att_w = att.shape[2]
    rw_w = yf.shape[2]
    kern = functools.partial(_mix_out_kernel, att_w=att_w)
    tok = lambda w: pl.BlockSpec((None, tm, w), lambda bi, i: (bi, i, 0))
    return pl.pallas_call(
        kern,
        grid=(b, s // tm),
        in_specs=[tok(d), tok(att_w), tok(rw_w), tok(rw_w), tok(rw_w), tok(rw_w),
                  pl.BlockSpec((None, 6, d), lambda bi, i: (bi, 0, 0)),
                  _const_spec(ng.shape), _const_spec(lnw.shape), _const_spec(lnb.shape), _const_spec(ones_bd.shape),
                  _const_spec(wo.shape), _const_spec(w1.shape), _const_spec(w2.shape)],
        out_specs=tok(d),
        out_shape=jax.ShapeDtypeStruct((b, s, d), F32),
        compiler_params=_cparams("parallel", "parallel"),
    )(x, att, yf, yb, bonus, gate, mod_l, ng, lnw, lnb, ones_bd, wo, w1, w2)


def _conv_kernel(x_ref, xp_ref, xn_ref, mod_ref, ng_ref, wi_ref, cw_ref, wo_ref, w1_ref, w2_ref, o_ref):
    i = pl.program_id(1)
    last = pl.num_programs(1) - 1
    d = x_ref.shape[1]
    tm = x_ref.shape[0]
    m = mod_ref[...]
    ng = ng_ref[...]

    def modnorm(u):
        return (_rms(u, ng[0:1] * (1.0 + m[1:2])) + m[0:1]).astype(BF16)

    x = x_ref[...]
    rows = jnp.concatenate([x, xp_ref[...], xn_ref[...]], axis=0)
    proj = jnp.dot(modnorm(rows), wi_ref[...], preferred_element_type=F32)
    z_all = proj[:, d:2 * d] * proj[:, 2 * d:]
    z = z_all[:tm]
    zp = jnp.where(i == 0, 0.0, z_all[tm + SUBLANES - 1:tm + SUBLANES, :])
    zn = jnp.where(i == last, 0.0, z_all[tm + SUBLANES:tm + SUBLANES + 1, :])
    ridx = lax.broadcasted_iota(jnp.int32, (tm, 1), 0)
    prev = jnp.where(ridx == 0, zp, pltpu.roll(z, 1, 0))
    nxt = jnp.where(ridx == tm - 1, zn, pltpu.roll(z, tm - 1, 0))
    cw = cw_ref[...]
    y = proj[:tm, :d] * (prev * cw[0:1] + z * cw[1:2] + nxt * cw[2:3])
    yl = jnp.dot(y.astype(BF16), wo_ref[...], preferred_element_type=F32)
    o_ref[...] = _residual_mlp(x, yl, m, ng, w1_ref, w2_ref)


def _conv_mlp(x, mod_l, ng, wi, cw, wo, w1, w2, tm):
    b, s, d = x.shape
    hb = tm // SUBLANES
    nt = s // tm
    return pl.pallas_call(
        _conv_kernel,
        grid=(b, nt),
        in_specs=[pl.BlockSpec((None, tm, d), lambda bi, i: (bi, i, 0)),
                  pl.BlockSpec((None, SUBLANES, d), lambda bi, i: (bi, jnp.maximum(i * hb - 1, 0), 0)),
                  pl.BlockSpec((None, SUBLANES, d), lambda bi, i: (bi, jnp.minimum((i + 1) * hb, nt * hb - 1), 0)),
                  pl.BlockSpec((None, 6, d), lambda bi, i: (bi, 0, 0)),
                  _const_spec(ng.shape), _const_spec(wi.shape), _const_spec(cw.shape), _const_spec(wo.shape),
                  _const_spec(w1.shape), _const_spec(w2.shape)],
        out_specs=pl.BlockSpec((None, tm, d), lambda bi, i: (bi, i, 0)),
        out_shape=jax.ShapeDtypeStruct((b, s, d), F32),
        compiler_params=_cparams("parallel", "parallel"),
    )(x, x, x, mod_l, ng, wi, cw, wo, w1, w2)


def _block_diag2(m):
    z = jnp.zeros_like(m[0])
    return jnp.concatenate([jnp.concatenate([m[0], z], axis=1), jnp.concatenate([z, m[1]], axis=1)], axis=0)


def _pick_tile(n, want):
    t = min(n, want)
    while n % t:
        t //= 2
    return t


def kernel(x, c, ctx, c_ctx, mod_w, mod_b, norm_g, mlp_w1, mlp_w2, ab_w_in, ab_w_out, att_sink, rwkv_mu, rwkv_w0,
           rwkv_w2, rwkv_a0, rwkv_a2, rwkv_g2, rwkv_kk, rwkv_ka, rwkv_rk, rwkv_ln_w, rwkv_ln_b, conv_w_in, conv_w,
           conv_w_out):
    b, s, d = x.shape
    l = ctx.shape[1]
    depth = mod_w.shape[0]
    assert depth == 2, "layer schedule below is written for one attention/RWKV layer followed by one conv layer"
    att_w = att_sink.shape[1] * HEAD_DIM
    kv_w = att_w // ATT_GROUP
    rw_w = rwkv_kk.shape[1]
    rw_in = rwkv_mu.shape[1]
    assert s % ATT_BLOCK == 0 and s % CHUNK == 0 and l % CHUNK == 0 and rw_w % LANES == 0
    assert kv_w == LANES and WINDOW == ATT_BLOCK

    rows = -(-(b + 1) // SUBLANES) * SUBLANES
    cv = jnp.concatenate([c, c_ctx[None, :], jnp.zeros((rows - b - 1, d), F32)], axis=0)
    mod = _modulation(cv, mod_w, mod_b)
    mod_l = [mod[i, :b].reshape(b, 6, d) for i in range(depth)]
    mod_c0 = mod[0, b].reshape(6, d)

    w_in = ab_w_in[0]
    wq, wk, wv, wr = (w_in[:, :att_w], w_in[:, att_w:att_w + kv_w], w_in[:, att_w + kv_w:att_w + 2 * kv_w],
                      w_in[:, att_w + 2 * kv_w:])
    n_heads = att_w // HEAD_DIM
    perm = jnp.arange(n_heads).reshape(n_heads // ATT_GROUP, ATT_GROUP).T.reshape(-1)
    wq = wq.reshape(d, n_heads, HEAD_DIM)[:, perm].reshape(d, att_w)
    w_lat = jnp.concatenate([wq, _rot_cols(wq), wk, _rot_cols(wk), wv, wr], axis=1).astype(BF16)
    w_ctx = jnp.concatenate([wk, wv, wr], axis=1).astype(BF16)
    cos, sin = _rope_tables(s)
    g0 = norm_g[0, 0].reshape(1, d)
    q, k, v, pr = _project_latent(x, mod_l[0], g0, w_lat, cos, sin, att_w, kv_w, rw_in, _pick_tile(s, PROJ_ROWS))
    kc, vc, prc = _project_ctx(ctx, mod_c0, g0, w_ctx, kv_w, rw_in)
    sink_rows = jnp.repeat(att_sink[0][perm].reshape(att_w // LANES, LANES // HEAD_DIM), ATT_BLOCK, axis=1)
    sink_rows = jnp.broadcast_to(sink_rows[..., None], sink_rows.shape + (LANES,))
    att = _attention(q, k, v, kc, vc, sink_rows)
    w_out = ab_w_out[0]
    w_out = jnp.concatenate([w_out[:att_w].reshape(n_heads, HEAD_DIM, d)[perm].reshape(att_w, d), w_out[att_w:]], axis=0)

    head_id = jnp.arange(rw_w) // HEAD_DIM
    ones_bd = (head_id[:, None] == head_id[None, :]).astype(BF16)
    fp = dict(mu=rwkv_mu[0].reshape(1, rw_in),
              w0=rwkv_w0[0].reshape(1, 2 * rw_w), a0=rwkv_a0[0].reshape(1, 2 * rw_w),
              w2=_block_diag2(rwkv_w2[0]).astype(BF16), a2=_block_diag2(rwkv_a2[0]).astype(BF16),
              g2=rwkv_g2[0].astype(BF16),
              kkp=rwkv_kk[0].reshape(1, rw_w), ka=rwkv_ka[0].reshape(1, rw_w), rk=rwkv_rk[0].reshape(1, rw_w),
              ones=ones_bd)
    npair = rw_w // LANES
    ptc, qtc, rhc, yhc, _, _ = _rwkv_features(prc, fp, _pick_tile(l, FEAT_ROWS))
    zero_state = jnp.zeros((b, 2, npair, CHUNK, LANES), F32)
    _, _, s_ctx = _rwkv_scan(zero_state, ptc, qtc, rhc, yhc, _pick_tile(l // CHUNK, SCAN_CHUNKS))
    pt, qt, rh, yh, bonus, gate = _rwkv_features(pr, fp, _pick_tile(s, FEAT_ROWS))
    yf, yb, _ = _rwkv_scan(s_ctx, pt, qt, rh, yh, _pick_tile(s // CHUNK, SCAN_CHUNKS))

    xl = _mix_out_mlp(x, att, yf, yb, bonus, gate, mod_l[0], norm_g[0],
                      rwkv_ln_w[0].reshape(1, rw_w), rwkv_ln_b[0].reshape(1, rw_w), ones_bd,
                      w_out.astype(BF16), mlp_w1[0].astype(BF16), mlp_w2[0].astype(BF16), _pick_tile(s, MLP_ROWS))

    return _conv_mlp(xl, mod_l[1], norm_g[1], conv_w_in[0].astype(BF16), conv_w[0], conv_w_out[0].astype(BF16),
                     mlp_w1[1].astype(BF16), mlp_w2[1].astype(BF16), _pick_tile(s, MLP_ROWS))
```

```python
import functools
import math

import jax
import jax.numpy as jnp
from jax import lax
from jax.experimental import pallas as pl
from jax.experimental.pallas import tpu as pltpu

F32 = jnp.float32
BF16 = jnp.bfloat16

HEAD_DIM = 64
GRID_W = 64
WINDOW = 128
ATT_BLOCK = 128
ATT_GROUP = 4
ROPE_BASE = 10000.0
NORM_EPS = 1e-6
GN_EPS = 64e-5
LANES = 128
SUBLANES = 8
CHUNK = 64
DECAY_SCALE = math.exp(-0.5)
PROJ_ROWS = 1024
FEAT_ROWS = 256
FEAT_CHUNK_GROUP = 2
SCAN_CHUNKS = 16
ATT_SUBBLOCKS = 16
MLP_ROWS = 512
MLP_HIDDEN_BLOCK = 2048
VMEM_LIMIT = 56 * 1024 * 1024


def _cparams(*sem):
    return pltpu.CompilerParams(dimension_semantics=sem, vmem_limit_bytes=VMEM_LIMIT)


def _const_spec(shape):
    nd = len(shape)
    return pl.BlockSpec(shape, lambda *_: (0,) * nd, pipeline_mode=pl.Buffered(1))


def _dot(a, b):
    return jnp.dot(a.astype(BF16), b.astype(BF16), preferred_element_type=F32)


def _dot_nt(a, b):
    return lax.dot_general(a.astype(BF16), b.astype(BF16), (((1,), (1,)), ((), ())), preferred_element_type=F32)


def _dot_tn(a, b):
    return lax.dot_general(a.astype(BF16), b.astype(BF16), (((0,), (0,)), ((), ())), preferred_element_type=F32)


def _split2(x):
    hi = x.astype(BF16)
    lo = (x - hi.astype(F32)).astype(BF16)
    return hi, lo


def _head_sum(x, ones_bd):
    hi, lo = _split2(x)
    return (jnp.dot(hi, ones_bd, preferred_element_type=F32)
            + jnp.dot(lo, ones_bd, preferred_element_type=F32))


def _rms(u, g):
    return u * lax.rsqrt(jnp.mean(u * u, axis=-1, keepdims=True) + NORM_EPS) * g


def _sigmoid(z):
    return 1.0 / (1.0 + jnp.exp(-z))


def _mod_kernel(cv_ref, w_ref, b_ref, o_ref):
    cv = cv_ref[...]
    s = cv * _sigmoid(cv)
    o_ref[...] = _dot(s, w_ref[...]) + b_ref[...]


def _modulation(cv, mod_w, mod_b):
    depth, d, six_d = mod_w.shape
    rows = cv.shape[0]
    nj = six_d // d
    return pl.pallas_call(
        _mod_kernel,
        grid=(depth, nj),
        in_specs=[pl.BlockSpec((rows, d), lambda l, j: (0, 0)),
                  pl.BlockSpec((None, d, d), lambda l, j: (l, 0, j)),
                  pl.BlockSpec((None, 1, d), lambda l, j: (l, 0, j))],
        out_specs=pl.BlockSpec((None, rows, d), lambda l, j: (l, 0, j)),
        out_shape=jax.ShapeDtypeStruct((depth, rows, six_d), F32),
        compiler_params=_cparams("arbitrary", "arbitrary"),
    )(cv, mod_w, mod_b.reshape(depth, 1, six_d))


def _proj_kernel(x_ref, mod_ref, g_ref, w_ref, cos_ref, sin_ref, q_ref, k_ref, v_ref, pr_ref, *, att_w, kv_w):
    m = mod_ref[...]
    h = (_rms(x_ref[...], g_ref[...] * (1.0 + m[1:2])) + m[0:1]).astype(BF16)
    cos, sin = cos_ref[...], sin_ref[...]
    scale = HEAD_DIM ** -0.5
    p = jnp.dot(h, w_ref[...], preferred_element_type=F32)
    o = 0
    for j in range(att_w // LANES):
        u = p[:, o + j * LANES:o + (j + 1) * LANES]
        ur = p[:, o + att_w + j * LANES:o + att_w + (j + 1) * LANES]
        q_ref[:, j * LANES:(j + 1) * LANES] = ((u * cos + ur * sin) * scale).astype(q_ref.dtype)
    o += 2 * att_w
    for j in range(kv_w // LANES):
        u = p[:, o + j * LANES:o + (j + 1) * LANES]
        ur = p[:, o + kv_w + j * LANES:o + kv_w + (j + 1) * LANES]
        k_ref[:, j * LANES:(j + 1) * LANES] = (u * cos + ur * sin).astype(k_ref.dtype)
    o += 2 * kv_w
    v_ref[...] = p[:, o:o + kv_w].astype(v_ref.dtype)
    o += kv_w
    pr_ref[...] = p[:, o:]


def _proj_ctx_kernel(x_ref, mod_ref, g_ref, w_ref, k_ref, v_ref, pr_ref, *, kv_w):
    m = mod_ref[...]
    h = (_rms(x_ref[...], g_ref[...] * (1.0 + m[1:2])) + m[0:1]).astype(BF16)
    k_ref[...] = jnp.dot(h, w_ref[:, 0:kv_w], preferred_element_type=F32).astype(k_ref.dtype)
    v_ref[...] = jnp.dot(h, w_ref[:, kv_w:2 * kv_w], preferred_element_type=F32).astype(v_ref.dtype)
    pr_ref[...] = jnp.dot(h, w_ref[:, 2 * kv_w:], preferred_element_type=F32)


def _rot_cols(w):
    d, n = w.shape
    m = HEAD_DIM // 4
    w4 = w.reshape(d, n // (2 * m), 2, m)
    return jnp.stack([-w4[:, :, 1], w4[:, :, 0]], axis=2).reshape(d, n)


def _rope_tables(seq):
    m = HEAD_DIM // 4
    t = jnp.arange(seq)
    inv = ROPE_BASE ** (-jnp.arange(m, dtype=F32) / m)
    ang_r = (t // GRID_W).astype(F32)[:, None] * inv[None, :]
    ang_c = (t % GRID_W).astype(F32)[:, None] * inv[None, :]
    ang = jnp.concatenate([ang_r, ang_r, ang_c, ang_c], axis=-1)
    ang = jnp.tile(ang, (1, LANES // HEAD_DIM))
    return jnp.cos(ang), jnp.sin(ang)


def _project_latent(x, mod_l, g, w_all, cos, sin, att_w, kv_w, rw_w, tm):
    b, s, d = x.shape
    n_all = w_all.shape[1]
    kern = functools.partial(_proj_kernel, att_w=att_w, kv_w=kv_w)
    return pl.pallas_call(
        kern,
        grid=(b, s // tm),
        in_specs=[pl.BlockSpec((None, tm, d), lambda bi, i: (bi, i, 0)),
                  pl.BlockSpec((None, 6, d), lambda bi, i: (bi, 0, 0)),
                  _const_spec((1, d)),
                  _const_spec((d, n_all)),
                  pl.BlockSpec((tm, LANES), lambda bi, i: (i, 0)),
                  pl.BlockSpec((tm, LANES), lambda bi, i: (i, 0))],
        out_specs=[pl.BlockSpec((None, tm, att_w), lambda bi, i: (bi, i, 0)),
                   pl.BlockSpec((None, tm, kv_w), lambda bi, i: (bi, i, 0)),
                   pl.BlockSpec((None, tm, kv_w), lambda bi, i: (bi, i, 0)),
                   pl.BlockSpec((None, tm, rw_w), lambda bi, i: (bi, i, 0))],
        out_shape=[jax.ShapeDtypeStruct((b, s, att_w), BF16),
                   jax.ShapeDtypeStruct((b, s, kv_w), BF16),
                   jax.ShapeDtypeStruct((b, s, kv_w), BF16),
                   jax.ShapeDtypeStruct((b, s, rw_w), F32)],
        compiler_params=_cparams("parallel", "parallel"),
    )(x, mod_l, g, w_all, cos, sin)


def _project_ctx(ctx, mod_c, g, w_ctx, kv_w, rw_w):
    b, l, d = ctx.shape
    kern = functools.partial(_proj_ctx_kernel, kv_w=kv_w)
    return pl.pallas_call(
        kern,
        grid=(b,),
        in_specs=[pl.BlockSpec((None, l, d), lambda bi: (bi, 0, 0)),
                  _const_spec((6, d)),
                  _const_spec((1, d)),
                  _const_spec((d, w_ctx.shape[1]))],
        out_specs=[pl.BlockSpec((None, l, kv_w), lambda bi: (bi, 0, 0)),
                   pl.BlockSpec((None, l, kv_w), lambda bi: (bi, 0, 0)),
                   pl.BlockSpec((None, l, rw_w), lambda bi: (bi, 0, 0))],
        out_shape=[jax.ShapeDtypeStruct((b, l, kv_w), BF16),
                   jax.ShapeDtypeStruct((b, l, kv_w), BF16),
                   jax.ShapeDtypeStruct((b, l, rw_w), F32)],
        compiler_params=_cparams("parallel"),
    )(ctx, mod_c, g, w_ctx)


def _attn_kernel(sink_ref, q_ref, kp_ref, kc_ref, kn_ref, vp_ref, vc_ref, vn_ref, kx_ref, vx_ref, o_ref, *, nsub):
    i = pl.program_id(1)
    nb = pl.num_programs(1)
    blk = ATT_BLOCK
    kall = jnp.concatenate([kp_ref[...], kc_ref[...], kn_ref[...]], axis=0)
    vall = jnp.concatenate([vp_ref[...], vc_ref[...], vn_ref[...]], axis=0)
    vall = jnp.concatenate([vall, jnp.ones_like(vall)], axis=1)
    kctx = kx_ref[...]
    vctx = jnp.concatenate([vx_ref[...], jnp.ones_like(vx_ref[...])], axis=1)
    row = lax.broadcasted_iota(jnp.int32, (2 * blk, 3 * blk), 0) % blk
    col = lax.broadcasted_iota(jnp.int32, (2 * blk, 3 * blk), 1)
    band = jnp.abs(row + blk - col) <= WINDOW
    even = lax.broadcasted_iota(jnp.int32, (blk, LANES), 1) < HEAD_DIM
    zero = jnp.zeros((), q_ref.dtype)
    tiles = range(q_ref.shape[1] // LANES)
    sinks = [sink_ref[j] for j in tiles]

    def band_masked(s, va):
        return jnp.concatenate([jnp.where(va[:, :blk], s[:, :blk], -1e30), s[:, blk:2 * blk],
                                jnp.where(va[:, 2 * blk:], s[:, 2 * blk:], -1e30)], axis=1)

    def scores(a):
        va = band
        if a == 0:
            va = va & jnp.logical_not((i == 0) & (col < blk))
        if a == nsub - 1:
            va = va & jnp.logical_not((i == nb - 1) & (col >= 2 * blk))
        kloc = kall[a * blk:(a + 3) * blk]
        qs = []
        for j in tiles:
            qj = q_ref[a * blk:(a + 1) * blk, j * LANES:(j + 1) * LANES]
            qs.append(jnp.concatenate([jnp.where(even, qj, zero), jnp.where(even, zero, qj)], axis=0))
        return [band_masked(_dot_nt(qj, kloc), va) for qj in qs], [_dot_nt(qj, kctx) for qj in qs]

    def row_max(sl, sc, sk):
        t = sk
        for s in (sl, sc):
            for c0 in range(0, s.shape[1], LANES):
                t = jnp.maximum(t, s[:, c0:c0 + LANES])
        return jnp.max(t, axis=-1, keepdims=True)

    def finish(a, s_loc, s_ctx):
        vloc = vall[a * blk:(a + 3) * blk]
        mx = [row_max(sl, sc, sk) for sl, sc, sk in zip(s_loc, s_ctx, sinks)]
        p_loc = [jnp.exp((sl - m).astype(BF16)) for sl, m in zip(s_loc, mx)]
        p_ctx = [jnp.exp((sc - m).astype(BF16)) for sc, m in zip(s_ctx, mx)]
        num = [jnp.dot(pa, vloc, preferred_element_type=F32) + jnp.dot(pc, vctx, preferred_element_type=F32)
               for pa, pc in zip(p_loc, p_ctx)]
        for j, n, sk, m in zip(tiles, num, sinks, mx):
            o = n[:, :LANES] / (n[:, LANES:] + jnp.exp(sk - m))
            o_ref[a * blk:(a + 1) * blk, j * LANES:(j + 1) * LANES] = jnp.where(even, o[:blk], o[blk:]).astype(o_ref.dtype)

    pending = scores(0)
    for a in range(nsub):
        ahead = scores(a + 1) if a + 1 < nsub else None
        finish(a, *pending)
        pending = ahead


def _attention(q, k, v, kc, vc, sink_rows):
    b, s, att_w = q.shape
    kv_w = k.shape[2]
    l = kc.shape[1]
    blk = ATT_BLOCK
    nsub = _pick_tile(s // blk, ATT_SUBBLOCKS)
    nb = s // (nsub * blk)
    kern = functools.partial(_attn_kernel, nsub=nsub)
    kv_prev = pl.BlockSpec((None, blk, kv_w), lambda bi, i: (bi, jnp.maximum(i * nsub - 1, 0), 0))
    kv_cur = pl.BlockSpec((None, nsub * blk, kv_w), lambda bi, i: (bi, i, 0))
    kv_next = pl.BlockSpec((None, blk, kv_w), lambda bi, i: (bi, jnp.minimum((i + 1) * nsub, nb * nsub - 1), 0))
    kv_ctx = pl.BlockSpec((None, l, kv_w), lambda bi, i: (bi, 0, 0))
    return pl.pallas_call(
        kern,
        grid=(b, nb),
        in_specs=[_const_spec(sink_rows.shape),
                  pl.BlockSpec((None, nsub * blk, att_w), lambda bi, i: (bi, i, 0)),
                  kv_prev, kv_cur, kv_next, kv_prev, kv_cur, kv_next, kv_ctx, kv_ctx],
        out_specs=pl.BlockSpec((None, nsub * blk, att_w), lambda bi, i: (bi, i, 0)),
        out_shape=jax.ShapeDtypeStruct((b, s, att_w), BF16),
        compiler_params=_cparams("parallel", "parallel"),
    )(sink_rows, q, k, k, k, v, v, v, kc, vc)


def _stack_heads(x, even):
    return jnp.concatenate([jnp.where(even, x, 0.0), jnp.where(even, 0.0, x)], axis=0)


def _chunk_summaries(insts):
    c2 = 2 * CHUNK
    even = lax.broadcasted_iota(jnp.int32, (CHUNK, LANES), 1) < HEAD_DIM
    rt = lax.broadcasted_iota(jnp.int32, (CHUNK, LANES), 0)
    ct = lax.broadcasted_iota(jnp.int32, (CHUNK, LANES), 1) % HEAD_DIM
    masks = {True: (ct < rt, ct <= rt), False: (ct > rt, ct >= rt)}
    eye = lax.broadcasted_iota(jnp.int32, (LANES, LANES), 0) == lax.broadcasted_iota(jnp.int32, (LANES, LANES), 1)

    zero = jnp.zeros((), BF16)

    def stacked(f):
        f = f.astype(BF16)
        return jnp.concatenate([jnp.where(even, f, zero), jnp.where(even, zero, f)], axis=0)

    prep = []
    for rr, vv, aa, bb, kd, cs, lw, tot, forward in insts:
        g_inv = jnp.exp(-cs)
        g_rem = jnp.exp(tot - cs)
        a_t = aa * jnp.exp(cs - lw)
        r_t = rr * jnp.exp(cs)
        v_s = stacked(vv)
        bk_s = jnp.concatenate([stacked(bb * g_rem), stacked(kd * g_rem)], axis=0)
        lhs = jnp.concatenate([a_t, r_t], axis=0).astype(BF16)
        rhs = jnp.concatenate([stacked(bb * g_inv), stacked(kd * g_inv)], axis=0)
        prep.append((_stack_heads(a_t, even), _stack_heads(r_t, even), v_s, bk_s, lhs, rhs, jnp.exp(tot), masks[forward]))

    gs = [_dot_nt(p[4], p[5]) for p in prep]
    tri, lfs = [], []
    for g, p in zip(gs, prep):
        strict, incl = p[7]
        lfs.append(jnp.where(strict, g[:CHUNK, :c2], 0.0).astype(BF16))
        tri.append((stacked(jnp.where(strict, g[:CHUNK, c2:], 0.0)), stacked(jnp.where(incl, g[CHUNK:, :c2], 0.0)),
                    stacked(jnp.where(incl, g[CHUNK:, c2:], 0.0))))
    own = (lax.broadcasted_iota(jnp.int32, (c2, LANES), 0) < CHUNK) == (
        lax.broadcasted_iota(jnp.int32, (c2, LANES), 1) < HEAD_DIM)
    xs = [p[0] + pltpu.roll(_dot(t[0], p[2]), HEAD_DIM, 1) for p, t in zip(prep, tri)]

    ts = [jnp.where((rt // 2 == ct // 2), lf.astype(F32), 0.0) + jnp.where(rt == ct, 1.0, 0.0) for lf in lfs]
    size = 2
    while size < CHUNK:
        couple = (rt // (2 * size) == ct // (2 * size)) & (rt // size != ct // size)
        os_ = [jnp.where(couple, lf, zero) for lf in lfs]
        ots = [_dot(o, stacked(t)) for o, t in zip(os_, ts)]
        ts = [t + _dot(t, stacked(ot)) for t, ot in zip(ts, ots)]
        size *= 2
    xs = [_dot(stacked(t), x) for t, x in zip(ts, xs)]
    rys = [p[1] + pltpu.roll(_dot(t[2], p[2]), HEAD_DIM, 1) + _dot(t[1], x)
           for p, t, x in zip(prep, tri, xs)]
    out = []
    for p, x, ry in zip(prep, xs, rys):
        v_s, bk_s, g_tot = p[2], p[3], p[6]
        x_a = jnp.where(own, x, 0.0).astype(BF16)
        x_u = pltpu.roll(jnp.where(own, 0.0, x), HEAD_DIM, 1).astype(BF16)
        pt = jnp.where(eye, g_tot, 0.0) + _dot_tn(bk_s[:c2], x_a)
        qt = _dot_tn(bk_s, jnp.concatenate([x_u, v_s], axis=0))
        rh = jnp.where(even, ry[:CHUNK], ry[CHUNK:])
        yh = pltpu.roll(jnp.where(even, ry[CHUNK:], ry[:CHUNK]), HEAD_DIM, 1)
        out.append((pt[:CHUNK] + pt[CHUNK:], qt[:CHUNK] + qt[CHUNK:], rh, yh))
    return out


def _feat_kernel(pr_ref, hp_ref, hn_ref, mu_ref, w0_ref, a0_ref, w2_ref, a2_ref, g2_ref, kkp_ref, ka_ref, rk_ref,
                 ones_ref, pt_ref, qt_ref, rh_ref, yh_ref, bonus_ref, gate_ref, *, rw_w):
    i = pl.program_id(1)
    last = pl.num_programs(1) - 1
    tt = pr_ref.shape[0]
    pr = pr_ref[...]
    ridx = lax.broadcasted_iota(jnp.int32, (tt, 1), 0)
    edge_prev = jnp.where(i == 0, 0.0, hp_ref[SUBLANES - 1:SUBLANES, :])
    edge_next = jnp.where(i == last, 0.0, hn_ref[0:1, :])
    prev = jnp.where(ridx == 0, edge_prev, pltpu.roll(pr, 1, 0))
    nxt = jnp.where(ridx == tt - 1, edge_next, pltpu.roll(pr, tt - 1, 0))
    mu = mu_ref[...]
    x = pr * (1.0 - mu) + (0.5 * mu) * (prev + nxt)

    r = x[:, 0:rw_w]
    k = x[:, rw_w:2 * rw_w]
    v = x[:, 2 * rw_w:3 * rw_w]
    o = 3 * rw_w
    wd = x[:, o:o + LANES]
    ad = x[:, o + LANES:o + 2 * LANES]
    gd = x[:, o + 2 * LANES:o + 3 * LANES]

    ones_bd = ones_ref[...]
    logw = -DECAY_SCALE * _sigmoid(_dot(jnp.tanh(wd), w2_ref[...]) + w0_ref[...])
    iclr = _sigmoid(_dot(ad, a2_ref[...]) + a0_ref[...])
    gate_ref[...] = _dot(_sigmoid(gd), g2_ref[...])

    kkv = k * kkp_ref[...]
    kk_sq = jnp.dot((kkv * kkv).astype(BF16), ones_bd, preferred_element_type=F32)
    kk = kkv / jnp.maximum(jnp.sqrt(kk_sq), 1e-12)
    ka = ka_ref[...]
    k_dir = [k * (1.0 + (iclr[:, d * rw_w:(d + 1) * rw_w] - 1.0) * ka) for d in range(2)]
    bonus_ref[...] = _head_sum(r * (k_dir[0] + k_dir[1]) * rk_ref[...], ones_bd) * v

    row = lax.broadcasted_iota(jnp.int32, (tt, tt), 0)
    col = lax.broadcasted_iota(jnp.int32, (tt, tt), 1)
    same = (row // CHUNK) == (col // CHUNK)
    tri = [jnp.where(same & (col <= row), 1.0, 0.0).astype(BF16),
           jnp.where(same & (col >= row), 1.0, 0.0).astype(BF16)]
    neg_kk = -kk
    lw_dir, cs_dir, b_dir = [], [], []
    for d in range(2):
        lw_d = logw[:, d * rw_w:(d + 1) * rw_w]
        lw_dir.append(lw_d)
        cs_dir.append(sum(jnp.dot(tri[d], p, preferred_element_type=F32) for p in _split2(lw_d)))
        b_dir.append(kk * iclr[:, d * rw_w:(d + 1) * rw_w])
    nchunks = tt // CHUNK
    group = _pick_tile(nchunks, FEAT_CHUNK_GROUP)
    for c0 in range(0, nchunks, group):
        keys, insts = [], []
        for c in range(c0, c0 + group):
            rows = slice(c * CHUNK, (c + 1) * CHUNK)
            for d in range(2):
                end = (c + 1) * CHUNK - 1 if d == 0 else c * CHUNK
                for p in range(rw_w // LANES):
                    ln = slice(p * LANES, (p + 1) * LANES)
                    keys.append((c, d, p, rows, ln))
                    insts.append((r[rows, ln], v[rows, ln], neg_kk[rows, ln], b_dir[d][rows, ln], k_dir[d][rows, ln],
                                  cs_dir[d][rows, ln], lw_dir[d][rows, ln], cs_dir[d][end:end + 1, ln], d == 0))
        for (c, d, p, rows, ln), (pt, qt, rh, yh) in zip(keys, _chunk_summaries(insts)):
            pt_ref[c, d, p] = pt
            qt_ref[c, d, p] = qt
            rh_ref[d, rows, ln] = rh.astype(rh_ref.dtype)
            yh_ref[d, rows, ln] = yh.astype(yh_ref.dtype)


def _rwkv_features(pr, fp, tt):
    b, t, w_all = pr.shape
    rw_w = fp["kkp"].shape[1]
    npair = rw_w // LANES
    nt = t // tt
    cpt = tt // CHUNK
    hb = tt // SUBLANES
    kern = functools.partial(_feat_kernel, rw_w=rw_w)
    names = ("mu", "w0", "a0", "w2", "a2", "g2", "kkp", "ka", "rk", "ones")
    return pl.pallas_call(
        kern,
        grid=(b, nt),
        in_specs=[pl.BlockSpec((None, tt, w_all), lambda bi, i: (bi, i, 0)),
                  pl.BlockSpec((None, SUBLANES, w_all), lambda bi, i: (bi, jnp.maximum(i * hb - 1, 0), 0)),
                  pl.BlockSpec((None, SUBLANES, w_all), lambda bi, i: (bi, jnp.minimum((i + 1) * hb, nt * hb - 1), 0))]
                 + [_const_spec(fp[n].shape) for n in names],
        out_specs=[pl.BlockSpec((None, cpt, 2, npair, CHUNK, LANES), lambda bi, i: (bi, i, 0, 0, 0, 0)),
                   pl.BlockSpec((None, cpt, 2, npair, CHUNK, LANES), lambda bi, i: (bi, i, 0, 0, 0, 0)),
                   pl.BlockSpec((None, 2, tt, rw_w), lambda bi, i: (bi, 0, i, 0)),
                   pl.BlockSpec((None, 2, tt, rw_w), lambda bi, i: (bi, 0, i, 0)),
                   pl.BlockSpec((None, tt, rw_w), lambda bi, i: (bi, i, 0)),
                   pl.BlockSpec((None, tt, rw_w), lambda bi, i: (bi, i, 0))],
        out_shape=[jax.ShapeDtypeStruct((b, t // CHUNK, 2, npair, CHUNK, LANES), F32),
                   jax.ShapeDtypeStruct((b, t // CHUNK, 2, npair, CHUNK, LANES), F32),
                   jax.ShapeDtypeStruct((b, 2, t, rw_w), BF16),
                   jax.ShapeDtypeStruct((b, 2, t, rw_w), BF16),
                   jax.ShapeDtypeStruct((b, t, rw_w), F32),
                   jax.ShapeDtypeStruct((b, t, rw_w), F32)],
        compiler_params=_cparams("parallel", "parallel"),
    )(pr, pr, pr, *[fp[n] for n in names])


def _scan_kernel(s0_ref, ptf_ref, qtf_ref, ptb_ref, qtb_ref, rhf_ref, rhb_ref,
                 yf_ref, yb_ref, sfin_ref, st_ref, *, cps, npair):
    i = pl.program_id(1)

    @pl.when(i == 0)
    def _():
        st_ref[...] = s0_ref[...]

    even = lax.broadcasted_iota(jnp.int32, (CHUNK, LANES), 1) < HEAD_DIM
    zero = jnp.zeros((), BF16)
    dirs = ((ptf_ref, qtf_ref, rhf_ref, yf_ref), (ptb_ref, qtb_ref, rhb_ref, yb_ref))
    keys = [(d, p) for d in range(2) for p in range(npair)]
    st = [st_ref[d, p] for d, p in keys]
    for step in range(cps):
        hl = []
        for s in st:
            hi = s.astype(BF16)
            lo = (s - hi.astype(F32)).astype(BF16)
            hl.append(jnp.concatenate([jnp.concatenate([jnp.where(even, hi, zero), jnp.where(even, zero, hi)], axis=0),
                                       jnp.concatenate([jnp.where(even, lo, zero), jnp.where(even, zero, lo)], axis=0)],
                                      axis=1))
        new = []
        for (d, p), s2 in zip(keys, hl):
            pt_ref, qt_ref, rh_ref, y_ref = dirs[d]
            c = step if d == 0 else cps - 1 - step
            rows = slice(c * CHUNK, (c + 1) * CHUNK)
            ln = slice(p * LANES, (p + 1) * LANES)
            y_ref[rows, ln] = jnp.dot(rh_ref[rows, ln], s2[:, :LANES], preferred_element_type=F32).astype(y_ref.dtype)
            pt_hi, pt_lo = _split2(pt_ref[c, p])
            n2 = jnp.dot(pt_hi, s2, preferred_element_type=F32)
            new.append(n2[:, :LANES] + n2[:, LANES:] + jnp.dot(pt_lo, s2[:, :LANES], preferred_element_type=F32)
                       + qt_ref[c, p])
        st = new
    for (d, p), s in zip(keys, st):
        st_ref[d, p] = s

    @pl.when(i == pl.num_programs(1) - 1)
    def _():
        sfin_ref[...] = st_ref[...]


def _rwkv_scan(s0, pt, qt, rh, cps):
    b, nc, _, npair, _, _ = pt.shape
    t, rw_w = rh.shape[2], rh.shape[3]
    ns = nc // cps
    ts = cps * CHUNK
    kern = functools.partial(_scan_kernel, cps=cps, npair=npair)
    mat_f = pl.BlockSpec((None, cps, None, npair, CHUNK, LANES), lambda bi, i: (bi, i, 0, 0, 0, 0))
    mat_b = pl.BlockSpec((None, cps, None, npair, CHUNK, LANES), lambda bi, i: (bi, ns - 1 - i, 1, 0, 0, 0))
    tok_f = pl.BlockSpec((None, None, ts, rw_w), lambda bi, i: (bi, 0, i, 0))
    tok_b = pl.BlockSpec((None, None, ts, rw_w), lambda bi, i: (bi, 1, ns - 1 - i, 0))
    state = pl.BlockSpec((None, 2, npair, CHUNK, LANES), lambda bi, i: (bi, 0, 0, 0, 0))
    return pl.pallas_call(
        kern,
        grid=(b, ns),
        in_specs=[state, mat_f, mat_f, mat_b, mat_b, tok_f, tok_b],
        out_specs=[pl.BlockSpec((None, ts, rw_w), lambda bi, i: (bi, i, 0)),
                   pl.BlockSpec((None, ts, rw_w), lambda bi, i: (bi, ns - 1 - i, 0)),
                   state],
        out_shape=[jax.ShapeDtypeStruct((b, t, rw_w), BF16),
                   jax.ShapeDtypeStruct((b, t, rw_w), BF16),
                   jax.ShapeDtypeStruct((b, 2, npair, CHUNK, LANES), F32)],
        scratch_shapes=[pltpu.VMEM((2, npair, CHUNK, LANES), F32)],
        compiler_params=_cparams("parallel", "arbitrary"),
    )(s0, pt, qt, pt, qt, rh, rh)


def _residual_mlp(xl, yl, m, ng, w1_ref, w2_ref):
    x2 = xl + _rms(yl, m[2:3] * ng[1:2])
    hm = (_rms(x2, ng[2:3] * (1.0 + m[4:5])) + m[3:4]).astype(BF16)
    out = None
    for j in range(w1_ref.shape[1] // MLP_HIDDEN_BLOCK):
        cols = slice(j * MLP_HIDDEN_BLOCK, (j + 1) * MLP_HIDDEN_BLOCK)
        hid = jnp.maximum(jnp.dot(hm, w1_ref[:, cols], preferred_element_type=F32), 0.0)
        part = jnp.dot((hid * hid).astype(BF16), w2_ref[cols, :], preferred_element_type=F32)
        out = part if out is None else out + part
    return x2 + _rms(out, m[5:6] * ng[3:4])


def _mix_out_kernel(x_ref, att_ref, yf_ref, yb_ref, yhf_ref, yhb_ref, bonus_ref, gate_ref, mod_ref, ng_ref, lnw_ref,
                    lnb_ref, ones_ref, wo_ref, w1_ref, w2_ref, o_ref, *, att_w):
    ones_bd = ones_ref[...]
    y = (yf_ref[...].astype(F32) + yb_ref[...].astype(F32)) + (yhf_ref[...].astype(F32) + yhb_ref[...].astype(F32))
    mean = _head_sum(y, ones_bd) * (1.0 / HEAD_DIM)
    yc = y - mean
    var = jnp.dot((yc * yc).astype(BF16), ones_bd, preferred_element_type=F32) * (1.0 / HEAD_DIM)
    yn = yc * lax.rsqrt(var + GN_EPS) * lnw_ref[...] + lnb_ref[...]
    rw = (yn + bonus_ref[...]) * gate_ref[...]
    yl = (jnp.dot(att_ref[...], wo_ref[0:att_w, :], preferred_element_type=F32)
          + jnp.dot(rw.astype(BF16), wo_ref[att_w:, :], preferred_element_type=F32))
    o_ref[...] = _residual_mlp(x_ref[...], yl, mod_ref[...], ng_ref[...], w1_ref, w2_ref)


def _mix_out_mlp(x, att, yf, yb, yh, bonus, gate, mod_l, ng, lnw, lnb, ones_bd, wo, w1, w2, tm):
    b, s, d = x.shape
    att_w = att.shape[2]
    rw_w = yf.shape[2]
    kern = functools.partial(_mix_out_kernel, att_w=att_w)
    tok = lambda w: pl.BlockSpec((None, tm, w), lambda bi, i: (bi, i, 0))
    yh_dir = lambda dd: pl.BlockSpec((None, None, tm, rw_w), lambda bi, i: (bi, dd, i, 0))
    return pl.pallas_call(
        kern,
        grid=(b, s // tm),
        in_specs=[tok(d), tok(att_w), tok(rw_w), tok(rw_w), yh_dir(0), yh_dir(1), tok(rw_w), tok(rw_w),
                  pl.BlockSpec((None, 6, d), lambda bi, i: (bi, 0, 0)),
                  _const_spec(ng.shape), _const_spec(lnw.shape), _const_spec(lnb.shape), _const_spec(ones_bd.shape),
                  _const_spec(wo.shape), _const_spec(w1.shape), _const_spec(w2.shape)],
        out_specs=tok(d),
        out_shape=jax.ShapeDtypeStruct((b, s, d), F32),
        compiler_params=_cparams("parallel", "parallel"),
    )(x, att, yf, yb, yh, yh, bonus, gate, mod_l, ng, lnw, lnb, ones_bd, wo, w1, w2)


def _conv_kernel(x_ref, xp_ref, xn_ref, mod_ref, ng_ref, wi_ref, cw_ref, wo_ref, w1_ref, w2_ref, o_ref):
    i = pl.program_id(1)
    last = pl.num_programs(1) - 1
    d = x_ref.shape[1]
    tm = x_ref.shape[0]
    m = mod_ref[...]
    ng = ng_ref[...]

    def modnorm(u):
        return (_rms(u, ng[0:1] * (1.0 + m[1:2])) + m[0:1]).astype(BF16)

    x = x_ref[...]
    rows = jnp.concatenate([x, xp_ref[...], xn_ref[...]], axis=0)
    proj = jnp.dot(modnorm(rows), wi_ref[...], preferred_element_type=F32)
    z_all = proj[:, d:2 * d] * proj[:, 2 * d:]
    z = z_all[:tm]
    zp = jnp.where(i == 0, 0.0, z_all[tm + SUBLANES - 1:tm + SUBLANES, :])
    zn = jnp.where(i == last, 0.0, z_all[tm + SUBLANES:tm + SUBLANES + 1, :])
    ridx = lax.broadcasted_iota(jnp.int32, (tm, 1), 0)
    prev = jnp.where(ridx == 0, zp, pltpu.roll(z, 1, 0))
    nxt = jnp.where(ridx == tm - 1, zn, pltpu.roll(z, tm - 1, 0))
    cw = cw_ref[...]
    y = proj[:tm, :d] * (prev * cw[0:1] + z * cw[1:2] + nxt * cw[2:3])
    yl = jnp.dot(y.astype(BF16), wo_ref[...], preferred_element_type=F32)
    o_ref[...] = _residual_mlp(x, yl, m, ng, w1_ref, w2_ref)


def _conv_mlp(x, mod_l, ng, wi, cw, wo, w1, w2, tm):
    b, s, d = x.shape
    hb = tm // SUBLANES
    nt = s // tm
    return pl.pallas_call(
        _conv_kernel,
        grid=(b, nt),
        in_specs=[pl.BlockSpec((None, tm, d), lambda bi, i: (bi, i, 0)),
                  pl.BlockSpec((None, SUBLANES, d), lambda bi, i: (bi, jnp.maximum(i * hb - 1, 0), 0)),
                  pl.BlockSpec((None, SUBLANES, d), lambda bi, i: (bi, jnp.minimum((i + 1) * hb, nt * hb - 1), 0)),
                  pl.BlockSpec((None, 6, d), lambda bi, i: (bi, 0, 0)),
                  _const_spec(ng.shape), _const_spec(wi.shape), _const_spec(cw.shape), _const_spec(wo.shape),
                  _const_spec(w1.shape), _const_spec(w2.shape)],
        out_specs=pl.BlockSpec((None, tm, d), lambda bi, i: (bi, i, 0)),
        out_shape=jax.ShapeDtypeStruct((b, s, d), F32),
        compiler_params=_cparams("parallel", "parallel"),
    )(x, x, x, mod_l, ng, wi, cw, wo, w1, w2)


def _block_diag2(m):
    z = jnp.zeros_like(m[0])
    return jnp.concatenate([jnp.concatenate([m[0], z], axis=1), jnp.concatenate([z, m[1]], axis=1)], axis=0)


def _pick_tile(n, want):
    t = min(n, want)
    while n % t:
        t //= 2
    return t


def kernel(x, c, ctx, c_ctx, mod_w, mod_b, norm_g, mlp_w1, mlp_w2, ab_w_in, ab_w_out, att_sink, rwkv_mu, rwkv_w0,
           rwkv_w2, rwkv_a0, rwkv_a2, rwkv_g2, rwkv_kk, rwkv_ka, rwkv_rk, rwkv_ln_w, rwkv_ln_b, conv_w_in, conv_w,
           conv_w_out):
    b, s, d = x.shape
    l = ctx.shape[1]
    depth = mod_w.shape[0]
    assert depth == 2, "layer schedule below is written for one attention/RWKV layer followed by one conv layer"
    att_w = att_sink.shape[1] * HEAD_DIM
    kv_w = att_w // ATT_GROUP
    rw_w = rwkv_kk.shape[1]
    rw_in = rwkv_mu.shape[1]
    assert s % ATT_BLOCK == 0 and s % CHUNK == 0 and l % CHUNK == 0 and rw_w % LANES == 0
    assert kv_w == LANES and WINDOW == ATT_BLOCK

    rows = -(-(b + 1) // SUBLANES) * SUBLANES
    cv = jnp.concatenate([c, c_ctx[None, :], jnp.zeros((rows - b - 1, d), F32)], axis=0)
    mod = _modulation(cv, mod_w, mod_b)
    mod_l = [mod[i, :b].reshape(b, 6, d) for i in range(depth)]
    mod_c0 = mod[0, b].reshape(6, d)

    w_in = ab_w_in[0]
    wq, wk, wv, wr = (w_in[:, :att_w], w_in[:, att_w:att_w + kv_w], w_in[:, att_w + kv_w:att_w + 2 * kv_w],
                      w_in[:, att_w + 2 * kv_w:])
    n_heads = att_w // HEAD_DIM
    perm = jnp.arange(n_heads).reshape(n_heads // ATT_GROUP, ATT_GROUP).T.reshape(-1)
    wq = wq.reshape(d, n_heads, HEAD_DIM)[:, perm].reshape(d, att_w)
    w_lat = jnp.concatenate([wq, _rot_cols(wq), wk, _rot_cols(wk), wv, wr], axis=1).astype(BF16)
    w_ctx = jnp.concatenate([wk, wv, wr], axis=1).astype(BF16)
    cos, sin = _rope_tables(s)
    g0 = norm_g[0, 0].reshape(1, d)
    q, k, v, pr = _project_latent(x, mod_l[0], g0, w_lat, cos, sin, att_w, kv_w, rw_in, _pick_tile(s, PROJ_ROWS))
    kc, vc, prc = _project_ctx(ctx, mod_c0, g0, w_ctx, kv_w, rw_in)
    sink_rows = jnp.repeat(att_sink[0][perm].reshape(att_w // LANES, LANES // HEAD_DIM), ATT_BLOCK, axis=1)
    sink_rows = jnp.broadcast_to(sink_rows[..., None], sink_rows.shape + (LANES,))
    att = _attention(q, k, v, kc, vc, sink_rows)
    w_out = ab_w_out[0]
    w_out = jnp.concatenate([w_out[:att_w].reshape(n_heads, HEAD_DIM, d)[perm].reshape(att_w, d), w_out[att_w:]], axis=0)

    head_id = jnp.arange(rw_w) // HEAD_DIM
    ones_bd = (head_id[:, None] == head_id[None, :]).astype(BF16)
    fp = dict(mu=rwkv_mu[0].reshape(1, rw_in),
              w0=rwkv_w0[0].reshape(1, 2 * rw_w), a0=rwkv_a0[0].reshape(1, 2 * rw_w),
              w2=_block_diag2(rwkv_w2[0]).astype(BF16), a2=_block_diag2(rwkv_a2[0]).astype(BF16),
              g2=rwkv_g2[0].astype(BF16),
              kkp=rwkv_kk[0].reshape(1, rw_w), ka=rwkv_ka[0].reshape(1, rw_w), rk=rwkv_rk[0].reshape(1, rw_w),
              ones=ones_bd)
    npair = rw_w // LANES
    ptc, qtc, rhc, yhc, _, _ = _rwkv_features(prc, fp, _pick_tile(l, FEAT_ROWS))
    zero_state = jnp.zeros((b, 2, npair, CHUNK, LANES), F32)
    _, _, s_ctx = _rwkv_scan(zero_state, ptc, qtc, rhc, _pick_tile(l // CHUNK, SCAN_CHUNKS))
    pt, qt, rh, yh, bonus, gate = _rwkv_features(pr, fp, _pick_tile(s, FEAT_ROWS))
    yf, yb, _ = _rwkv_scan(s_ctx, pt, qt, rh, _pick_tile(s // CHUNK, SCAN_CHUNKS))

    xl = _mix_out_mlp(x, att, yf, yb, yh, bonus, gate, mod_l[0], norm_g[0],
                      rwkv_ln_w[0].reshape(1, rw_w), rwkv_ln_b[0].reshape(1, rw_w), ones_bd,
                      w_out.astype(BF16), mlp_w1[0].astype(BF16), mlp_w2[0].astype(BF16), _pick_tile(s, MLP_ROWS))

    return _conv_mlp(xl, mod_l[1], norm_g[1], conv_w_in[0].astype(BF16), conv_w[0], conv_w_out[0].astype(BF16),
                     mlp_w1[1].astype(BF16), mlp_w2[1].astype(BF16), _pick_tile(s, MLP_ROWS))
```
